```python
import math
import jax, jax.numpy as jnp
from jax import lax
import numpy as np

D_MODEL = 1024
BATCH = 4
SEQ = 4096
DEPTH = 4
DEC_BATCH = 32
DEC_SEQ = 4
PAST_LEN = 8192
PAGE_SIZE = 128

D_MIX = D_MODEL
GLA_HEADS = 4
GLA_WIDTH = D_MIX // 4
GLA_DV = GLA_WIDTH // GLA_HEADS
GLA_DK = GLA_DV // 2
GLA_RANK = 16
GLA_TAU = 16.0
GLA_CHUNK = 64
SWA_WIDTH = D_MIX // 2
SWA_HEADS = 8
SWA_HD = SWA_WIDTH // SWA_HEADS
DILATED_PATTERNS = ((128, 1), (512, 4), (2048, 16))
SWA_WMAX = max(w for w, _ in DILATED_PATTERNS)
SWA_QBLOCK = 128
ROPE_THETA = 500000.0
ROPE_DIMS = SWA_HD // 4
SSM_WIDTH = D_MIX // 4
SSM_GROUP = 16
SSM_GROUPS = SSM_WIDTH // SSM_GROUP
SSM_STATE = 64
NORM_EPS = 1e-6

IN_SPLITS = (GLA_HEADS * GLA_DK, GLA_HEADS * GLA_DK, GLA_WIDTH, GLA_RANK, GLA_WIDTH,
             SWA_WIDTH, SWA_WIDTH, SWA_WIDTH, SWA_WIDTH,
             SSM_WIDTH, SSM_WIDTH)
IN_COLS = sum(IN_SPLITS)
IN_OFFSETS = tuple(int(v) for v in np.cumsum(IN_SPLITS)[:-1])

kernel_name = "hymba_gla_dilated_s5_decoder_step"

F32 = jnp.float32


def rms_norm(x, g):
    xf = x.astype(F32)
    y = xf * lax.rsqrt(jnp.mean(xf * xf, axis=-1, keepdims=True) + NORM_EPS)
    return (y * g.astype(F32)).astype(x.dtype)


def partial_rope(x, pos):
    half = ROPE_DIMS // 2
    inv = ROPE_THETA ** (-jnp.arange(half, dtype=F32) * (2.0 / ROPE_DIMS))
    ang = pos.astype(F32)[:, None] * inv[None, :]
    cos = jnp.cos(ang)[None, :, None, :]
    sin = jnp.sin(ang)[None, :, None, :]
    xf = x.astype(F32)
    x1, x2 = xf[..., :half], xf[..., half:ROPE_DIMS]
    out = jnp.concatenate([x1 * cos - x2 * sin, x2 * cos + x1 * sin, xf[..., ROPE_DIMS:]], axis=-1)
    return out.astype(x.dtype)


def gla_mix(q, k, v, g, s0):
    Bn, T, H, DK = q.shape
    DV = v.shape[-1]
    C = GLA_CHUNK if T % GLA_CHUNK == 0 else T
    nc = T // C
    q = q.astype(F32).reshape(Bn, nc, C, H, DK) * (DK ** -0.5)
    k = k.astype(F32).reshape(Bn, nc, C, H, DK)
    v = v.astype(F32).reshape(Bn, nc, C, H, DV)
    b = jnp.cumsum(g.astype(F32).reshape(Bn, nc, C, H, DK), axis=2)
    b_last = b[:, :, -1]
    qd = q * jnp.exp(b)
    kd = k * jnp.exp(-b)
    causal = jnp.tril(jnp.ones((C, C), dtype=bool))
    att = jnp.where(causal, jnp.einsum('bnihd,bnjhd->bnhij', qd, kd), 0.0)
    intra = jnp.einsum('bnhij,bnjhe->bnihe', att, v)
    ds = jnp.einsum('bnjhd,bnjhe->bnhde', k * jnp.exp(b_last[:, :, None] - b), v)
    decay = jnp.exp(b_last)

    def step(s, inp):
        dec, d = inp
        return dec[..., None] * s + d, s

    s_fin, s_prev = lax.scan(step, s0.astype(F32), (jnp.swapaxes(decay, 0, 1), jnp.swapaxes(ds, 0, 1)))
    s_prev = jnp.swapaxes(s_prev, 0, 1)
    inter = jnp.einsum('bnihd,bnhde->bnihe', qd, s_prev)
    return (inter + intra).reshape(Bn, T, H, DV), s_fin


def dilated_window_attn(q, k_ext, v_ext, n_invalid):
    Bn, T, H, hd = q.shape
    QB = SWA_QBLOCK if T % SWA_QBLOCK == 0 else T
    nb = T // QB
    scale = hd ** -0.5
    qi = jnp.arange(QB)

    def block(bi):
        start = bi * QB
        qb = lax.dynamic_slice_in_dim(q, start, QB, axis=1).astype(F32) * scale
        kb = lax.dynamic_slice_in_dim(k_ext, start, SWA_WMAX + QB, axis=1)
        vb = lax.dynamic_slice_in_dim(v_ext, start, SWA_WMAX + QB, axis=1)
        outs, lses = [], []
        for window, dil in DILATED_PATTERNS:
            offs = jnp.arange(0, window + 1, dil)
            idx = SWA_WMAX + qi[:, None] - offs[None, :]
            valid = (start + idx) >= n_invalid
            kg = kb[:, idx].astype(F32)
            vg = vb[:, idx].astype(F32)
            s = jnp.einsum('bqhd,bqjhd->bqhj', qb, kg)
            s = jnp.where(valid[None, :, None, :], s, -jnp.inf)
            m = jnp.max(s, axis=-1, keepdims=True)
            e = jnp.exp(s - m)
            den = jnp.sum(e, axis=-1)
            outs.append(jnp.einsum('bqhj,bqjhd->bqhd', e, vg) / den[..., None])
            lses.append(m[..., 0] + jnp.log(den))
        w = jax.nn.softmax(jnp.stack(lses, axis=0), axis=0)
        return jnp.sum(w[..., None] * jnp.stack(outs, axis=0), axis=0)

    out = lax.map(block, jnp.arange(nb))
    return jnp.transpose(out, (1, 0, 2, 3, 4)).reshape(Bn, T, H, hd)


def s5_mix(u, x0_re, x0_im, lam_re, lam_im, log_dt, b_re, b_im, c_re, c_im, d_skip, w_glu, b_glu):
    Bn, T, _ = u.shape
    uf = u.astype(F32)
    ug = uf.reshape(Bn, T, SSM_GROUPS, SSM_GROUP)
    lam = lax.complex(lam_re.astype(F32), lam_im.astype(F32))
    dt = jnp.exp(log_dt.astype(F32))[:, None]
    lam_bar = jnp.exp(lam * dt)
    b_bar = ((lam_bar - 1.0) / lam)[..., None] * lax.complex(b_re.astype(F32), b_im.astype(F32))
    bu = jnp.einsum('btgc,gpc->btgp', ug.astype(jnp.complex64), b_bar)
    x0 = lax.complex(x0_re.astype(F32), x0_im.astype(F32))
    bu = bu.at[:, 0].add(lam_bar[None] * x0)
    a = jnp.broadcast_to(lam_bar, bu.shape)

    def combine(l, r):
        return (l[0] * r[0], r[0] * l[1] + r[1])

    _, xs = lax.associative_scan(combine, (a, bu), axis=1)
    y = (jnp.einsum('btgp,gcp->btgc', jnp.real(xs), c_re.astype(F32))
         - jnp.einsum('btgp,gcp->btgc', jnp.imag(xs), c_im.astype(F32)))
    y = y.reshape(Bn, T, SSM_WIDTH) + d_skip.astype(F32) * uf
    z = jax.nn.gelu(y)
    out = z * jax.nn.sigmoid(z @ w_glu.astype(F32) + b_glu.astype(F32))
    x_last = xs[:, -1]
    return out, jnp.real(x_last), jnp.imag(x_last)


def layer(x, c, pos0, gla_state, k_ctx, v_ctx, ssm_re, ssm_im,
          w_ada, b_ada, g_pre, g_post, w_in, w_gla_lr, b_gla_lr, g_gla,
          lam_re, lam_im, log_dt, b_re, b_im, c_re, c_im, d_skip, w_glu, b_glu, w_out):
    Bn, T, _ = x.shape
    dt = x.dtype
    mod = jax.nn.silu(c.astype(F32)) @ w_ada.astype(F32) + b_ada.astype(F32)
    shift, scale, gate = jnp.split(mod, 3, axis=-1)
    h = rms_norm(x, g_pre).astype(F32)
    h = (h * (1.0 + scale[:, None]) + shift[:, None]).astype(dt)
    proj = h @ w_in
    aq, ak, av, alr, agate, bq, bk, bv, bgate, cu, cgate = jnp.split(proj, IN_OFFSETS, axis=-1)

    glog = jax.nn.log_sigmoid((alr @ w_gla_lr + b_gla_lr).astype(F32)) / GLA_TAU
    o_a, gla_new = gla_mix(aq.reshape(Bn, T, GLA_HEADS, GLA_DK), ak.reshape(Bn, T, GLA_HEADS, GLA_DK),
                           av.reshape(Bn, T, GLA_HEADS, GLA_DV), glog.reshape(Bn, T, GLA_HEADS, GLA_DK),
                           gla_state)
    o_a = rms_norm(o_a, g_gla.reshape(GLA_HEADS, GLA_DV)).reshape(Bn, T, GLA_WIDTH)
    o_a = (o_a * jax.nn.silu(agate.astype(F32))).astype(dt)

    pos = pos0 + jnp.arange(T)
    q = partial_rope(bq.reshape(Bn, T, SWA_HEADS, SWA_HD), pos)
    k = partial_rope(bk.reshape(Bn, T, SWA_HEADS, SWA_HD), pos)
    v = bv.reshape(Bn, T, SWA_HEADS, SWA_HD)
    Lc = k_ctx.shape[1]
    pad = jnp.zeros((Bn, SWA_WMAX - Lc, SWA_HEADS, SWA_HD), dtype=k.dtype)
    k_ext = jnp.concatenate([pad, k_ctx.astype(k.dtype), k], axis=1)
    v_ext = jnp.concatenate([pad, v_ctx.astype(v.dtype), v], axis=1)
    o_b = dilated_window_attn(q, k_ext, v_ext, SWA_WMAX - Lc).reshape(Bn, T, SWA_WIDTH)
    o_b = (o_b * jax.nn.silu(bgate.astype(F32))).astype(dt)
    keep = min(SWA_WMAX, Lc + T)
    n_ext = SWA_WMAX + T
    k_buf = k_ext[:, n_ext - keep:]
    v_buf = v_ext[:, n_ext - keep:]

    o_c, s_re, s_im = s5_mix(cu, ssm_re, ssm_im, lam_re, lam_im, log_dt, b_re, b_im, c_re, c_im,
                             d_skip, w_glu, b_glu)
    o_c = (o_c * jax.nn.silu(cgate.astype(F32))).astype(dt)

    mixed = jnp.concatenate([o_a, o_b, o_c], axis=-1)
    y = rms_norm(mixed @ w_out, g_post).astype(F32)
    x_new = (x.astype(F32) + gate[:, None] * y).astype(dt)
    return x_new, gla_new, k_buf, v_buf, s_re, s_im


def setup_inputs(seed: int = 0) -> dict:
    key = jax.random.key(seed)
    ks = jax.random.split(key, 32)
    L_BUF = min(SWA_WMAX, PAST_LEN)
    nrm = lambda k, shape, s=1.0: jax.random.normal(k, shape, F32) * s
    inp = {}
    inp['x_prompt'] = nrm(ks[0], (BATCH, SEQ, D_MODEL))
    inp['x_sample'] = nrm(ks[1], (DEC_BATCH, DEC_SEQ, D_MODEL))
    inp['c_prompt'] = nrm(ks[2], (BATCH, D_MODEL))
    inp['c_sample'] = nrm(ks[3], (DEC_BATCH, D_MODEL))
    inp['state_gla'] = nrm(ks[4], (DEPTH, DEC_BATCH, GLA_HEADS, GLA_DK, GLA_DV), 0.5)
    inp['cache_swa_k'] = nrm(ks[5], (DEPTH, DEC_BATCH, L_BUF, SWA_HEADS, SWA_HD))
    inp['cache_swa_v'] = nrm(ks[6], (DEPTH, DEC_BATCH, L_BUF, SWA_HEADS, SWA_HD))
    inp['state_ssm_re'] = nrm(ks[7], (DEPTH, DEC_BATCH, SSM_GROUPS, SSM_STATE), 0.1)
    inp['state_ssm_im'] = nrm(ks[8], (DEPTH, DEC_BATCH, SSM_GROUPS, SSM_STATE), 0.1)
    inp['w_ada'] = nrm(ks[9], (DEPTH, D_MODEL, 3 * D_MODEL), 0.5 / math.sqrt(D_MODEL))
    inp['b_ada'] = nrm(ks[10], (DEPTH, 3 * D_MODEL), 0.02)
    inp['g_pre'] = 1.0 + nrm(ks[11], (DEPTH, D_MODEL), 0.05)
    inp['g_post'] = 1.0 + nrm(ks[12], (DEPTH, D_MODEL), 0.05)
    inp['w_in'] = nrm(ks[13], (DEPTH, D_MODEL, IN_COLS), 1.0 / math.sqrt(D_MODEL))
    inp['w_gla_lr'] = nrm(ks[14], (DEPTH, GLA_RANK, GLA_HEADS * GLA_DK), 1.0 / math.sqrt(GLA_RANK))
    inp['b_gla_lr'] = nrm(ks[15], (DEPTH, GLA_HEADS * GLA_DK), 0.1)
    inp['g_gla'] = 1.0 + nrm(ks[16], (DEPTH, GLA_WIDTH), 0.05)
    inp['ssm_lambda_re'] = -0.5 + nrm(ks[17], (DEPTH, SSM_GROUPS, SSM_STATE), 0.01)
    inp['ssm_lambda_im'] = (math.pi * jnp.arange(SSM_STATE, dtype=F32))[None, None, :] + nrm(ks[18], (DEPTH, SSM_GROUPS, SSM_STATE), 0.01)
    inp['ssm_log_dt'] = jax.random.uniform(ks[19], (DEPTH, SSM_GROUPS), F32, math.log(1e-3), math.log(1e-1))
    inp['ssm_b_re'] = nrm(ks[20], (DEPTH, SSM_GROUPS, SSM_STATE, SSM_GROUP), 0.5 / math.sqrt(SSM_GROUP))
    inp['ssm_b_im'] = nrm(ks[21], (DEPTH, SSM_GROUPS, SSM_STATE, SSM_GROUP), 0.5 / math.sqrt(SSM_GROUP))
    inp['ssm_c_re'] = nrm(ks[22], (DEPTH, SSM_GROUPS, SSM_GROUP, SSM_STATE), 0.5 / math.sqrt(SSM_STATE))
    inp['ssm_c_im'] = nrm(ks[23], (DEPTH, SSM_GROUPS, SSM_GROUP, SSM_STATE), 0.5 / math.sqrt(SSM_STATE))
    inp['ssm_d'] = nrm(ks[24], (DEPTH, SSM_WIDTH))
    inp['w_glu'] = nrm(ks[25], (DEPTH, SSM_WIDTH, SSM_WIDTH), 1.0 / math.sqrt(SSM_WIDTH))
    inp['b_glu'] = nrm(ks[26], (DEPTH, SSM_WIDTH), 0.02)
    inp['w_out'] = nrm(ks[27], (DEPTH, D_MIX, D_MODEL), 1.0 / math.sqrt(D_MIX))
    return inp


def reference(x_prompt, x_sample, c_prompt, c_sample, state_gla, cache_swa_k, cache_swa_v,
              state_ssm_re, state_ssm_im, w_ada, b_ada, g_pre, g_post, w_in, w_gla_lr, b_gla_lr,
              g_gla, ssm_lambda_re, ssm_lambda_im, ssm_log_dt, ssm_b_re, ssm_b_im, ssm_c_re,
              ssm_c_im, ssm_d, w_glu, b_glu, w_out):
    Bp = x_prompt.shape[0]
    xp, xs = x_prompt, x_sample
    gla_p, gla_s, kp, vp, ks_, vs_ = [], [], [], [], [], []
    rep, imp, res, ims = [], [], [], []
    zero_gla = jnp.zeros((Bp, GLA_HEADS, GLA_DK, GLA_DV), F32)
    empty_kv = jnp.zeros((Bp, 0, SWA_HEADS, SWA_HD), x_prompt.dtype)
    zero_ssm = jnp.zeros((Bp, SSM_GROUPS, SSM_STATE), F32)
    for l in range(DEPTH):
        lp = (w_ada[l], b_ada[l], g_pre[l], g_post[l], w_in[l], w_gla_lr[l], b_gla_lr[l], g_gla[l],
              ssm_lambda_re[l], ssm_lambda_im[l], ssm_log_dt[l], ssm_b_re[l], ssm_b_im[l],
              ssm_c_re[l], ssm_c_im[l], ssm_d[l], w_glu[l], b_glu[l], w_out[l])
        xp, g1, k1, v1, r1, i1 = layer(xp, c_prompt, 0, zero_gla, empty_kv, empty_kv,
                                       zero_ssm, zero_ssm, *lp)
        xs, g2, k2, v2, r2, i2 = layer(xs, c_sample, PAST_LEN, state_gla[l], cache_swa_k[l],
                                       cache_swa_v[l], state_ssm_re[l], state_ssm_im[l], *lp)
        gla_p.append(g1); kp.append(k1); vp.append(v1); rep.append(r1); imp.append(i1)
        gla_s.append(g2); ks_.append(k2); vs_.append(v2); res.append(r2); ims.append(i2)
    return (xp, xs, jnp.stack(gla_p), jnp.stack(gla_s), jnp.stack(kp), jnp.stack(vp),
            jnp.stack(ks_), jnp.stack(vs_), jnp.stack(rep), jnp.stack(imp),
            jnp.stack(res), jnp.stack(ims))
```

```python
import functools

import numpy as np
import jax
import jax.numpy as jnp
from jax import lax
from jax.experimental import pallas as pl
from jax.experimental.pallas import tpu as pltpu

F32 = jnp.float32
BF16 = jnp.bfloat16
HIGHEST = lax.Precision.HIGHEST

D_MODEL = 1024
DEPTH = 4
PAST_LEN = 8192
GLA_HEADS = 4
GLA_DK = 32
GLA_DV = 64
GLA_WIDTH = GLA_HEADS * GLA_DV
GLA_RANK = 16
GLA_TAU = 16.0
GLA_CHUNK = 64
SWA_WIDTH = 512
SWA_HEADS = 8
SWA_HD = 64
DILATED_PATTERNS = ((128, 1), (512, 4), (2048, 16))
SWA_WMAX = 2048
SWA_QBLOCK = 128
ROPE_THETA = 500000.0
ROPE_DIMS = 16
SSM_WIDTH = 256
SSM_GROUP = 16
SSM_GROUPS = 16
SSM_STATE = 64
SSM_FLAT = SSM_GROUPS * SSM_STATE
NORM_EPS = 1e-6
NEG_BIG = -1e30

SAMPLE_PAD = 8
LANES = 128
VMEM_LIMIT = 48 * 1024 * 1024

W_COLS = (("a_qk", 256), ("a_v", 256), ("a_gate", 256), ("b_q", 512), ("b_k", 512),
          ("b_v", 512), ("b_gate", 512), ("c_u", 256), ("c_gate", 256), ("a_lr", 128))
W_OFF = {}
_o = 0
for _n, _w in W_COLS:
    W_OFF[_n] = (_o, _o + _w)
    _o += _w
W_TOTAL = _o


def _params(*sem):
    return pltpu.CompilerParams(dimension_semantics=sem, vmem_limit_bytes=VMEM_LIMIT)


def _dot(a, b):
    return jnp.dot(a.astype(BF16), b.astype(BF16), preferred_element_type=F32)


def _dot_nt(a, b):
    return lax.dot_general(a.astype(BF16), b.astype(BF16), (((1,), (1,)), ((), ())),
                           preferred_element_type=F32)


def _dot_tn(a, b):
    return lax.dot_general(a.astype(BF16), b.astype(BF16), (((0,), (0,)), ((), ())),
                           preferred_element_type=F32)


def _dot_f32(a, b):
    return jnp.dot(a, b, precision=HIGHEST, preferred_element_type=F32)


def _silu(x):
    return x * jax.nn.sigmoid(x)


def _mod_kernel(c_ref, w_ref, b_ref, o_ref):
    o_ref[0] = _dot_f32(_silu(c_ref[...]), w_ref[0]) + b_ref[0]


def _ada_mod(c_all, w_ada, b_ada):
    rows = c_all.shape[0]
    nt = 3 * D_MODEL // 1024
    return pl.pallas_call(
        _mod_kernel,
        grid=(DEPTH, nt),
        in_specs=[pl.BlockSpec((rows, D_MODEL), lambda l, n: (0, 0)),
                  pl.BlockSpec((1, D_MODEL, 1024), lambda l, n: (l, 0, n)),
                  pl.BlockSpec((1, 1, 1024), lambda l, n: (l, 0, n))],
        out_specs=pl.BlockSpec((1, rows, 1024), lambda l, n: (l, 0, n)),
        out_shape=jax.ShapeDtypeStruct((DEPTH, rows, 3 * D_MODEL), F32),
        compiler_params=_params("arbitrary", "arbitrary"),
        name="ada_mod",
    )(c_all, w_ada, b_ada.reshape(DEPTH, 1, 3 * D_MODEL))


def _rope(x, cos, sa, sb):
    outs = []
    for c in range(SWA_WIDTH // LANES):
        xc = x[:, LANES * c:LANES * (c + 1)]
        outs.append(xc * cos + pltpu.roll(xc, LANES - ROPE_DIMS // 2, 1) * sa
                    + pltpu.roll(xc, ROPE_DIMS // 2, 1) * sb)
    return jnp.concatenate(outs, axis=1)


def _proj_kernel(x_ref, sc_ref, sh_ref, g_ref, w_ref, cos_ref, sa_ref, sb_ref, *outs, emit_tail):
    x = x_ref[0]
    ms = jnp.mean(x * x, axis=-1, keepdims=True)
    h = x * lax.rsqrt(ms + NORM_EPS) * g_ref[...]
    h = h * (1.0 + sc_ref[0]) + sh_ref[0]
    hb = h.astype(BF16)

    def mm(name):
        c0, c1 = W_OFF[name]
        return jnp.dot(hb, w_ref[:, c0:c1], preferred_element_type=F32)

    names = [n for n, _ in W_COLS]
    cos, sa, sb = cos_ref[...], sa_ref[...], sb_ref[...]
    for idx, name in enumerate(names):
        val = mm(name)
        if name in ("b_q", "b_k"):
            val = _rope(val, cos, sa, sb)
        outs[idx][0] = val
        if emit_tail and name == "b_k":
            outs[len(names)][0] = val
        if emit_tail and name == "b_v":
            outs[len(names) + 1][0] = val


def _proj(x, scale, shift, g_pre, w_cat, rope_tabs, *, tm, emit_tail):
    nb, T, _ = x.shape
    nt = T // tm
    R = scale.shape[1]
    if R == 1:
        mod_spec = pl.BlockSpec((1, 1, D_MODEL), lambda b, j: (b, 0, 0))
    else:
        mod_spec = pl.BlockSpec((1, tm, D_MODEL), lambda b, j: (b, j, 0))
    tab_spec = pl.BlockSpec((tm, LANES), lambda b, j: (j, 0))
    out_specs = [pl.BlockSpec((1, tm, w), lambda b, j: (b, j, 0)) for _, w in W_COLS]
    out_shape = [jax.ShapeDtypeStruct((nb, T, w), F32) for _, w in W_COLS]
    if emit_tail:
        keep = min(SWA_WMAX, T)
        first = (T - keep) // tm
        tail_spec = pl.BlockSpec((1, tm, SWA_WIDTH), lambda b, j: (b, jnp.maximum(j - first, 0), 0))
        out_specs += [tail_spec, tail_spec]
        out_shape += [jax.ShapeDtypeStruct((nb, keep, SWA_WIDTH), F32)] * 2
    res = pl.pallas_call(
        functools.partial(_proj_kernel, emit_tail=emit_tail),
        grid=(nb, nt),
        in_specs=[pl.BlockSpec((1, tm, D_MODEL), lambda b, j: (b, j, 0)),
                  mod_spec, mod_spec,
                  pl.BlockSpec((1, D_MODEL), lambda b, j: (0, 0)),
                  pl.BlockSpec((D_MODEL, W_TOTAL), lambda b, j: (0, 0)),
                  tab_spec, tab_spec, tab_spec],
        out_specs=out_specs,
        out_shape=out_shape,
        compiler_params=_params("arbitrary", "arbitrary"),
        name="in_proj",
    )(x, scale, shift, g_pre.reshape(1, D_MODEL), w_cat, *rope_tabs)
    names = [n for n, _ in W_COLS]
    d = dict(zip(names, res[:len(names)]))
    if emit_tail:
        d["k_tail"], d["v_tail"] = res[len(names)], res[len(names) + 1]
    return d


def _gla_kernel(qk_ref, v_ref, gate_ref, lr_ref, wlr_ref, blr_ref, gg_ref, s0_ref,
                o_ref, sf_ref, s_scr, *, TT, C, t_valid, T):
    j = pl.program_id(1)

    @pl.when(j == 0)
    def _():
        s_scr[...] = s0_ref[0]

    HK = GLA_HEADS * GLA_DK
    z = _dot_f32(lr_ref[0], wlr_ref[...]) + blr_ref[...]
    glog = (jnp.minimum(z, 0.0) - jnp.log1p(jnp.exp(-jnp.abs(z)))) * (1.0 / GLA_TAU)
    qk = qk_ref[0]
    q = qk[:, :HK]
    k = qk[:, HK:]
    if t_valid < T:
        row = j * TT + lax.broadcasted_iota(jnp.int32, (TT, HK), 0)
        glog = jnp.where(row < t_valid, glog, 0.0)
        k = jnp.where(row < t_valid, k, 0.0)
    v = v_ref[0]
    gate = gate_ref[0]
    gg = gg_ref[...]
    ri = lax.broadcasted_iota(jnp.int32, (C, C), 0)
    ci = lax.broadcasted_iota(jnp.int32, (C, C), 1)
    causal = ri >= ci
    tri = causal.astype(F32)
    for c in range(TT // C):
        rows = slice(c * C, (c + 1) * C)
        bc = _dot_f32(tri, glog[rows])
        bl = bc[C - 1:C]
        qd = q[rows] * (GLA_DK ** -0.5) * jnp.exp(bc)
        kd = k[rows] * jnp.exp(-bc)
        kdl = k[rows] * jnp.exp(bl - bc)
        dec_t = jnp.broadcast_to(jnp.exp(bl), (8, HK)).T
        vc = v[rows]
        heads = []
        for h in range(GLA_HEADS):
            ks = slice(h * GLA_DK, (h + 1) * GLA_DK)
            vs = slice(h * GLA_DV, (h + 1) * GLA_DV)
            att = jnp.where(causal, _dot_nt(qd[:, ks], kd[:, ks]), 0.0)
            s_prev = s_scr[h]
            o_h = _dot(qd[:, ks], s_prev) + _dot(att, vc[:, vs])
            s_scr[h] = dec_t[ks, 0:1] * s_prev + _dot_tn(kdl[:, ks], vc[:, vs])
            ms = jnp.mean(o_h * o_h, axis=-1, keepdims=True)
            heads.append(o_h * lax.rsqrt(ms + NORM_EPS) * gg[:, vs])
        o_ref[0, rows, :] = jnp.concatenate(heads, axis=1) * _silu(gate[rows])
    sf_ref[0] = s_scr[...]


def _gla(p, w_lr, b_lr, g_gla, s0, *, TT, C, t_valid):
    B, T, _ = p["a_qk"].shape
    HK = GLA_HEADS * GLA_DK
    row = lambda w: pl.BlockSpec((1, TT, w), lambda b, j: (b, j, 0))
    full = lambda r, c: pl.BlockSpec((r, c), lambda b, j: (0, 0))
    st_spec = pl.BlockSpec((1, GLA_HEADS, GLA_DK, GLA_DV), lambda b, j: (b, 0, 0, 0))
    return pl.pallas_call(
        functools.partial(_gla_kernel, TT=TT, C=C, t_valid=t_valid, T=T),
        grid=(B, T // TT),
        in_specs=[row(2 * HK), row(GLA_WIDTH), row(GLA_WIDTH), row(LANES),
                  full(LANES, HK), full(1, HK), full(1, GLA_WIDTH), st_spec],
        out_specs=[row(GLA_WIDTH), st_spec],
        out_shape=[jax.ShapeDtypeStruct((B, T, GLA_WIDTH), F32),
                   jax.ShapeDtypeStruct((B, GLA_HEADS, GLA_DK, GLA_DV), F32)],
        scratch_shapes=[pltpu.VMEM((GLA_HEADS, GLA_DK, GLA_DV), F32)],
        compiler_params=_params("arbitrary", "arbitrary"),
        name="gla_mix",
    )(p["a_qk"], p["a_v"], p["a_gate"], p["a_lr"], w_lr, b_lr.reshape(1, HK),
      g_gla.reshape(1, GLA_WIDTH), s0)


def _swa_kernel(q_ref, k_ref, v_ref, g_ref, o_ref, op_scr, lse_scr, *, T):
    QB = SWA_QBLOCK
    ri = lax.broadcasted_iota(jnp.int32, (QB, 2 * QB), 0)
    ci = lax.broadcasted_iota(jnp.int32, (QB, 2 * QB), 1)
    band = (ci >= ri) & (ci <= ri + QB)
    ri1 = lax.broadcasted_iota(jnp.int32, (QB, QB), 0)
    ci1 = lax.broadcasted_iota(jnp.int32, (QB, QB), 1)
    first = ci1 <= ri1

    def rows(start, n, d):
        return pl.ds(start, n) if d == 1 else pl.ds(start, n, stride=d)

    def block(p, d, qstart, kstart, nk, mask):
        q = q_ref[0, rows(qstart, QB, d), :]
        k = k_ref[0, rows(kstart, nk, d), :]
        v = v_ref[0, rows(kstart, nk, d), :]
        outs, lses = [], []
        for h in range(LANES // SWA_HD):
            hs = slice(h * SWA_HD, (h + 1) * SWA_HD)
            s = _dot_nt(q[:, hs] * (SWA_HD ** -0.5), k[:, hs])
            s = jnp.where(mask, s, NEG_BIG)
            m = jnp.max(s, axis=-1, keepdims=True)
            e = jnp.exp(s - m)
            den = jnp.sum(e, axis=-1, keepdims=True)
            outs.append(_dot(e, v[:, hs]) / den)
            lses.append(jnp.broadcast_to(m + jnp.log(den), (QB, SWA_HD)))
        op_scr[p, rows(qstart, QB, d), :] = jnp.concatenate(outs, axis=1)
        lse_scr[p, rows(qstart, QB, d), :] = jnp.concatenate(lses, axis=1)

    for p, (_, d) in enumerate(DILATED_PATTERNS):
        nblk = T // d // QB

        def per_residue(r, carry, p=p, d=d, nblk=nblk):
            block(p, d, r, r, QB, first)

            def per_block(n, c2):
                block(p, d, r + d * QB * n, r + d * QB * (n - 1), 2 * QB, band)
                return c2

            lax.fori_loop(1, nblk, per_block, 0)
            return carry

        lax.fori_loop(0, d, per_residue, 0)

    CH = 512

    def combine(i, carry):
        rs = pl.ds(pl.multiple_of(i * CH, CH), CH)
        l0, l1, l2 = lse_scr[0, rs, :], lse_scr[1, rs, :], lse_scr[2, rs, :]
        mx = jnp.maximum(jnp.maximum(l0, l1), l2)
        w0, w1, w2 = jnp.exp(l0 - mx), jnp.exp(l1 - mx), jnp.exp(l2 - mx)
        mixed = (w0 * op_scr[0, rs, :] + w1 * op_scr[1, rs, :] + w2 * op_scr[2, rs, :]) / (w0 + w1 + w2)
        o_ref[0, rs, :] = mixed * _silu(g_ref[0, rs, :])
        return carry

    lax.fori_loop(0, T // CH, combine, 0)


def _swa_prompt(p):
    B, T, _ = p["b_q"].shape
    spec = pl.BlockSpec((1, T, LANES), lambda b, hp: (b, 0, hp))
    return pl.pallas_call(
        functools.partial(_swa_kernel, T=T),
        grid=(B, SWA_WIDTH // LANES),
        in_specs=[spec] * 4,
        out_specs=spec,
        out_shape=jax.ShapeDtypeStruct((B, T, SWA_WIDTH), F32),
        scratch_shapes=[pltpu.VMEM((len(DILATED_PATTERNS), T, LANES), F32),
                        pltpu.VMEM((len(DILATED_PATTERNS), T, LANES), F32)],
        compiler_params=_params("arbitrary", "arbitrary"),
        name="swa_prompt",
    )(p["b_q"], p["b_k"], p["b_v"], p["b_gate"])


def _swa_dec_kernel(ck_ref, cv_ref, q_ref, k_ref, v_ref, g_ref, mult_ref, multn_ref, hm_ref,
                    o_ref, ko_ref, vo_ref, *, L, Tn):
    NQ = SAMPLE_PAD
    ko_ref[0, pl.ds(0, L - Tn), :] = ck_ref[0, 0, pl.ds(Tn, L - Tn), :]
    ko_ref[0, pl.ds(L - Tn, Tn), :] = k_ref[0, pl.ds(0, Tn), :]
    vo_ref[0, pl.ds(0, L - Tn), :] = cv_ref[0, 0, pl.ds(Tn, L - Tn), :]
    vo_ref[0, pl.ds(L - Tn, Tn), :] = v_ref[0, pl.ds(0, Tn), :]

    hm = hm_ref[...]
    q = q_ref[0] * (SWA_HD ** -0.5)
    qe = [jnp.broadcast_to(q[i:i + 1], (SWA_HEADS, SWA_WIDTH)) * hm for i in range(NQ)]
    qe.append(jnp.zeros((LANES - NQ * SWA_HEADS, SWA_WIDTH), F32))
    qe = jnp.concatenate(qe, axis=0).astype(BF16)
    kn = k_ref[0]
    vn = v_ref[0]
    s = _dot_nt(ck_ref[0, 0], qe)
    sn = _dot_nt(kn, qe)
    mult = mult_ref[...]
    multn = multn_ref[...]
    s = jnp.where(mult > 0, s, NEG_BIG)
    sn = jnp.where(multn > 0, sn, NEG_BIG)
    m = jnp.maximum(jnp.max(s, axis=0, keepdims=True), jnp.max(sn, axis=0, keepdims=True))
    pe = jnp.exp(s - m) * mult
    pn = jnp.exp(sn - m) * multn
    den = jnp.sum(pe, axis=0, keepdims=True) + jnp.sum(pn, axis=0, keepdims=True)
    den = jnp.where(den > 0, den, 1.0)
    o = _dot_tn(pe, cv_ref[0, 0]) + _dot_tn(pn, vn)
    den_col = jnp.broadcast_to(den, (8, LANES)).T[:, 0:1]
    o = (o / den_col)[:NQ * SWA_HEADS].reshape(NQ, SWA_HEADS, SWA_WIDTH)
    o = jnp.sum(o * hm[None], axis=1)
    o_ref[0] = o * _silu(g_ref[0])


def _mult_tables(L, Tn):
    NQ = SAMPLE_PAD
    mult = np.zeros((L, LANES), np.float32)
    multn = np.zeros((NQ, LANES), np.float32)
    for i in range(Tn):
        off = (SWA_WMAX + i) - (np.arange(L) + SWA_WMAX - L)
        cnt = np.zeros(L, np.float32)
        for w, d in DILATED_PATTERNS:
            cnt += ((off <= w) & (off % d == 0)).astype(np.float32)
        mult[:, i * SWA_HEADS:(i + 1) * SWA_HEADS] = cnt[:, None]
        for jn in range(i + 1):
            offn = i - jn
            multn[jn, i * SWA_HEADS:(i + 1) * SWA_HEADS] = sum(
                1.0 for w, d in DILATED_PATTERNS if offn <= w and offn % d == 0)
    hm = (np.arange(SWA_WIDTH)[None, :] // SWA_HD == np.arange(SWA_HEADS)[:, None]).astype(np.float32)
    return jnp.asarray(mult), jnp.asarray(multn), jnp.asarray(hm)


def _swa_sample(p, cache_k, cache_v, l, Tn):
    B = p["b_q"].shape[0]
    L = cache_k.shape[2]
    mult, multn, hm = _mult_tables(L, Tn)
    cspec = pl.BlockSpec((1, 1, L, SWA_WIDTH), lambda b: (l, b, 0, 0))
    nspec = pl.BlockSpec((1, SAMPLE_PAD, SWA_WIDTH), lambda b: (b, 0, 0))
    ospec = pl.BlockSpec((1, L, SWA_WIDTH), lambda b: (b, 0, 0))
    const = lambda a: pl.BlockSpec(a.shape, lambda b: (0, 0))
    return pl.pallas_call(
        functools.partial(_swa_dec_kernel, L=L, Tn=Tn),
        grid=(B,),
        in_specs=[cspec, cspec, nspec, nspec, nspec, nspec, const(mult), const(multn), const(hm)],
        out_specs=[nspec, ospec, ospec],
        out_shape=[jax.ShapeDtypeStruct((B, SAMPLE_PAD, SWA_WIDTH), F32),
                   jax.ShapeDtypeStruct((B, L, SWA_WIDTH), F32),
                   jax.ShapeDtypeStruct((B, L, SWA_WIDTH), F32)],
        compiler_params=_params("arbitrary"),
        name="swa_sample",
    )(cache_k, cache_v, p["b_q"], p["b_k"], p["b_v"], p["b_gate"], mult, multn, hm)


def _s5prep_kernel(lr_ref, li_ref, ldt_ref, bre_ref, bim_ref, bcat_ref, tab_ref):
    lr = lr_ref[0]
    li = li_ref[0]
    dt = jnp.exp(ldt_ref[0])
    mag = jnp.exp(lr * dt)
    br = mag * jnp.cos(li * dt)
    bi = mag * jnp.sin(li * dt)
    den = lr * lr + li * li
    nr = br - 1.0
    cr = (nr * lr + bi * li) / den
    ci = (bi * lr - nr * li) / den
    bre = bre_ref[0]
    bim = bim_ref[0]
    bcat_ref[0, :, 0:SSM_FLAT] = (cr * bre - ci * bim).astype(BF16)
    bcat_ref[0, :, SSM_FLAT:2 * SSM_FLAT] = (cr * bim + ci * bre).astype(BF16)
    pows = [(br, bi)]
    for _ in range(7):
        pr, pi = pows[-1]
        pows.append((pr * br - pi * bi, pr * bi + pi * br))
    row = lax.broadcasted_iota(jnp.int32, (8, SSM_FLAT), 0)
    bc = lambda a: jnp.broadcast_to(a, (8, SSM_FLAT))
    for si, s in enumerate((1, 2, 4)):
        tab_ref[0, 2 * si] = jnp.where(row >= s, bc(pows[s - 1][0]), 0.0)
        tab_ref[0, 2 * si + 1] = jnp.where(row >= s, bc(pows[s - 1][1]), 0.0)
    lpr, lpi = bc(pows[0][0]), bc(pows[0][1])
    for kk in range(1, 8):
        lpr = jnp.where(row == kk, bc(pows[kk][0]), lpr)
        lpi = jnp.where(row == kk, bc(pows[kk][1]), lpi)
    tab_ref[0, 6] = lpr
    tab_ref[0, 7] = lpi


def _s5_prep(lam_re, lam_im, log_dt, b_re, b_im):
    eye = jnp.eye(SSM_GROUPS, dtype=F32)

    def block_diag_b(b):
        bt = jnp.transpose(b, (0, 1, 3, 2))
        return (bt[:, :, :, None, :] * eye[None, :, None, :, None]).reshape(DEPTH, SSM_WIDTH, SSM_FLAT)

    flat = lambda a: a.reshape(DEPTH, 1, SSM_FLAT)
    ldt = jnp.broadcast_to(log_dt[:, :, None], (DEPTH, SSM_GROUPS, SSM_STATE))
    vec = pl.BlockSpec((1, 1, SSM_FLAT), lambda l: (l, 0, 0))
    mat = pl.BlockSpec((1, SSM_WIDTH, SSM_FLAT), lambda l: (l, 0, 0))
    return pl.pallas_call(
        _s5prep_kernel,
        grid=(DEPTH,),
        in_specs=[vec, vec, vec, mat, mat],
        out_specs=[pl.BlockSpec((1, SSM_WIDTH, 2 * SSM_FLAT), lambda l: (l, 0, 0)),
                   pl.BlockSpec((1, 8, 8, SSM_FLAT), lambda l: (l, 0, 0, 0))],
        out_shape=[jax.ShapeDtypeStruct((DEPTH, SSM_WIDTH, 2 * SSM_FLAT), BF16),
                   jax.ShapeDtypeStruct((DEPTH, 8, 8, SSM_FLAT), F32)],
        compiler_params=_params("arbitrary"),
        name="s5_prep",
    )(flat(lam_re), flat(lam_im), flat(ldt), block_diag_b(b_re), block_diag_b(b_im))


def _s5_kernel(u_ref, gate_ref, x0r_ref, x0i_ref, bcat_ref, cre_ref, cim_ref, tab_ref, d_ref,
               wglu_ref, bglu_ref, o_ref, sr_ref, si_ref, xr_scr, xi_scr, cr_scr, ci_scr,
               *, TT, last_tile, last_group, last_row):
    j = pl.program_id(1)
    NG = TT // 8

    @pl.when(j == 0)
    def _():
        cr_scr[...] = jnp.broadcast_to(x0r_ref[0], (8, SSM_FLAT))
        ci_scr[...] = jnp.broadcast_to(x0i_ref[0], (8, SSM_FLAT))

    u = u_ref[0]
    bu = _dot(u, bcat_ref[...])
    xr = bu[:, :SSM_FLAT].reshape(NG, 8, SSM_FLAT)
    xi = bu[:, SSM_FLAT:].reshape(NG, 8, SSM_FLAT)
    for si in range(3):
        s = 1 << si
        ar = tab_ref[2 * si]
        ai = tab_ref[2 * si + 1]
        shr = pltpu.roll(xr, s, 1)
        shi = pltpu.roll(xi, s, 1)
        xr, xi = xr + ar * shr - ai * shi, xi + ar * shi + ai * shr
    xr_scr[...] = xr
    xi_scr[...] = xi
    lpr = tab_ref[6]
    lpi = tab_ref[7]

    def group(g, carry):
        cr, ci = carry
        nr = xr_scr[g] + lpr * cr - lpi * ci
        ni = xi_scr[g] + lpr * ci + lpi * cr
        xr_scr[g] = nr
        xi_scr[g] = ni
        return (jnp.broadcast_to(nr[7:8], (8, SSM_FLAT)), jnp.broadcast_to(ni[7:8], (8, SSM_FLAT)))

    cr, ci = lax.fori_loop(0, NG, group, (cr_scr[...], ci_scr[...]))
    cr_scr[...] = cr
    ci_scr[...] = ci

    @pl.when(j == last_tile)
    def _():
        sr_ref[0] = xr_scr[last_group][last_row:last_row + 1]
        si_ref[0] = xi_scr[last_group][last_row:last_row + 1]

    xr = xr_scr[...].reshape(TT, SSM_FLAT)
    xi = xi_scr[...].reshape(TT, SSM_FLAT)
    y = _dot(xr, cre_ref[...]) - _dot(xi, cim_ref[...]) + d_ref[...] * u
    z = jax.nn.gelu(y)
    out = z * jax.nn.sigmoid(_dot(z, wglu_ref[...]) + bglu_ref[...])
    o_ref[0] = out * _silu(gate_ref[0])


def _s5(p, x0r, x0i, bcat, cre, cim, tab, d_skip, w_glu, b_glu, *, TT, t_valid):
    B, T, _ = p["c_u"].shape
    last = t_valid - 1
    row = pl.BlockSpec((1, TT, SSM_WIDTH), lambda b, j: (b, j, 0))
    st = pl.BlockSpec((1, 1, SSM_FLAT), lambda b, j: (b, 0, 0))
    full = lambda a: pl.BlockSpec(a.shape, lambda b, j: (0,) * a.ndim)
    d2 = d_skip.reshape(1, SSM_WIDTH)
    bg2 = b_glu.reshape(1, SSM_WIDTH)
    return pl.pallas_call(
        functools.partial(_s5_kernel, TT=TT, last_tile=last // TT, last_group=(last % TT) // 8,
                          last_row=last % 8),
        grid=(B, T // TT),
        in_specs=[row, row, st, st, full(bcat), full(cre), full(cim), full(tab), full(d2),
                  full(w_glu), full(bg2)],
        out_specs=[row, st, st],
        out_shape=[jax.ShapeDtypeStruct((B, T, SSM_WIDTH), F32),
                   jax.ShapeDtypeStruct((B, 1, SSM_FLAT), F32),
                   jax.ShapeDtypeStruct((B, 1, SSM_FLAT), F32)],
        scratch_shapes=[pltpu.VMEM((TT // 8, 8, SSM_FLAT), F32), pltpu.VMEM((TT // 8, 8, SSM_FLAT), F32),
                        pltpu.VMEM((8, SSM_FLAT), F32), pltpu.VMEM((8, SSM_FLAT), F32)],
        compiler_params=_params("arbitrary", "arbitrary"),
        name="s5_mix",
    )(p["c_u"], p["c_gate"], x0r, x0i, bcat, cre, cim, tab, d2, w_glu, bg2)


def _out_kernel(oa_ref, ob_ref, oc_ref, x_ref, gate_ref, g_ref, w_ref, o_ref):
    a1 = GLA_WIDTH
    a2 = GLA_WIDTH + SWA_WIDTH
    y = (jnp.dot(oa_ref[0].astype(BF16), w_ref[0:a1, :], preferred_element_type=F32)
         + jnp.dot(ob_ref[0].astype(BF16), w_ref[a1:a2, :], preferred_element_type=F32)
         + jnp.dot(oc_ref[0].astype(BF16), w_ref[a2:, :], preferred_element_type=F32))
    ms = jnp.mean(y * y, axis=-1, keepdims=True)
    yn = y * lax.rsqrt(ms + NORM_EPS) * g_ref[...]
    o_ref[0] = x_ref[0] + gate_ref[0] * yn


def _out_proj(oa, ob, oc, x, gate, g_post, w_out, *, tm):
    nb, T, _ = x.shape
    R = gate.shape[1]
    row = lambda w: pl.BlockSpec((1, tm, w), lambda b, j: (b, j, 0))
    if R == 1:
        gate_spec = pl.BlockSpec((1, 1, D_MODEL), lambda b, j: (b, 0, 0))
    else:
        gate_spec = row(D_MODEL)
    return pl.pallas_call(
        _out_kernel,
        grid=(nb, T // tm),
        in_specs=[row(GLA_WIDTH), row(SWA_WIDTH), row(SSM_WIDTH), row(D_MODEL), gate_spec,
                  pl.BlockSpec((1, D_MODEL), lambda b, j: (0, 0)),
                  pl.BlockSpec((D_MODEL, D_MODEL), lambda b, j: (0, 0))],
        out_specs=row(D_MODEL),
        out_shape=jax.ShapeDtypeStruct((nb, T, D_MODEL), F32),
        compiler_params=_params("arbitrary", "arbitrary"),
        name="out_proj",
    )(oa, ob, oc, x, gate, g_post.reshape(1, D_MODEL), w_out)


def _rope_tables(pos):
    half = ROPE_DIMS // 2
    inv = ROPE_THETA ** (-jnp.arange(half, dtype=F32) * (2.0 / ROPE_DIMS))
    ang = pos.astype(F32)[:, None] * inv[None, :]
    cos, sin = jnp.cos(ang), jnp.sin(ang)
    n = pos.shape[0]
    rest = SWA_HD - ROPE_DIMS
    head = lambda a, b, fill: jnp.concatenate([a, b, jnp.full((n, rest), fill, F32)], axis=1)
    zero = jnp.zeros((n, half), F32)
    reps = LANES // SWA_HD
    return (jnp.tile(head(cos, cos, 1.0), (1, reps)),
            jnp.tile(head(-sin, zero, 0.0), (1, reps)),
            jnp.tile(head(zero, sin, 0.0), (1, reps)))


def _rearrange_w_in(w_in):
    GK = GLA_HEADS * GLA_DK
    o = np.cumsum([0, GK, GK, GLA_WIDTH, GLA_RANK, GLA_WIDTH, SWA_WIDTH, SWA_WIDTH, SWA_WIDTH,
                   SWA_WIDTH, SSM_WIDTH, SSM_WIDTH])
    piece = lambda i: w_in[:, :, o[i]:o[i + 1]]
    lr = jnp.pad(piece(3), ((0, 0), (0, 0), (0, LANES - GLA_RANK)))
    cat = jnp.concatenate([piece(0), piece(1), piece(2), piece(4), piece(5), piece(6), piece(7),
                           piece(8), piece(9), piece(10), lr], axis=-1)
    return cat.astype(BF16)


def kernel(x_prompt, x_sample, c_prompt, c_sample, state_gla, cache_swa_k, cache_swa_v, state_ssm_re, state_ssm_im, w_ada, b_ada, g_pre, g_post, w_in, w_gla_lr, b_gla_lr, g_gla, ssm_lambda_re, ssm_lambda_im, ssm_log_dt, ssm_b_re, ssm_b_im, ssm_c_re, ssm_c_im, ssm_d, w_glu, b_glu, w_out):
    Bp, Tp, _ = x_prompt.shape
    Bs, Ts, _ = x_sample.shape
    PAD = SAMPLE_PAD
    Lc = cache_swa_k.shape[2]

    w_cat = _rearrange_w_in(w_in)
    w_out_b = w_out.astype(BF16)
    w_glu_b = w_glu.astype(BF16)
    w_lr_pad = jnp.pad(w_gla_lr, ((0, 0), (0, LANES - GLA_RANK), (0, 0)))
    eye = jnp.eye(SSM_GROUPS, dtype=F32)

    def block_diag_c(c):
        ct = jnp.transpose(c, (0, 1, 3, 2))
        return (ct[:, :, :, None, :] * eye[None, :, None, :, None]).reshape(DEPTH, SSM_FLAT, SSM_WIDTH)

    cre = block_diag_c(ssm_c_re).astype(BF16)
    cim = block_diag_c(ssm_c_im).astype(BF16)
    bcat, tab = _s5_prep(ssm_lambda_re, ssm_lambda_im, ssm_log_dt, ssm_b_re, ssm_b_im)

    n_c = Bp + Bs
    c_rows = -(-n_c // 8) * 8
    c_all = jnp.pad(jnp.concatenate([c_prompt, c_sample], axis=0), ((0, c_rows - n_c), (0, 0)))
    mod = _ada_mod(c_all, w_ada, b_ada)

    tabs_p = _rope_tables(jnp.arange(Tp))
    tabs_s = _rope_tables(PAST_LEN + jnp.arange(Bs * PAD) % PAD)

    xp = x_prompt
    xs = jnp.pad(x_sample, ((0, 0), (0, PAD - Ts), (0, 0))).reshape(1, Bs * PAD, D_MODEL)
    cache_k = cache_swa_k.reshape(DEPTH, Bs, Lc, SWA_WIDTH)
    cache_v = cache_swa_v.reshape(DEPTH, Bs, Lc, SWA_WIDTH)
    zero_gla = jnp.zeros((Bp, GLA_HEADS, GLA_DK, GLA_DV), F32)
    zero_ssm = jnp.zeros((Bp, 1, SSM_FLAT), F32)

    acc = [[] for _ in range(10)]
    for l in range(DEPTH):
        def chunk(rows, k):
            return mod[l, rows, k * D_MODEL:(k + 1) * D_MODEL]

        pr = slice(0, Bp)
        shift, scale, gate = (chunk(pr, k).reshape(Bp, 1, D_MODEL) for k in range(3))
        p = _proj(xp, scale, shift, g_pre[l], w_cat[l], tabs_p, tm=256, emit_tail=True)
        oa, gla_p = _gla(p, w_lr_pad[l], b_gla_lr[l], g_gla[l], zero_gla, TT=256, C=GLA_CHUNK, t_valid=Tp)
        ob = _swa_prompt(p)
        oc, re_p, im_p = _s5(p, zero_ssm, zero_ssm, bcat[l], cre[l], cim[l], tab[l], ssm_d[l],
                             w_glu_b[l], b_glu[l], TT=256, t_valid=Tp)
        xp = _out_proj(oa, ob, oc, xp, gate, g_post[l], w_out_b[l], tm=512)

        sr = slice(Bp, Bp + Bs)
        shift, scale, gate = (jnp.repeat(chunk(sr, k), PAD, axis=0).reshape(1, Bs * PAD, D_MODEL)
                              for k in range(3))
        q = _proj(xs, scale, shift, g_pre[l], w_cat[l], tabs_s, tm=Bs * PAD, emit_tail=False)
        q = {n: a.reshape(Bs, PAD, a.shape[-1]) for n, a in q.items()}
        oa, gla_s = _gla(q, w_lr_pad[l], b_gla_lr[l], g_gla[l], state_gla[l], TT=PAD, C=PAD, t_valid=Ts)
        ob, k_s, v_s = _swa_sample(q, cache_k, cache_v, l, Ts)
        oc, re_s, im_s = _s5(q, state_ssm_re[l].reshape(Bs, 1, SSM_FLAT),
                             state_ssm_im[l].reshape(Bs, 1, SSM_FLAT), bcat[l], cre[l], cim[l], tab[l],
                             ssm_d[l], w_glu_b[l], b_glu[l], TT=PAD, t_valid=Ts)
        flat = lambda a: a.reshape(1, Bs * PAD, a.shape[-1])
        xs = _out_proj(flat(oa), flat(ob), flat(oc), xs, gate, g_post[l], w_out_b[l], tm=Bs * PAD)

        for i, a in enumerate((gla_p, gla_s, p["k_tail"], p["v_tail"], k_s, v_s, re_p, im_p, re_s, im_s)):
            acc[i].append(a)

    st = [jnp.stack(a) for a in acc]
    kv = lambda a: a.reshape(DEPTH, a.shape[1], a.shape[2], SWA_HEADS, SWA_HD)
    ssm = lambda a: a.reshape(DEPTH, a.shape[1], SSM_GROUPS, SSM_STATE)
    y_sample = xs.reshape(Bs, PAD, D_MODEL)[:, :Ts]
    return (xp, y_sample, st[0], st[1], kv(st[2]), kv(st[3]), kv(st[4]), kv(st[5]),
            ssm(st[6]), ssm(st[7]), ssm(st[8]), ssm(st[9]))
```

```python
import functools

import numpy as np
import jax
import jax.numpy as jnp
from jax import lax
from jax.experimental import pallas as pl
from jax.experimental.pallas import tpu as pltpu

F32 = jnp.float32
BF16 = jnp.bfloat16
HIGHEST = lax.Precision.HIGHEST

D_MODEL = 1024
DEPTH = 4
PAST_LEN = 8192
GLA_HEADS = 4
GLA_DK = 32
GLA_DV = 64
GLA_WIDTH = GLA_HEADS * GLA_DV
GLA_RANK = 16
GLA_TAU = 16.0
GLA_CHUNK = 64
SWA_WIDTH = 512
SWA_HEADS = 8
SWA_HD = 64
DILATED_PATTERNS = ((128, 1), (512, 4), (2048, 16))
SWA_WMAX = 2048
SWA_QBLOCK = 128
ROPE_THETA = 500000.0
ROPE_DIMS = 16
SSM_WIDTH = 256
SSM_GROUP = 16
SSM_GROUPS = 16
SSM_STATE = 64
SSM_FLAT = SSM_GROUPS * SSM_STATE
NORM_EPS = 1e-6
NEG_BIG = -1e30

SAMPLE_PAD = 8
LANES = 128
VMEM_LIMIT = 48 * 1024 * 1024

W_COLS = (("a_qk", 256), ("a_v", 256), ("a_gate", 256), ("b_q", 512), ("b_k", 512),
          ("b_v", 512), ("b_gate", 512), ("c_u", 256), ("c_gate", 256), ("a_lr", 128))
W_OFF = {}
_o = 0
for _n, _w in W_COLS:
    W_OFF[_n] = (_o, _o + _w)
    _o += _w
W_TOTAL = _o


def _params(*sem):
    return pltpu.CompilerParams(dimension_semantics=sem, vmem_limit_bytes=VMEM_LIMIT)


def _dot(a, b):
    return jnp.dot(a.astype(BF16), b.astype(BF16), preferred_element_type=F32)


def _dot_nt(a, b):
    return lax.dot_general(a.astype(BF16), b.astype(BF16), (((1,), (1,)), ((), ())),
                           preferred_element_type=F32)


def _dot_tn(a, b):
    return lax.dot_general(a.astype(BF16), b.astype(BF16), (((0,), (0,)), ((), ())),
                           preferred_element_type=F32)


def _dot_f32(a, b):
    return jnp.dot(a, b, precision=HIGHEST, preferred_element_type=F32)


def _silu(x):
    return x * jax.nn.sigmoid(x)


def _mod_kernel(c_ref, w_ref, b_ref, o_ref):
    o_ref[0] = _dot_f32(_silu(c_ref[...]), w_ref[0]) + b_ref[0]


def _ada_mod(c_all, w_ada, b_ada):
    rows = c_all.shape[0]
    nt = 3 * D_MODEL // 1024
    return pl.pallas_call(
        _mod_kernel,
        grid=(DEPTH, nt),
        in_specs=[pl.BlockSpec((rows, D_MODEL), lambda l, n: (0, 0)),
                  pl.BlockSpec((1, D_MODEL, 1024), lambda l, n: (l, 0, n)),
                  pl.BlockSpec((1, 1, 1024), lambda l, n: (l, 0, n))],
        out_specs=pl.BlockSpec((1, rows, 1024), lambda l, n: (l, 0, n)),
        out_shape=jax.ShapeDtypeStruct((DEPTH, rows, 3 * D_MODEL), F32),
        compiler_params=_params("arbitrary", "arbitrary"),
        name="ada_mod",
    )(c_all, w_ada, b_ada.reshape(DEPTH, 1, 3 * D_MODEL))


def _rope(x, cos, sa, sb):
    outs = []
    for c in range(SWA_WIDTH // LANES):
        xc = x[:, LANES * c:LANES * (c + 1)]
        outs.append(xc * cos + pltpu.roll(xc, LANES - ROPE_DIMS // 2, 1) * sa
                    + pltpu.roll(xc, ROPE_DIMS // 2, 1) * sb)
    return jnp.concatenate(outs, axis=1)


def _proj_kernel(x_ref, sc_ref, sh_ref, g_ref, w_ref, cos_ref, sa_ref, sb_ref, *rest, emit_tail):
    n_out = len(W_COLS) + (2 if emit_tail else 0)
    outs = rest[len(rest) - n_out:]
    x = x_ref[0]
    ms = jnp.mean(x * x, axis=-1, keepdims=True)
    h = x * lax.rsqrt(ms + NORM_EPS) * g_ref[...]
    h = h * (1.0 + sc_ref[0]) + sh_ref[0]
    hb = h.astype(BF16)

    def mm(name):
        c0, c1 = W_OFF[name]
        return jnp.dot(hb, w_ref[:, c0:c1], preferred_element_type=F32)

    names = [n for n, _ in W_COLS]
    cos, sa, sb = cos_ref[...], sa_ref[...], sb_ref[...]
    for idx, name in enumerate(names):
        val = mm(name)
        if name in ("b_q", "b_k"):
            val = _rope(val, cos, sa, sb)
        outs[idx][0] = val
        if emit_tail and name == "b_k":
            outs[len(names)][0, 0] = val.T.reshape(SWA_HEADS, SWA_HD, val.shape[0])
        if emit_tail and name == "b_v":
            outs[len(names) + 1][0, 0] = val.T.reshape(SWA_HEADS, SWA_HD, val.shape[0])


def _proj(x, scale, shift, g_pre, w_cat, rope_tabs, *, tm, tail=None):
    nb, T, _ = x.shape
    nt = T // tm
    R = scale.shape[1]
    emit_tail = tail is not None
    if R == 1:
        mod_spec = pl.BlockSpec((1, 1, D_MODEL), lambda b, j: (b, 0, 0))
    else:
        mod_spec = pl.BlockSpec((1, tm, D_MODEL), lambda b, j: (b, j, 0))
    tab_spec = pl.BlockSpec((tm, LANES), lambda b, j: (j, 0))
    out_specs = [pl.BlockSpec((1, tm, w), lambda b, j: (b, j, 0)) for _, w in W_COLS]
    out_shape = [jax.ShapeDtypeStruct((nb, T, w), F32) for _, w in W_COLS]
    in_specs = [pl.BlockSpec((1, tm, D_MODEL), lambda b, j: (b, j, 0)),
                mod_spec, mod_spec,
                pl.BlockSpec((1, D_MODEL), lambda b, j: (0, 0)),
                pl.BlockSpec((D_MODEL, W_TOTAL), lambda b, j: (0, 0)),
                tab_spec, tab_spec, tab_spec]
    args = [x, scale, shift, g_pre.reshape(1, D_MODEL), w_cat, *rope_tabs]
    aliases = {}
    if emit_tail:
        l, bufs = tail
        keep = min(SWA_WMAX, T)
        first = (T - keep) // tm
        tail_spec = pl.BlockSpec((1, 1, SWA_HEADS, SWA_HD, tm),
                                 lambda b, j: (l, b, 0, 0, jnp.maximum(j - first, 0)))
        out_specs += [tail_spec, tail_spec]
        out_shape += [jax.ShapeDtypeStruct((DEPTH, nb, SWA_HEADS, SWA_HD, keep), F32)] * 2
        if bufs is not None:
            in_specs += [pl.BlockSpec(memory_space=pl.ANY)] * 2
            aliases = {len(args): len(W_COLS), len(args) + 1: len(W_COLS) + 1}
            args += list(bufs)
    res = pl.pallas_call(
        functools.partial(_proj_kernel, emit_tail=emit_tail),
        grid=(nb, nt),
        in_specs=in_specs,
        out_specs=out_specs,
        out_shape=out_shape,
        input_output_aliases=aliases,
        compiler_params=_params("arbitrary", "arbitrary"),
        name="in_proj",
    )(*args)
    names = [n for n, _ in W_COLS]
    d = dict(zip(names, res[:len(names)]))
    if emit_tail:
        d["k_tail"], d["v_tail"] = res[len(names)], res[len(names) + 1]
    return d


def _gla_kernel(qk_ref, v_ref, gate_ref, lr_ref, wlr_ref, blr_ref, gg_ref, s0_ref,
                o_ref, sf_ref, s_scr, *, TT, C, t_valid, T):
    j = pl.program_id(1)

    @pl.when(j == 0)
    def _():
        s_scr[...] = s0_ref[0]

    HK = GLA_HEADS * GLA_DK
    z = _dot_f32(lr_ref[0], wlr_ref[...]) + blr_ref[...]
    glog = (jnp.minimum(z, 0.0) - jnp.log1p(jnp.exp(-jnp.abs(z)))) * (1.0 / GLA_TAU)
    qk = qk_ref[0]
    q = qk[:, :HK]
    k = qk[:, HK:]
    if t_valid < T:
        row = j * TT + lax.broadcasted_iota(jnp.int32, (TT, HK), 0)
        glog = jnp.where(row < t_valid, glog, 0.0)
        k = jnp.where(row < t_valid, k, 0.0)
    v = v_ref[0]
    gate = gate_ref[0]
    gg = gg_ref[...]
    ri = lax.broadcasted_iota(jnp.int32, (C, C), 0)
    ci = lax.broadcasted_iota(jnp.int32, (C, C), 1)
    causal = ri >= ci
    tri = causal.astype(F32)
    for c in range(TT // C):
        rows = slice(c * C, (c + 1) * C)
        bc = _dot_f32(tri, glog[rows])
        bl = bc[C - 1:C]
        qd = q[rows] * (GLA_DK ** -0.5) * jnp.exp(bc)
        kd = k[rows] * jnp.exp(-bc)
        kdl = k[rows] * jnp.exp(bl - bc)
        dec_t = jnp.broadcast_to(jnp.exp(bl), (8, HK)).T
        vc = v[rows]
        heads = []
        for h in range(GLA_HEADS):
            ks = slice(h * GLA_DK, (h + 1) * GLA_DK)
            vs = slice(h * GLA_DV, (h + 1) * GLA_DV)
            att = jnp.where(causal, _dot_nt(qd[:, ks], kd[:, ks]), 0.0)
            s_prev = s_scr[h]
            o_h = _dot(qd[:, ks], s_prev) + _dot(att, vc[:, vs])
            s_scr[h] = dec_t[ks, 0:1] * s_prev + _dot_tn(kdl[:, ks], vc[:, vs])
            ms = jnp.mean(o_h * o_h, axis=-1, keepdims=True)
            heads.append(o_h * lax.rsqrt(ms + NORM_EPS) * gg[:, vs])
        o_ref[0, rows, :] = jnp.concatenate(heads, axis=1) * _silu(gate[rows])
    sf_ref[0] = s_scr[...]


def _gla(p, w_lr, b_lr, g_gla, s0, *, TT, C, t_valid):
    B, T, _ = p["a_qk"].shape
    HK = GLA_HEADS * GLA_DK
    row = lambda w: pl.BlockSpec((1, TT, w), lambda b, j: (b, j, 0))
    full = lambda r, c: pl.BlockSpec((r, c), lambda b, j: (0, 0))
    st_spec = pl.BlockSpec((1, GLA_HEADS, GLA_DK, GLA_DV), lambda b, j: (b, 0, 0, 0))
    return pl.pallas_call(
        functools.partial(_gla_kernel, TT=TT, C=C, t_valid=t_valid, T=T),
        grid=(B, T // TT),
        in_specs=[row(2 * HK), row(GLA_WIDTH), row(GLA_WIDTH), row(LANES),
                  full(LANES, HK), full(1, HK), full(1, GLA_WIDTH), st_spec],
        out_specs=[row(GLA_WIDTH), st_spec],
        out_shape=[jax.ShapeDtypeStruct((B, T, GLA_WIDTH), F32),
                   jax.ShapeDtypeStruct((B, GLA_HEADS, GLA_DK, GLA_DV), F32)],
        scratch_shapes=[pltpu.VMEM((GLA_HEADS, GLA_DK, GLA_DV), F32)],
        compiler_params=_params("arbitrary", "arbitrary"),
        name="gla_mix",
    )(p["a_qk"], p["a_v"], p["a_gate"], p["a_lr"], w_lr, b_lr.reshape(1, HK),
      g_gla.reshape(1, GLA_WIDTH), s0)


def _swa_kernel(q_ref, k_ref, v_ref, g_ref, o_ref, qd_scr, kd_scr, vd_scr, op_scr, lse_scr, *, T):
    QB = SWA_QBLOCK
    CP = 256

    for p, (_, d) in enumerate(DILATED_PATTERNS):
        per_seq = T // d // CP

        def regroup(j, carry, p=p, d=d, per_seq=per_seq):
            r = j // per_seq
            c = j % per_seq
            start = r + d * CP * c
            src = pl.ds(start, CP) if d == 1 else pl.ds(start, CP, stride=d)
            dst = pl.ds(pl.multiple_of(j * CP, CP), CP)
            qd_scr[p, dst, :] = (q_ref[0, src, :] * (SWA_HD ** -0.5)).astype(BF16)
            kd_scr[p, dst, :] = k_ref[0, src, :].astype(BF16)
            vd_scr[p, dst, :] = v_ref[0, src, :].astype(BF16)
            return carry

        lax.fori_loop(0, T // CP, regroup, 0)

    ri = lax.broadcasted_iota(jnp.int32, (2 * QB, 2 * QB), 0) % QB
    ci = lax.broadcasted_iota(jnp.int32, (2 * QB, 2 * QB), 1)
    band = (ci >= ri) & (ci <= ri + QB)
    bias_band = jnp.where(band, 0.0, NEG_BIG)
    bias_first = jnp.where(band & (ci >= QB), 0.0, NEG_BIG)
    lo_q = lax.broadcasted_iota(jnp.int32, (QB, LANES), 1) < SWA_HD
    ones = jnp.ones((2 * QB, LANES), BF16)

    def step(i, carry):
        cur = pl.ds(pl.multiple_of(i * QB, QB), QB)
        prev = pl.ds(pl.multiple_of(jnp.maximum(i - 1, 0) * QB, QB), QB)
        for p, (_, d) in enumerate(DILATED_PATTERNS):
            nblk = T // d // QB
            r = i // nblk
            n = i % nblk
            q = qd_scr[p, cur, :]
            qz = jnp.concatenate([jnp.where(lo_q, q, 0), jnp.where(lo_q, 0, q)], axis=0)
            k = jnp.concatenate([kd_scr[p, prev, :], kd_scr[p, cur, :]], axis=0)
            v = jnp.concatenate([vd_scr[p, prev, :], vd_scr[p, cur, :]], axis=0)
            s = lax.dot_general(qz, k, (((1,), (1,)), ((), ())), preferred_element_type=F32)
            s = s + jnp.where(n > 0, bias_band, bias_first)
            m = jnp.max(s, axis=-1, keepdims=True)
            e = jnp.exp(s - m).astype(BF16)
            ox = jnp.dot(e, jnp.concatenate([v, ones], axis=1), preferred_element_type=F32)
            num = jnp.where(lo_q, ox[:QB, :LANES], ox[QB:, :LANES])
            den = jnp.where(lo_q, ox[:QB, LANES:], ox[QB:, LANES:])
            mx = jnp.where(lo_q, jnp.broadcast_to(m[:QB], (QB, LANES)),
                           jnp.broadcast_to(m[QB:], (QB, LANES)))
            start = r + d * QB * n
            dst = pl.ds(start, QB) if d == 1 else pl.ds(start, QB, stride=d)
            op_scr[p, dst, :] = num / den
            lse_scr[p, dst, :] = mx + jnp.log(den)
        return carry

    lax.fori_loop(0, T // QB, step, 0, unroll=4)

    CH = 512

    def combine(i, carry):
        rs = pl.ds(pl.multiple_of(i * CH, CH), CH)
        l0, l1, l2 = lse_scr[0, rs, :], lse_scr[1, rs, :], lse_scr[2, rs, :]
        mx = jnp.maximum(jnp.maximum(l0, l1), l2)
        w0, w1, w2 = jnp.exp(l0 - mx), jnp.exp(l1 - mx), jnp.exp(l2 - mx)
        mixed = (w0 * op_scr[0, rs, :] + w1 * op_scr[1, rs, :] + w2 * op_scr[2, rs, :]) / (w0 + w1 + w2)
        o_ref[0, rs, :] = mixed * _silu(g_ref[0, rs, :])
        return carry

    lax.fori_loop(0, T // CH, combine, 0)


def _swa_prompt(p):
    B, T, _ = p["b_q"].shape
    spec = pl.BlockSpec((1, T, LANES), lambda b, hp: (b, 0, hp))
    return pl.pallas_call(
        functools.partial(_swa_kernel, T=T),
        grid=(B, SWA_WIDTH // LANES),
        in_specs=[spec] * 4,
        out_specs=spec,
        out_shape=jax.ShapeDtypeStruct((B, T, SWA_WIDTH), F32),
        scratch_shapes=[pltpu.VMEM((len(DILATED_PATTERNS), T, LANES), BF16)] * 3
        + [pltpu.VMEM((len(DILATED_PATTERNS), T, LANES), F32)] * 2,
        compiler_params=_params("arbitrary", "arbitrary"),
        name="swa_prompt",
    )(p["b_q"], p["b_k"], p["b_v"], p["b_gate"])


def _swa_dec_kernel(ck_ref, cv_ref, q_ref, k_ref, v_ref, g_ref, mult_ref, multf_ref, *rest, L, Tn):
    o_ref, ko_ref, vo_ref = rest[-3:]
    HG = ck_ref.shape[2]
    lane = lax.broadcasted_iota(jnp.int32, (HG, SWA_HD, LANES), 2)

    def shifted(c_ref, new_rows):
        r = pltpu.roll(c_ref[0, 0], L - Tn, 2)
        moved = pltpu.roll(new_rows, SAMPLE_PAD - Tn, 0)
        pad = jnp.concatenate([jnp.zeros((LANES - SAMPLE_PAD, HG * SWA_HD), F32), moved], axis=0)
        tile = pad.T.reshape(HG, SWA_HD, LANES)
        last = jnp.where(lane >= LANES - Tn, tile, r[:, :, L - LANES:])
        return jnp.concatenate([r[:, :, :L - LANES], last], axis=2)

    kout = shifted(ck_ref, k_ref[0])
    vout = shifted(cv_ref, v_ref[0])
    ko_ref[0, 0] = kout
    vo_ref[0, 0] = vout

    q = q_ref[0] * (SWA_HD ** -0.5)
    q3 = jnp.stack([q[:, h * SWA_HD:(h + 1) * SWA_HD] for h in range(HG)], axis=0).astype(BF16)
    qk = lambda kk: jnp.einsum('hid,hdr->hir', q3, kk.astype(BF16), preferred_element_type=F32)
    pv = lambda pp, vv: jnp.einsum('hir,hdr->hid', pp.astype(BF16), vv.astype(BF16),
                                   preferred_element_type=F32)
    mult = mult_ref[...][None]
    multf = multf_ref[...][None]
    s = jnp.where(mult > 0, qk(kout), NEG_BIG)
    sf = jnp.where(multf > 0, qk(ck_ref[0, 0, :, :, 0:LANES]), NEG_BIG)
    m = jnp.maximum(jnp.max(s, axis=-1, keepdims=True), jnp.max(sf, axis=-1, keepdims=True))
    pe = jnp.exp(s - m) * mult
    pf = jnp.exp(sf - m) * multf
    den = jnp.sum(pe, axis=-1, keepdims=True) + jnp.sum(pf, axis=-1, keepdims=True)
    den = jnp.where(den > 0, den, 1.0)
    o3 = (pv(pe, vout) + pv(pf, cv_ref[0, 0, :, :, 0:LANES])) / den
    o = jnp.concatenate([o3[h] for h in range(HG)], axis=1)
    o_ref[0] = o * _silu(g_ref[0])


def _mult_tables(L, Tn):
    assert L == SWA_WMAX
    qpos = SWA_WMAX + np.arange(SAMPLE_PAD)[:, None]

    def count(kpos):
        off = qpos - kpos[None, :]
        cnt = np.zeros(off.shape, np.float32)
        for w, d in DILATED_PATTERNS:
            cnt += ((off >= 0) & (off <= w) & (off % d == 0)).astype(np.float32)
        cnt[Tn:] = 0.0
        return cnt

    mult = count(np.arange(L) + Tn)
    multf = count(np.arange(LANES))
    multf[:, Tn:] = 0.0
    return jnp.asarray(mult), jnp.asarray(multf)


def _swa_sample(p, cache_k, cache_v, l, Tn, bufs):
    B = p["b_q"].shape[0]
    L = cache_k.shape[-1]
    HG = 4
    mult, multf = _mult_tables(L, Tn)
    cspec = pl.BlockSpec((1, 1, HG, SWA_HD, L), lambda b, g: (l, b, g, 0, 0))
    nspec = pl.BlockSpec((1, SAMPLE_PAD, HG * SWA_HD), lambda b, g: (b, 0, g))
    const = lambda a: pl.BlockSpec(a.shape, lambda b, g: (0, 0))
    stacked = jax.ShapeDtypeStruct(cache_k.shape, F32)
    in_specs = [cspec, cspec, nspec, nspec, nspec, nspec, const(mult), const(multf)]
    args = [cache_k, cache_v, p["b_q"], p["b_k"], p["b_v"], p["b_gate"], mult, multf]
    aliases = {}
    if bufs is not None:
        in_specs += [pl.BlockSpec(memory_space=pl.ANY)] * 2
        aliases = {len(args): 1, len(args) + 1: 2}
        args += list(bufs)
    return pl.pallas_call(
        functools.partial(_swa_dec_kernel, L=L, Tn=Tn),
        grid=(B, SWA_HEADS // HG),
        in_specs=in_specs,
        out_specs=[nspec, cspec, cspec],
        out_shape=[jax.ShapeDtypeStruct((B, SAMPLE_PAD, SWA_WIDTH), F32), stacked, stacked],
        input_output_aliases=aliases,
        compiler_params=_params("arbitrary", "arbitrary"),
        name="swa_sample",
    )(*args)


def _s5prep_kernel(lr_ref, li_ref, ldt_ref, bre_ref, bim_ref, bcat_ref, tab_ref):
    lr = lr_ref[0]
    li = li_ref[0]
    dt = jnp.exp(ldt_ref[0])
    mag = jnp.exp(lr * dt)
    br = mag * jnp.cos(li * dt)
    bi = mag * jnp.sin(li * dt)
    den = lr * lr + li * li
    nr = br - 1.0
    cr = (nr * lr + bi * li) / den
    ci = (bi * lr - nr * li) / den
    bre = bre_ref[0]
    bim = bim_ref[0]
    bcat_ref[0, :, 0:SSM_FLAT] = (cr * bre - ci * bim).astype(BF16)
    bcat_ref[0, :, SSM_FLAT:2 * SSM_FLAT] = (cr * bim + ci * bre).astype(BF16)
    pows = [(br, bi)]
    for _ in range(7):
        pr, pi = pows[-1]
        pows.append((pr * br - pi * bi, pr * bi + pi * br))
    row = lax.broadcasted_iota(jnp.int32, (8, SSM_FLAT), 0)
    bc = lambda a: jnp.broadcast_to(a, (8, SSM_FLAT))
    for si, s in enumerate((1, 2, 4)):
        tab_ref[0, 2 * si] = jnp.where(row >= s, bc(pows[s - 1][0]), 0.0)
        tab_ref[0, 2 * si + 1] = jnp.where(row >= s, bc(pows[s - 1][1]), 0.0)
    lpr, lpi = bc(pows[0][0]), bc(pows[0][1])
    for kk in range(1, 8):
        lpr = jnp.where(row == kk, bc(pows[kk][0]), lpr)
        lpi = jnp.where(row == kk, bc(pows[kk][1]), lpi)
    tab_ref[0, 6] = lpr
    tab_ref[0, 7] = lpi


def _s5_prep(lam_re, lam_im, log_dt, b_re, b_im):
    eye = jnp.eye(SSM_GROUPS, dtype=F32)

    def block_diag_b(b):
        bt = jnp.transpose(b, (0, 1, 3, 2))
        return (bt[:, :, :, None, :] * eye[None, :, None, :, None]).reshape(DEPTH, SSM_WIDTH, SSM_FLAT)

    flat = lambda a: a.reshape(DEPTH, 1, SSM_FLAT)
    ldt = jnp.broadcast_to(log_dt[:, :, None], (DEPTH, SSM_GROUPS, SSM_STATE))
    vec = pl.BlockSpec((1, 1, SSM_FLAT), lambda l: (l, 0, 0))
    mat = pl.BlockSpec((1, SSM_WIDTH, SSM_FLAT), lambda l: (l, 0, 0))
    return pl.pallas_call(
        _s5prep_kernel,
        grid=(DEPTH,),
        in_specs=[vec, vec, vec, mat, mat],
        out_specs=[pl.BlockSpec((1, SSM_WIDTH, 2 * SSM_FLAT), lambda l: (l, 0, 0)),
                   pl.BlockSpec((1, 8, 8, SSM_FLAT), lambda l: (l, 0, 0, 0))],
        out_shape=[jax.ShapeDtypeStruct((DEPTH, SSM_WIDTH, 2 * SSM_FLAT), BF16),
                   jax.ShapeDtypeStruct((DEPTH, 8, 8, SSM_FLAT), F32)],
        compiler_params=_params("arbitrary"),
        name="s5_prep",
    )(flat(lam_re), flat(lam_im), flat(ldt), block_diag_b(b_re), block_diag_b(b_im))


def _s5_kernel(u_ref, gate_ref, x0r_ref, x0i_ref, bcat_ref, cre_ref, cim_ref, tab_ref, d_ref,
               wglu_ref, bglu_ref, o_ref, sr_ref, si_ref, xr_scr, xi_scr, cr_scr, ci_scr,
               *, TT, last_tile, last_group, last_row):
    j = pl.program_id(1)
    NG = TT // 8

    @pl.when(j == 0)
    def _():
        cr_scr[...] = jnp.broadcast_to(x0r_ref[0], (8, SSM_FLAT))
        ci_scr[...] = jnp.broadcast_to(x0i_ref[0], (8, SSM_FLAT))

    u = u_ref[0]
    bu = _dot(u, bcat_ref[...])
    xr = bu[:, :SSM_FLAT].reshape(NG, 8, SSM_FLAT)
    xi = bu[:, SSM_FLAT:].reshape(NG, 8, SSM_FLAT)
    for si in range(3):
        s = 1 << si
        ar = tab_ref[2 * si]
        ai = tab_ref[2 * si + 1]
        shr = pltpu.roll(xr, s, 1)
        shi = pltpu.roll(xi, s, 1)
        xr, xi = xr + ar * shr - ai * shi, xi + ar * shi + ai * shr
    xr_scr[...] = xr
    xi_scr[...] = xi
    lpr = tab_ref[6]
    lpi = tab_ref[7]

    def group(g, carry):
        cr, ci = carry
        nr = xr_scr[g] + lpr * cr - lpi * ci
        ni = xi_scr[g] + lpr * ci + lpi * cr
        xr_scr[g] = nr
        xi_scr[g] = ni
        return (jnp.broadcast_to(nr[7:8], (8, SSM_FLAT)), jnp.broadcast_to(ni[7:8], (8, SSM_FLAT)))

    cr, ci = lax.fori_loop(0, NG, group, (cr_scr[...], ci_scr[...]))
    cr_scr[...] = cr
    ci_scr[...] = ci

    @pl.when(j == last_tile)
    def _():
        sr_ref[0] = xr_scr[last_group][last_row:last_row + 1]
        si_ref[0] = xi_scr[last_group][last_row:last_row + 1]

    xr = xr_scr[...].reshape(TT, SSM_FLAT)
    xi = xi_scr[...].reshape(TT, SSM_FLAT)
    y = _dot(xr, cre_ref[...]) - _dot(xi, cim_ref[...]) + d_ref[...] * u
    z = jax.nn.gelu(y)
    out = z * jax.nn.sigmoid(_dot(z, wglu_ref[...]) + bglu_ref[...])
    o_ref[0] = out * _silu(gate_ref[0])


def _s5(p, x0r, x0i, bcat, cre, cim, tab, d_skip, w_glu, b_glu, *, TT, t_valid):
    B, T, _ = p["c_u"].shape
    last = t_valid - 1
    row = pl.BlockSpec((1, TT, SSM_WIDTH), lambda b, j: (b, j, 0))
    st = pl.BlockSpec((1, 1, SSM_FLAT), lambda b, j: (b, 0, 0))
    full = lambda a: pl.BlockSpec(a.shape, lambda b, j: (0,) * a.ndim)
    d2 = d_skip.reshape(1, SSM_WIDTH)
    bg2 = b_glu.reshape(1, SSM_WIDTH)
    return pl.pallas_call(
        functools.partial(_s5_kernel, TT=TT, last_tile=last // TT, last_group=(last % TT) // 8,
                          last_row=last % 8),
        grid=(B, T // TT),
        in_specs=[row, row, st, st, full(bcat), full(cre), full(cim), full(tab), full(d2),
                  full(w_glu), full(bg2)],
        out_specs=[row, st, st],
        out_shape=[jax.ShapeDtypeStruct((B, T, SSM_WIDTH), F32),
                   jax.ShapeDtypeStruct((B, 1, SSM_FLAT), F32),
                   jax.ShapeDtypeStruct((B, 1, SSM_FLAT), F32)],
        scratch_shapes=[pltpu.VMEM((TT // 8, 8, SSM_FLAT), F32), pltpu.VMEM((TT // 8, 8, SSM_FLAT), F32),
                        pltpu.VMEM((8, SSM_FLAT), F32), pltpu.VMEM((8, SSM_FLAT), F32)],
        compiler_params=_params("arbitrary", "arbitrary"),
        name="s5_mix",
    )(p["c_u"], p["c_gate"], x0r, x0i, bcat, cre, cim, tab, d2, w_glu, bg2)


def _out_kernel(oa_ref, ob_ref, oc_ref, x_ref, gate_ref, g_ref, w_ref, o_ref):
    a1 = GLA_WIDTH
    a2 = GLA_WIDTH + SWA_WIDTH
    y = (jnp.dot(oa_ref[0].astype(BF16), w_ref[0:a1, :], preferred_element_type=F32)
         + jnp.dot(ob_ref[0].astype(BF16), w_ref[a1:a2, :], preferred_element_type=F32)
         + jnp.dot(oc_ref[0].astype(BF16), w_ref[a2:, :], preferred_element_type=F32))
    ms = jnp.mean(y * y, axis=-1, keepdims=True)
    yn = y * lax.rsqrt(ms + NORM_EPS) * g_ref[...]
    o_ref[0] = x_ref[0] + gate_ref[0] * yn


def _out_proj(oa, ob, oc, x, gate, g_post, w_out, *, tm):
    nb, T, _ = x.shape
    R = gate.shape[1]
    row = lambda w: pl.BlockSpec((1, tm, w), lambda b, j: (b, j, 0))
    if R == 1:
        gate_spec = pl.BlockSpec((1, 1, D_MODEL), lambda b, j: (b, 0, 0))
    else:
        gate_spec = row(D_MODEL)
    return pl.pallas_call(
        _out_kernel,
        grid=(nb, T // tm),
        in_specs=[row(GLA_WIDTH), row(SWA_WIDTH), row(SSM_WIDTH), row(D_MODEL), gate_spec,
                  pl.BlockSpec((1, D_MODEL), lambda b, j: (0, 0)),
                  pl.BlockSpec((D_MODEL, D_MODEL), lambda b, j: (0, 0))],
        out_specs=row(D_MODEL),
        out_shape=jax.ShapeDtypeStruct((nb, T, D_MODEL), F32),
        compiler_params=_params("arbitrary", "arbitrary"),
        name="out_proj",
    )(oa, ob, oc, x, gate, g_post.reshape(1, D_MODEL), w_out)


def _rope_tables(pos):
    half = ROPE_DIMS // 2
    inv = ROPE_THETA ** (-jnp.arange(half, dtype=F32) * (2.0 / ROPE_DIMS))
    ang = pos.astype(F32)[:, None] * inv[None, :]
    cos, sin = jnp.cos(ang), jnp.sin(ang)
    n = pos.shape[0]
    rest = SWA_HD - ROPE_DIMS
    head = lambda a, b, fill: jnp.concatenate([a, b, jnp.full((n, rest), fill, F32)], axis=1)
    zero = jnp.zeros((n, half), F32)
    reps = LANES // SWA_HD
    return (jnp.tile(head(cos, cos, 1.0), (1, reps)),
            jnp.tile(head(-sin, zero, 0.0), (1, reps)),
            jnp.tile(head(zero, sin, 0.0), (1, reps)))


def _rearrange_w_in(w_in):
    GK = GLA_HEADS * GLA_DK
    o = np.cumsum([0, GK, GK, GLA_WIDTH, GLA_RANK, GLA_WIDTH, SWA_WIDTH, SWA_WIDTH, SWA_WIDTH,
                   SWA_WIDTH, SSM_WIDTH, SSM_WIDTH])
    piece = lambda i: w_in[:, :, o[i]:o[i + 1]]
    lr = jnp.pad(piece(3), ((0, 0), (0, 0), (0, LANES - GLA_RANK)))
    cat = jnp.concatenate([piece(0), piece(1), piece(2), piece(4), piece(5), piece(6), piece(7),
                           piece(8), piece(9), piece(10), lr], axis=-1)
    return cat.astype(BF16)


def kernel(x_prompt, x_sample, c_prompt, c_sample, state_gla, cache_swa_k, cache_swa_v, state_ssm_re, state_ssm_im, w_ada, b_ada, g_pre, g_post, w_in, w_gla_lr, b_gla_lr, g_gla, ssm_lambda_re, ssm_lambda_im, ssm_log_dt, ssm_b_re, ssm_b_im, ssm_c_re, ssm_c_im, ssm_d, w_glu, b_glu, w_out):
    Bp, Tp, _ = x_prompt.shape
    Bs, Ts, _ = x_sample.shape
    PAD = SAMPLE_PAD
    Lc = cache_swa_k.shape[2]

    w_cat = _rearrange_w_in(w_in)
    w_out_b = w_out.astype(BF16)
    w_glu_b = w_glu.astype(BF16)
    w_lr_pad = jnp.pad(w_gla_lr, ((0, 0), (0, LANES - GLA_RANK), (0, 0)))
    eye = jnp.eye(SSM_GROUPS, dtype=F32)

    def block_diag_c(c):
        ct = jnp.transpose(c, (0, 1, 3, 2))
        return (ct[:, :, :, None, :] * eye[None, :, None, :, None]).reshape(DEPTH, SSM_FLAT, SSM_WIDTH)

    cre = block_diag_c(ssm_c_re).astype(BF16)
    cim = block_diag_c(ssm_c_im).astype(BF16)
    bcat, tab = _s5_prep(ssm_lambda_re, ssm_lambda_im, ssm_log_dt, ssm_b_re, ssm_b_im)

    n_c = Bp + Bs
    c_rows = -(-n_c // 8) * 8
    c_all = jnp.pad(jnp.concatenate([c_prompt, c_sample], axis=0), ((0, c_rows - n_c), (0, 0)))
    mod = _ada_mod(c_all, w_ada, b_ada)

    tabs_p = _rope_tables(jnp.arange(Tp))
    tabs_s = _rope_tables(PAST_LEN + jnp.arange(Bs * PAD) % PAD)

    xp = x_prompt
    xs = jnp.pad(x_sample, ((0, 0), (0, PAD - Ts), (0, 0))).reshape(1, Bs * PAD, D_MODEL)
    seq_last = lambda a: jnp.transpose(a, (0, 1, 3, 4, 2))
    seq_first = lambda a: jnp.transpose(a, (0, 1, 4, 2, 3))
    cache_k = seq_last(cache_swa_k)
    cache_v = seq_last(cache_swa_v)
    zero_gla = jnp.zeros((Bp, GLA_HEADS, GLA_DK, GLA_DV), F32)
    zero_ssm = jnp.zeros((Bp, 1, SSM_FLAT), F32)

    acc = [[] for _ in range(6)]
    tails = None
    shifted = None
    for l in range(DEPTH):
        def chunk(rows, k):
            return mod[l, rows, k * D_MODEL:(k + 1) * D_MODEL]

        pr = slice(0, Bp)
        shift, scale, gate = (chunk(pr, k).reshape(Bp, 1, D_MODEL) for k in range(3))
        p = _proj(xp, scale, shift, g_pre[l], w_cat[l], tabs_p, tm=256, tail=(l, tails))
        tails = (p["k_tail"], p["v_tail"])
        oa, gla_p = _gla(p, w_lr_pad[l], b_gla_lr[l], g_gla[l], zero_gla, TT=256, C=GLA_CHUNK, t_valid=Tp)
        ob = _swa_prompt(p)
        oc, re_p, im_p = _s5(p, zero_ssm, zero_ssm, bcat[l], cre[l], cim[l], tab[l], ssm_d[l],
                             w_glu_b[l], b_glu[l], TT=256, t_valid=Tp)
        xp = _out_proj(oa, ob, oc, xp, gate, g_post[l], w_out_b[l], tm=512)

        sr = slice(Bp, Bp + Bs)
        shift, scale, gate = (jnp.repeat(chunk(sr, k), PAD, axis=0).reshape(1, Bs * PAD, D_MODEL)
                              for k in range(3))
        q = _proj(xs, scale, shift, g_pre[l], w_cat[l], tabs_s, tm=Bs * PAD)
        q = {n: a.reshape(Bs, PAD, a.shape[-1]) for n, a in q.items()}
        oa, gla_s = _gla(q, w_lr_pad[l], b_gla_lr[l], g_gla[l], state_gla[l], TT=PAD, C=PAD, t_valid=Ts)
        ob, k_s, v_s = _swa_sample(q, cache_k, cache_v, l, Ts, shifted)
        shifted = (k_s, v_s)
        oc, re_s, im_s = _s5(q, state_ssm_re[l].reshape(Bs, 1, SSM_FLAT),
                             state_ssm_im[l].reshape(Bs, 1, SSM_FLAT), bcat[l], cre[l], cim[l], tab[l],
                             ssm_d[l], w_glu_b[l], b_glu[l], TT=PAD, t_valid=Ts)
        flat = lambda a: a.reshape(1, Bs * PAD, a.shape[-1])
        xs = _out_proj(flat(oa), flat(ob), flat(oc), xs, gate, g_post[l], w_out_b[l], tm=Bs * PAD)

        for i, a in enumerate((gla_p, gla_s, re_p, im_p, re_s, im_s)):
            acc[i].append(a)

    st = [jnp.stack(a) for a in acc]
    ssm = lambda a: a.reshape(DEPTH, a.shape[1], SSM_GROUPS, SSM_STATE)
    y_sample = xs.reshape(Bs, PAD, D_MODEL)[:, :Ts]
    return (xp, y_sample, st[0], st[1], seq_first(tails[0]), seq_first(tails[1]),
            seq_first(shifted[0]), seq_first(shifted[1]),
            ssm(st[2]), ssm(st[3]), ssm(st[4]), ssm(st[5]))
```

```python
import functools

import numpy as np
import jax
import jax.numpy as jnp
from jax import lax
from jax.experimental import pallas as pl
from jax.experimental.pallas import tpu as pltpu

F32 = jnp.float32
BF16 = jnp.bfloat16
HIGHEST = lax.Precision.HIGHEST

D_MODEL = 1024
DEPTH = 4
PAST_LEN = 8192
GLA_HEADS = 4
GLA_DK = 32
GLA_DV = 64
GLA_WIDTH = GLA_HEADS * GLA_DV
GLA_RANK = 16
GLA_TAU = 16.0
GLA_CHUNK = 64
SWA_WIDTH = 512
SWA_HEADS = 8
SWA_HD = 64
DILATED_PATTERNS = ((128, 1), (512, 4), (2048, 16))
SWA_WMAX = 2048
SWA_QBLOCK = 128
ROPE_THETA = 500000.0
ROPE_DIMS = 16
SSM_WIDTH = 256
SSM_GROUP = 16
SSM_GROUPS = 16
SSM_STATE = 64
SSM_FLAT = SSM_GROUPS * SSM_STATE
NORM_EPS = 1e-6
NEG_BIG = -1e30

SAMPLE_PAD = 8
LANES = 128
VMEM_LIMIT = 48 * 1024 * 1024

W_COLS = (("a_qk", 256), ("a_v", 256), ("a_gate", 256), ("b_q", 512), ("b_k", 512),
          ("b_v", 512), ("b_gate", 512), ("c_u", 256), ("c_gate", 256), ("a_lr", 128))
W_OFF = {}
_o = 0
for _n, _w in W_COLS:
    W_OFF[_n] = (_o, _o + _w)
    _o += _w
W_TOTAL = _o


def _params(*sem):
    return pltpu.CompilerParams(dimension_semantics=sem, vmem_limit_bytes=VMEM_LIMIT)


def _dot(a, b):
    return jnp.dot(a.astype(BF16), b.astype(BF16), preferred_element_type=F32)


def _dot_nt(a, b):
    return lax.dot_general(a.astype(BF16), b.astype(BF16), (((1,), (1,)), ((), ())),
                           preferred_element_type=F32)


def _dot_tn(a, b):
    return lax.dot_general(a.astype(BF16), b.astype(BF16), (((0,), (0,)), ((), ())),
                           preferred_element_type=F32)


def _dot_f32(a, b):
    return jnp.dot(a, b, precision=HIGHEST, preferred_element_type=F32)


def _silu(x):
    return x * jax.nn.sigmoid(x)


def _mod_kernel(c_ref, w_ref, b_ref, o_ref):
    o_ref[0] = _dot_f32(_silu(c_ref[...]), w_ref[0]) + b_ref[0]


def _ada_mod(c_all, w_ada, b_ada):
    rows = c_all.shape[0]
    nt = 3 * D_MODEL // 1024
    return pl.pallas_call(
        _mod_kernel,
        grid=(DEPTH, nt),
        in_specs=[pl.BlockSpec((rows, D_MODEL), lambda l, n: (0, 0)),
                  pl.BlockSpec((1, D_MODEL, 1024), lambda l, n: (l, 0, n)),
                  pl.BlockSpec((1, 1, 1024), lambda l, n: (l, 0, n))],
        out_specs=pl.BlockSpec((1, rows, 1024), lambda l, n: (l, 0, n)),
        out_shape=jax.ShapeDtypeStruct((DEPTH, rows, 3 * D_MODEL), F32),
        compiler_params=_params("arbitrary", "arbitrary"),
        name="ada_mod",
    )(c_all, w_ada, b_ada.reshape(DEPTH, 1, 3 * D_MODEL))


def _rope(x, cos, sa, sb):
    outs = []
    for c in range(SWA_WIDTH // LANES):
        xc = x[:, LANES * c:LANES * (c + 1)]
        outs.append(xc * cos + pltpu.roll(xc, LANES - ROPE_DIMS // 2, 1) * sa
                    + pltpu.roll(xc, ROPE_DIMS // 2, 1) * sb)
    return jnp.concatenate(outs, axis=1)


def _proj_kernel(x_ref, sc_ref, sh_ref, g_ref, w_ref, cos_ref, sa_ref, sb_ref, *rest, emit_tail):
    n_out = len(W_COLS) + (2 if emit_tail else 0)
    outs = rest[len(rest) - n_out:]
    x = x_ref[0]
    ms = jnp.mean(x * x, axis=-1, keepdims=True)
    h = x * lax.rsqrt(ms + NORM_EPS) * g_ref[...]
    h = h * (1.0 + sc_ref[0]) + sh_ref[0]
    hb = h.astype(BF16)

    def mm(name):
        c0, c1 = W_OFF[name]
        return jnp.dot(hb, w_ref[:, c0:c1], preferred_element_type=F32)

    names = [n for n, _ in W_COLS]
    cos, sa, sb = cos_ref[...], sa_ref[...], sb_ref[...]
    for idx, name in enumerate(names):
        val = mm(name)
        if name in ("b_q", "b_k"):
            val = _rope(val, cos, sa, sb)
        outs[idx][0] = val
        if emit_tail and name == "b_k":
            outs[len(names)][0, 0] = val.T.reshape(SWA_HEADS, SWA_HD, val.shape[0])
        if emit_tail and name == "b_v":
            outs[len(names) + 1][0, 0] = val.T.reshape(SWA_HEADS, SWA_HD, val.shape[0])


def _proj(x, scale, shift, g_pre, w_cat, rope_tabs, *, tm, tail=None):
    nb, T, _ = x.shape
    nt = T // tm
    R = scale.shape[1]
    emit_tail = tail is not None
    if R == 1:
        mod_spec = pl.BlockSpec((1, 1, D_MODEL), lambda b, j: (b, 0, 0))
    else:
        mod_spec = pl.BlockSpec((1, tm, D_MODEL), lambda b, j: (b, j, 0))
    tab_spec = pl.BlockSpec((tm, LANES), lambda b, j: (j, 0))
    out_specs = [pl.BlockSpec((1, tm, w), lambda b, j: (b, j, 0)) for _, w in W_COLS]
    out_shape = [jax.ShapeDtypeStruct((nb, T, w), F32) for _, w in W_COLS]
    in_specs = [pl.BlockSpec((1, tm, D_MODEL), lambda b, j: (b, j, 0)),
                mod_spec, mod_spec,
                pl.BlockSpec((1, D_MODEL), lambda b, j: (0, 0)),
                pl.BlockSpec((D_MODEL, W_TOTAL), lambda b, j: (0, 0)),
                tab_spec, tab_spec, tab_spec]
    args = [x, scale, shift, g_pre.reshape(1, D_MODEL), w_cat, *rope_tabs]
    aliases = {}
    if emit_tail:
        l, bufs = tail
        keep = min(SWA_WMAX, T)
        first = (T - keep) // tm
        tail_spec = pl.BlockSpec((1, 1, SWA_HEADS, SWA_HD, tm),
                                 lambda b, j: (l, b, 0, 0, jnp.maximum(j - first, 0)))
        out_specs += [tail_spec, tail_spec]
        out_shape += [jax.ShapeDtypeStruct((DEPTH, nb, SWA_HEADS, SWA_HD, keep), F32)] * 2
        if bufs is not None:
            in_specs += [pl.BlockSpec(memory_space=pl.ANY)] * 2
            aliases = {len(args): len(W_COLS), len(args) + 1: len(W_COLS) + 1}
            args += list(bufs)
    res = pl.pallas_call(
        functools.partial(_proj_kernel, emit_tail=emit_tail),
        grid=(nb, nt),
        in_specs=in_specs,
        out_specs=out_specs,
        out_shape=out_shape,
        input_output_aliases=aliases,
        compiler_params=_params("arbitrary", "arbitrary"),
        name="in_proj",
    )(*args)
    names = [n for n, _ in W_COLS]
    d = dict(zip(names, res[:len(names)]))
    if emit_tail:
        d["k_tail"], d["v_tail"] = res[len(names)], res[len(names) + 1]
    return d


def _cumsum_rows(x, C):
    n = x.shape[0]
    blk = min(n, max(C, LANES))
    ri = lax.broadcasted_iota(jnp.int32, (blk, blk), 0)
    ci = lax.broadcasted_iota(jnp.int32, (blk, blk), 1)
    tri = jnp.where((ri >= ci) & (ri // C == ci // C), 1.0, 0.0).astype(BF16)
    hi = x.astype(BF16)
    r1 = x - hi.astype(F32)
    mid = r1.astype(BF16)
    lo = (r1 - mid.astype(F32)).astype(BF16)
    out = []
    for b in range(n // blk):
        rs = slice(b * blk, (b + 1) * blk)
        dot = lambda t: jnp.dot(tri, t[rs], preferred_element_type=F32)
        out.append(dot(hi) + dot(mid) + dot(lo))
    return jnp.concatenate(out, axis=0)


def _gla_kernel(qk_ref, v_ref, gate_ref, lr_ref, wlr_ref, blr_ref, gg_ref, s0_ref,
                o_ref, sf_ref, s_scr, o_scr, *, TT, C, t_valid, T):
    j = pl.program_id(1)
    H, DK, DV = GLA_HEADS, GLA_DK, GLA_DV
    HK, HV = H * DK, H * DV

    @pl.when(j == 0)
    def _():
        blocks = []
        for h in range(H):
            parts = [jnp.zeros((DK, DV), F32)] * H
            parts[h] = s0_ref[0, h]
            blocks.append(jnp.concatenate(parts, axis=1))
        s_scr[...] = jnp.concatenate(blocks, axis=0)

    z = _dot_f32(lr_ref[0], wlr_ref[...]) + blr_ref[...]
    glog = (jnp.minimum(z, 0.0) - jnp.log1p(jnp.exp(-jnp.abs(z)))) * (1.0 / GLA_TAU)
    qk = qk_ref[0]
    q = qk[:, :HK]
    k = qk[:, HK:]
    if t_valid < T:
        row = j * TT + lax.broadcasted_iota(jnp.int32, (TT, HK), 0)
        glog = jnp.where(row < t_valid, glog, 0.0)
        k = jnp.where(row < t_valid, k, 0.0)
    v = v_ref[0]
    nc = TT // C
    bc = _cumsum_rows(glog, C)
    bl = jnp.concatenate([jnp.broadcast_to(bc[(c + 1) * C - 1:(c + 1) * C], (C, HK))
                          for c in range(nc)], axis=0)
    qd = q * (DK ** -0.5) * jnp.exp(bc)
    kd = k * jnp.exp(-bc)
    kdl = k * jnp.exp(bl - bc)
    dec = jnp.exp(bl)

    i0 = lambda shape: lax.broadcasted_iota(jnp.int32, shape, 0)
    i1 = lambda shape: lax.broadcasted_iota(jnp.int32, shape, 1)
    own_cols = i0((H * C, HK)) // C == i1((H * C, HK)) // DK
    causal = i0((H * C, C)) % C >= i1((H * C, C))
    diag = i0((HK, HV)) // DK == i1((HK, HV)) // DV
    lane_head = i1((C, HV)) // DV
    s_prev = s_scr[...]
    for c in range(nc):
        rows = slice(c * C, (c + 1) * C)
        qc, kc, klc, vc = qd[rows], kd[rows], kdl[rows], v[rows]
        qz = jnp.where(own_cols, jnp.concatenate([qc] * H, axis=0), 0.0)
        att = jnp.where(causal, _dot_nt(qz, kc), 0.0)
        res = _dot(att, vc)
        intra = res[0:C]
        for h in range(1, H):
            intra = jnp.where(lane_head == h, res[h * C:(h + 1) * C], intra)
        o_scr[rows, :] = intra + _dot(qc, s_prev)
        dec_col = jnp.broadcast_to(dec[c * C:c * C + 1], (8, HK)).T[:, 0:1]
        s_prev = dec_col * s_prev + jnp.where(diag, _dot_tn(klc, vc), 0.0)
    s_scr[...] = s_prev

    o = o_scr[...]
    heads = []
    for h in range(H):
        seg = o[:, h * DV:(h + 1) * DV]
        ms = jnp.mean(seg * seg, axis=-1, keepdims=True)
        heads.append(seg * lax.rsqrt(ms + NORM_EPS))
    o_ref[0] = jnp.concatenate(heads, axis=1) * gg_ref[...] * _silu(gate_ref[0])
    for h in range(H):
        sf_ref[0, h] = s_scr[h * DK:(h + 1) * DK, h * DV:(h + 1) * DV]


def _gla(p, w_lr, b_lr, g_gla, s0, *, TT, C, t_valid):
    B, T, _ = p["a_qk"].shape
    HK = GLA_HEADS * GLA_DK
    row = lambda w: pl.BlockSpec((1, TT, w), lambda b, j: (b, j, 0))
    full = lambda r, c: pl.BlockSpec((r, c), lambda b, j: (0, 0))
    st_spec = pl.BlockSpec((1, GLA_HEADS, GLA_DK, GLA_DV), lambda b, j: (b, 0, 0, 0))
    return pl.pallas_call(
        functools.partial(_gla_kernel, TT=TT, C=C, t_valid=t_valid, T=T),
        grid=(B, T // TT),
        in_specs=[row(2 * HK), row(GLA_WIDTH), row(GLA_WIDTH), row(LANES),
                  full(LANES, HK), full(1, HK), full(1, GLA_WIDTH), st_spec],
        out_specs=[row(GLA_WIDTH), st_spec],
        out_shape=[jax.ShapeDtypeStruct((B, T, GLA_WIDTH), F32),
                   jax.ShapeDtypeStruct((B, GLA_HEADS, GLA_DK, GLA_DV), F32)],
        scratch_shapes=[pltpu.VMEM((HK, GLA_WIDTH), F32), pltpu.VMEM((TT, GLA_WIDTH), F32)],
        compiler_params=_params("arbitrary", "arbitrary"),
        name="gla_mix",
    )(p["a_qk"], p["a_v"], p["a_gate"], p["a_lr"], w_lr, b_lr.reshape(1, HK),
      g_gla.reshape(1, GLA_WIDTH), s0)


def _swa_kernel(q_ref, k_ref, v_ref, g_ref, o_ref, qd_scr, kd_scr, vd_scr, op_scr, lse_scr, *, T):
    QB = SWA_QBLOCK
    CP = 256

    for p, (_, d) in enumerate(DILATED_PATTERNS):
        per_seq = T // d // CP

        def regroup(j, carry, p=p, d=d, per_seq=per_seq):
            r = j // per_seq
            c = j % per_seq
            start = r + d * CP * c
            src = pl.ds(start, CP) if d == 1 else pl.ds(start, CP, stride=d)
            dst = pl.ds(pl.multiple_of(j * CP, CP), CP)
            qd_scr[p, dst, :] = (q_ref[0, src, :] * (SWA_HD ** -0.5)).astype(BF16)
            kd_scr[p, dst, :] = k_ref[0, src, :].astype(BF16)
            vd_scr[p, dst, :] = v_ref[0, src, :].astype(BF16)
            return carry

        lax.fori_loop(0, T // CP, regroup, 0)

    ri = lax.broadcasted_iota(jnp.int32, (2 * QB, 2 * QB), 0) % QB
    ci = lax.broadcasted_iota(jnp.int32, (2 * QB, 2 * QB), 1)
    band = (ci >= ri) & (ci <= ri + QB)
    bias_band = jnp.where(band, 0.0, NEG_BIG)
    bias_first = jnp.where(band & (ci >= QB), 0.0, NEG_BIG)
    lo_q = lax.broadcasted_iota(jnp.int32, (QB, LANES), 1) < SWA_HD
    ones = jnp.ones((2 * QB, LANES), BF16)

    def step(i, carry):
        cur = pl.ds(pl.multiple_of(i * QB, QB), QB)
        prev = pl.ds(pl.multiple_of(jnp.maximum(i - 1, 0) * QB, QB), QB)
        for p, (_, d) in enumerate(DILATED_PATTERNS):
            nblk = T // d // QB
            r = i // nblk
            n = i % nblk
            q = qd_scr[p, cur, :]
            qz = jnp.concatenate([jnp.where(lo_q, q, 0), jnp.where(lo_q, 0, q)], axis=0)
            k = jnp.concatenate([kd_scr[p, prev, :], kd_scr[p, cur, :]], axis=0)
            v = jnp.concatenate([vd_scr[p, prev, :], vd_scr[p, cur, :]], axis=0)
            s = lax.dot_general(qz, k, (((1,), (1,)), ((), ())), preferred_element_type=F32)
            s = s + jnp.where(n > 0, bias_band, bias_first)
            m = jnp.max(s, axis=-1, keepdims=True)
            e = jnp.exp(s - m).astype(BF16)
            ox = jnp.dot(e, jnp.concatenate([v, ones], axis=1), preferred_element_type=F32)
            num = jnp.where(lo_q, ox[:QB, :LANES], ox[QB:, :LANES])
            den = jnp.where(lo_q, ox[:QB, LANES:], ox[QB:, LANES:])
            mx = jnp.where(lo_q, jnp.broadcast_to(m[:QB], (QB, LANES)),
                           jnp.broadcast_to(m[QB:], (QB, LANES)))
            start = r + d * QB * n
            dst = pl.ds(start, QB) if d == 1 else pl.ds(start, QB, stride=d)
            op_scr[p, dst, :] = num / den
            lse_scr[p, dst, :] = mx + jnp.log(den)
        return carry

    lax.fori_loop(0, T // QB, step, 0, unroll=4)

    CH = 512

    def combine(i, carry):
        rs = pl.ds(pl.multiple_of(i * CH, CH), CH)
        l0, l1, l2 = lse_scr[0, rs, :], lse_scr[1, rs, :], lse_scr[2, rs, :]
        mx = jnp.maximum(jnp.maximum(l0, l1), l2)
        w0, w1, w2 = jnp.exp(l0 - mx), jnp.exp(l1 - mx), jnp.exp(l2 - mx)
        mixed = (w0 * op_scr[0, rs, :] + w1 * op_scr[1, rs, :] + w2 * op_scr[2, rs, :]) / (w0 + w1 + w2)
        o_ref[0, rs, :] = mixed * _silu(g_ref[0, rs, :])
        return carry

    lax.fori_loop(0, T // CH, combine, 0)


def _swa_prompt(p):
    B, T, _ = p["b_q"].shape
    spec = pl.BlockSpec((1, T, LANES), lambda b, hp: (b, 0, hp))
    return pl.pallas_call(
        functools.partial(_swa_kernel, T=T),
        grid=(B, SWA_WIDTH // LANES),
        in_specs=[spec] * 4,
        out_specs=spec,
        out_shape=jax.ShapeDtypeStruct((B, T, SWA_WIDTH), F32),
        scratch_shapes=[pltpu.VMEM((len(DILATED_PATTERNS), T, LANES), BF16)] * 3
        + [pltpu.VMEM((len(DILATED_PATTERNS), T, LANES), F32)] * 2,
        compiler_params=_params("arbitrary", "arbitrary"),
        name="swa_prompt",
    )(p["b_q"], p["b_k"], p["b_v"], p["b_gate"])


def _swa_dec_kernel(ck_ref, cv_ref, q_ref, k_ref, v_ref, g_ref, mult_ref, multf_ref, *rest, L, Tn):
    o_ref, ko_ref, vo_ref = rest[-3:]
    HG = ck_ref.shape[2]
    lane = lax.broadcasted_iota(jnp.int32, (HG, SWA_HD, LANES), 2)

    def shifted(c_ref, new_rows):
        r = pltpu.roll(c_ref[0, 0], L - Tn, 2)
        moved = pltpu.roll(new_rows, SAMPLE_PAD - Tn, 0)
        pad = jnp.concatenate([jnp.zeros((LANES - SAMPLE_PAD, HG * SWA_HD), F32), moved], axis=0)
        tile = pad.T.reshape(HG, SWA_HD, LANES)
        last = jnp.where(lane >= LANES - Tn, tile, r[:, :, L - LANES:])
        return jnp.concatenate([r[:, :, :L - LANES], last], axis=2)

    kout = shifted(ck_ref, k_ref[0])
    vout = shifted(cv_ref, v_ref[0])
    ko_ref[0, 0] = kout
    vo_ref[0, 0] = vout

    q = q_ref[0] * (SWA_HD ** -0.5)
    q3 = jnp.stack([q[:, h * SWA_HD:(h + 1) * SWA_HD] for h in range(HG)], axis=0).astype(BF16)
    qk = lambda kk: jnp.einsum('hid,hdr->hir', q3, kk.astype(BF16), preferred_element_type=F32)
    pv = lambda pp, vv: jnp.einsum('hir,hdr->hid', pp.astype(BF16), vv.astype(BF16),
                                   preferred_element_type=F32)
    mult = mult_ref[...][None]
    multf = multf_ref[...][None]
    s = jnp.where(mult > 0, qk(kout), NEG_BIG)
    sf = jnp.where(multf > 0, qk(ck_ref[0, 0, :, :, 0:LANES]), NEG_BIG)
    m = jnp.maximum(jnp.max(s, axis=-1, keepdims=True), jnp.max(sf, axis=-1, keepdims=True))
    pe = jnp.exp(s - m) * mult
    pf = jnp.exp(sf - m) * multf
    den = jnp.sum(pe, axis=-1, keepdims=True) + jnp.sum(pf, axis=-1, keepdims=True)
    den = jnp.where(den > 0, den, 1.0)
    o3 = (pv(pe, vout) + pv(pf, cv_ref[0, 0, :, :, 0:LANES])) / den
    o = jnp.concatenate([o3[h] for h in range(HG)], axis=1)
    o_ref[0] = o * _silu(g_ref[0])


def _mult_tables(L, Tn):
    assert L == SWA_WMAX
    qpos = SWA_WMAX + np.arange(SAMPLE_PAD)[:, None]

    def count(kpos):
        off = qpos - kpos[None, :]
        cnt = np.zeros(off.shape, np.float32)
        for w, d in DILATED_PATTERNS:
            cnt += ((off >= 0) & (off <= w) & (off % d == 0)).astype(np.float32)
        cnt[Tn:] = 0.0
        return cnt

    mult = count(np.arange(L) + Tn)
    multf = count(np.arange(LANES))
    multf[:, Tn:] = 0.0
    return jnp.asarray(mult), jnp.asarray(multf)


def _swa_sample(p, cache_k, cache_v, l, Tn, bufs):
    B = p["b_q"].shape[0]
    L = cache_k.shape[-1]
    HG = 4
    mult, multf = _mult_tables(L, Tn)
    cspec = pl.BlockSpec((1, 1, HG, SWA_HD, L), lambda b, g: (l, b, g, 0, 0))
    nspec = pl.BlockSpec((1, SAMPLE_PAD, HG * SWA_HD), lambda b, g: (b, 0, g))
    const = lambda a: pl.BlockSpec(a.shape, lambda b, g: (0, 0))
    stacked = jax.ShapeDtypeStruct(cache_k.shape, F32)
    in_specs = [cspec, cspec, nspec, nspec, nspec, nspec, const(mult), const(multf)]
    args = [cache_k, cache_v, p["b_q"], p["b_k"], p["b_v"], p["b_gate"], mult, multf]
    aliases = {}
    if bufs is not None:
        in_specs += [pl.BlockSpec(memory_space=pl.ANY)] * 2
        aliases = {len(args): 1, len(args) + 1: 2}
        args += list(bufs)
    return pl.pallas_call(
        functools.partial(_swa_dec_kernel, L=L, Tn=Tn),
        grid=(B, SWA_HEADS // HG),
        in_specs=in_specs,
        out_specs=[nspec, cspec, cspec],
        out_shape=[jax.ShapeDtypeStruct((B, SAMPLE_PAD, SWA_WIDTH), F32), stacked, stacked],
        input_output_aliases=aliases,
        compiler_params=_params("arbitrary", "arbitrary"),
        name="swa_sample",
    )(*args)


def _s5prep_kernel(lr_ref, li_ref, ldt_ref, bre_ref, bim_ref, bcat_ref, tab_ref):
    lr = lr_ref[0]
    li = li_ref[0]
    dt = jnp.exp(ldt_ref[0])
    mag = jnp.exp(lr * dt)
    br = mag * jnp.cos(li * dt)
    bi = mag * jnp.sin(li * dt)
    den = lr * lr + li * li
    nr = br - 1.0
    cr = (nr * lr + bi * li) / den
    ci = (bi * lr - nr * li) / den
    bre = bre_ref[0]
    bim = bim_ref[0]
    bcat_ref[0, :, 0:SSM_FLAT] = (cr * bre - ci * bim).astype(BF16)
    bcat_ref[0, :, SSM_FLAT:2 * SSM_FLAT] = (cr * bim + ci * bre).astype(BF16)
    tab_ref[0, 0] = jnp.broadcast_to(br, (8, SSM_FLAT))
    tab_ref[0, 1] = jnp.broadcast_to(bi, (8, SSM_FLAT))


def _s5_prep(lam_re, lam_im, log_dt, b_re, b_im):
    eye = jnp.eye(SSM_GROUPS, dtype=F32)

    def block_diag_b(b):
        bt = jnp.transpose(b, (0, 1, 3, 2))
        return (bt[:, :, :, None, :] * eye[None, :, None, :, None]).reshape(DEPTH, SSM_WIDTH, SSM_FLAT)

    flat = lambda a: a.reshape(DEPTH, 1, SSM_FLAT)
    ldt = jnp.broadcast_to(log_dt[:, :, None], (DEPTH, SSM_GROUPS, SSM_STATE))
    vec = pl.BlockSpec((1, 1, SSM_FLAT), lambda l: (l, 0, 0))
    mat = pl.BlockSpec((1, SSM_WIDTH, SSM_FLAT), lambda l: (l, 0, 0))
    return pl.pallas_call(
        _s5prep_kernel,
        grid=(DEPTH,),
        in_specs=[vec, vec, vec, mat, mat],
        out_specs=[pl.BlockSpec((1, SSM_WIDTH, 2 * SSM_FLAT), lambda l: (l, 0, 0)),
                   pl.BlockSpec((1, 2, 8, SSM_FLAT), lambda l: (l, 0, 0, 0))],
        out_shape=[jax.ShapeDtypeStruct((DEPTH, SSM_WIDTH, 2 * SSM_FLAT), BF16),
                   jax.ShapeDtypeStruct((DEPTH, 2, 8, SSM_FLAT), F32)],
        compiler_params=_params("arbitrary"),
        name="s5_prep",
    )(flat(lam_re), flat(lam_im), flat(ldt), block_diag_b(b_re), block_diag_b(b_im))


def _s5_kernel(u_ref, gate_ref, x0r_ref, x0i_ref, bcat_ref, cre_ref, cim_ref, tab_ref, d_ref,
               wglu_ref, bglu_ref, o_ref, sr_ref, si_ref, xr_scr, xi_scr, cr_scr, ci_scr,
               *, TT, B, last_tile, last_t):
    j = pl.program_id(0)
    NBG, NL = xr_scr.shape[0], xr_scr.shape[1]
    split = lambda a: jnp.stack([a[:, LANES * i:LANES * (i + 1)] for i in range(NL)], axis=0)
    merge = lambda a: jnp.concatenate([a[i] for i in range(NL)], axis=1)

    @pl.when(j == 0)
    def _():
        xr_scr[...] = jnp.zeros(xr_scr.shape, F32)
        xi_scr[...] = jnp.zeros(xi_scr.shape, F32)
        for g in range(NBG):
            cr_scr[g] = split(x0r_ref[g])
            ci_scr[g] = split(x0i_ref[g])

    u = u_ref[...].reshape(B * TT, SSM_WIDTH)
    bu = _dot(u, bcat_ref[...])
    for b in range(B):
        g, s = divmod(b, 8)
        rows = pl.ds(s, TT, stride=8)
        for i in range(NL):
            xr_scr[g, i, rows, :] = bu[b * TT:(b + 1) * TT, LANES * i:LANES * (i + 1)]
            xi_scr[g, i, rows, :] = bu[b * TT:(b + 1) * TT, SSM_FLAT + LANES * i:SSM_FLAT + LANES * (i + 1)]

    lam_r = split(tab_ref[0])
    lam_i = split(tab_ref[1])

    def step(t, carry):
        rows = pl.ds(pl.multiple_of(t * 8, 8), 8)
        out = []
        for g in range(NBG):
            cr, ci = carry[2 * g], carry[2 * g + 1]
            nr = lam_r * cr - lam_i * ci + xr_scr[g, :, rows, :]
            ni = lam_r * ci + lam_i * cr + xi_scr[g, :, rows, :]
            xr_scr[g, :, rows, :] = nr
            xi_scr[g, :, rows, :] = ni
            out += [nr, ni]
        return tuple(out)

    init = tuple(ref[g] for g in range(NBG) for ref in (cr_scr, ci_scr))
    fin = lax.fori_loop(0, TT, step, init, unroll=min(TT, 8))
    for g in range(NBG):
        cr_scr[g] = fin[2 * g]
        ci_scr[g] = fin[2 * g + 1]

    @pl.when(j == last_tile)
    def _():
        rows = slice(last_t * 8, last_t * 8 + 8)
        for g in range(NBG):
            sr_ref[g] = merge(xr_scr[g, :, rows, :])
            si_ref[g] = merge(xi_scr[g, :, rows, :])

    def gather(scr):
        per_b = []
        for b in range(B):
            g, s = divmod(b, 8)
            rows = pl.ds(s, TT, stride=8)
            per_b.append(jnp.concatenate([scr[g, i, rows, :] for i in range(NL)], axis=1))
        return jnp.concatenate(per_b, axis=0)

    y = _dot(gather(xr_scr), cre_ref[...]) - _dot(gather(xi_scr), cim_ref[...]) + d_ref[...] * u
    z = jax.nn.gelu(y)
    out = z * jax.nn.sigmoid(_dot(z, wglu_ref[...]) + bglu_ref[...])
    out = out * _silu(gate_ref[...].reshape(B * TT, SSM_WIDTH))
    o_ref[...] = out.reshape(B, TT, SSM_WIDTH)


def _s5(p, x0r, x0i, bcat, cre, cim, tab, d_skip, w_glu, b_glu, *, TT, t_valid):
    B, T, _ = p["c_u"].shape
    NBG = x0r.shape[0]
    NL = SSM_FLAT // LANES
    last = t_valid - 1
    row = pl.BlockSpec((B, TT, SSM_WIDTH), lambda j: (0, j, 0))
    st = pl.BlockSpec((NBG, 8, SSM_FLAT), lambda j: (0, 0, 0))
    full = lambda a: pl.BlockSpec(a.shape, lambda j: (0,) * a.ndim)
    d2 = d_skip.reshape(1, SSM_WIDTH)
    bg2 = b_glu.reshape(1, SSM_WIDTH)
    state = jax.ShapeDtypeStruct((NBG, 8, SSM_FLAT), F32)
    return pl.pallas_call(
        functools.partial(_s5_kernel, TT=TT, B=B, last_tile=last // TT, last_t=last % TT),
        grid=(T // TT,),
        in_specs=[row, row, st, st, full(bcat), full(cre), full(cim), full(tab), full(d2),
                  full(w_glu), full(bg2)],
        out_specs=[row, st, st],
        out_shape=[jax.ShapeDtypeStruct((B, T, SSM_WIDTH), F32), state, state],
        scratch_shapes=[pltpu.VMEM((NBG, NL, TT * 8, LANES), F32)] * 2
        + [pltpu.VMEM((NBG, NL, 8, LANES), F32)] * 2,
        compiler_params=_params("arbitrary"),
        name="s5_mix",
    )(p["c_u"], p["c_gate"], x0r, x0i, bcat, cre, cim, tab, d2, w_glu, bg2)


def _out_kernel(oa_ref, ob_ref, oc_ref, x_ref, gate_ref, g_ref, w_ref, o_ref):
    a1 = GLA_WIDTH
    a2 = GLA_WIDTH + SWA_WIDTH
    y = (jnp.dot(oa_ref[0].astype(BF16), w_ref[0:a1, :], preferred_element_type=F32)
         + jnp.dot(ob_ref[0].astype(BF16), w_ref[a1:a2, :], preferred_element_type=F32)
         + jnp.dot(oc_ref[0].astype(BF16), w_ref[a2:, :], preferred_element_type=F32))
    ms = jnp.mean(y * y, axis=-1, keepdims=True)
    yn = y * lax.rsqrt(ms + NORM_EPS) * g_ref[...]
    o_ref[0] = x_ref[0] + gate_ref[0] * yn


def _out_proj(oa, ob, oc, x, gate, g_post, w_out, *, tm):
    nb, T, _ = x.shape
    R = gate.shape[1]
    row = lambda w: pl.BlockSpec((1, tm, w), lambda b, j: (b, j, 0))
    if R == 1:
        gate_spec = pl.BlockSpec((1, 1, D_MODEL), lambda b, j: (b, 0, 0))
    else:
        gate_spec = row(D_MODEL)
    return pl.pallas_call(
        _out_kernel,
        grid=(nb, T // tm),
        in_specs=[row(GLA_WIDTH), row(SWA_WIDTH), row(SSM_WIDTH), row(D_MODEL), gate_spec,
                  pl.BlockSpec((1, D_MODEL), lambda b, j: (0, 0)),
                  pl.BlockSpec((D_MODEL, D_MODEL), lambda b, j: (0, 0))],
        out_specs=row(D_MODEL),
        out_shape=jax.ShapeDtypeStruct((nb, T, D_MODEL), F32),
        compiler_params=_params("arbitrary", "arbitrary"),
        name="out_proj",
    )(oa, ob, oc, x, gate, g_post.reshape(1, D_MODEL), w_out)


def _rope_tables(pos):
    half = ROPE_DIMS // 2
    inv = ROPE_THETA ** (-jnp.arange(half, dtype=F32) * (2.0 / ROPE_DIMS))
    ang = pos.astype(F32)[:, None] * inv[None, :]
    cos, sin = jnp.cos(ang), jnp.sin(ang)
    n = pos.shape[0]
    rest = SWA_HD - ROPE_DIMS
    head = lambda a, b, fill: jnp.concatenate([a, b, jnp.full((n, rest), fill, F32)], axis=1)
    zero = jnp.zeros((n, half), F32)
    reps = LANES // SWA_HD
    return (jnp.tile(head(cos, cos, 1.0), (1, reps)),
            jnp.tile(head(-sin, zero, 0.0), (1, reps)),
            jnp.tile(head(zero, sin, 0.0), (1, reps)))


def _rearrange_w_in(w_in):
    GK = GLA_HEADS * GLA_DK
    o = np.cumsum([0, GK, GK, GLA_WIDTH, GLA_RANK, GLA_WIDTH, SWA_WIDTH, SWA_WIDTH, SWA_WIDTH,
                   SWA_WIDTH, SSM_WIDTH, SSM_WIDTH])
    piece = lambda i: w_in[:, :, o[i]:o[i + 1]]
    lr = jnp.pad(piece(3), ((0, 0), (0, 0), (0, LANES - GLA_RANK)))
    cat = jnp.concatenate([piece(0), piece(1), piece(2), piece(4), piece(5), piece(6), piece(7),
                           piece(8), piece(9), piece(10), lr], axis=-1)
    return cat.astype(BF16)


def kernel(x_prompt, x_sample, c_prompt, c_sample, state_gla, cache_swa_k, cache_swa_v, state_ssm_re, state_ssm_im, w_ada, b_ada, g_pre, g_post, w_in, w_gla_lr, b_gla_lr, g_gla, ssm_lambda_re, ssm_lambda_im, ssm_log_dt, ssm_b_re, ssm_b_im, ssm_c_re, ssm_c_im, ssm_d, w_glu, b_glu, w_out):
    Bp, Tp, _ = x_prompt.shape
    Bs, Ts, _ = x_sample.shape
    PAD = SAMPLE_PAD
    Lc = cache_swa_k.shape[2]

    w_cat = _rearrange_w_in(w_in)
    w_out_b = w_out.astype(BF16)
    w_glu_b = w_glu.astype(BF16)
    w_lr_pad = jnp.pad(w_gla_lr, ((0, 0), (0, LANES - GLA_RANK), (0, 0)))
    eye = jnp.eye(SSM_GROUPS, dtype=F32)

    def block_diag_c(c):
        ct = jnp.transpose(c, (0, 1, 3, 2))
        return (ct[:, :, :, None, :] * eye[None, :, None, :, None]).reshape(DEPTH, SSM_FLAT, SSM_WIDTH)

    cre = block_diag_c(ssm_c_re).astype(BF16)
    cim = block_diag_c(ssm_c_im).astype(BF16)
    bcat, tab = _s5_prep(ssm_lambda_re, ssm_lambda_im, ssm_log_dt, ssm_b_re, ssm_b_im)

    n_c = Bp + Bs
    c_rows = -(-n_c // 8) * 8
    c_all = jnp.pad(jnp.concatenate([c_prompt, c_sample], axis=0), ((0, c_rows - n_c), (0, 0)))
    mod = _ada_mod(c_all, w_ada, b_ada)

    tabs_p = _rope_tables(jnp.arange(Tp))
    tabs_s = _rope_tables(PAST_LEN + jnp.arange(Bs * PAD) % PAD)

    xp = x_prompt
    xs = jnp.pad(x_sample, ((0, 0), (0, PAD - Ts), (0, 0))).reshape(1, Bs * PAD, D_MODEL)
    seq_last = lambda a: jnp.transpose(a, (0, 1, 3, 4, 2))
    seq_first = lambda a: jnp.transpose(a, (0, 1, 4, 2, 3))
    cache_k = seq_last(cache_swa_k)
    cache_v = seq_last(cache_swa_v)
    zero_gla = jnp.zeros((Bp, GLA_HEADS, GLA_DK, GLA_DV), F32)
    groups = lambda n: -(-n // 8)
    ssm_in = lambda a: jnp.pad(a.reshape(a.shape[0], SSM_FLAT),
                               ((0, groups(a.shape[0]) * 8 - a.shape[0]), (0, 0))
                               ).reshape(groups(a.shape[0]), 8, SSM_FLAT)
    zero_ssm = jnp.zeros((groups(Bp), 8, SSM_FLAT), F32)

    acc = [[] for _ in range(6)]
    tails = None
    shifted = None
    for l in range(DEPTH):
        def chunk(rows, k):
            return mod[l, rows, k * D_MODEL:(k + 1) * D_MODEL]

        pr = slice(0, Bp)
        shift, scale, gate = (chunk(pr, k).reshape(Bp, 1, D_MODEL) for k in range(3))
        p = _proj(xp, scale, shift, g_pre[l], w_cat[l], tabs_p, tm=256, tail=(l, tails))
        tails = (p["k_tail"], p["v_tail"])
        oa, gla_p = _gla(p, w_lr_pad[l], b_gla_lr[l], g_gla[l], zero_gla, TT=1024, C=GLA_CHUNK, t_valid=Tp)
        ob = _swa_prompt(p)
        oc, re_p, im_p = _s5(p, zero_ssm, zero_ssm, bcat[l], cre[l], cim[l], tab[l], ssm_d[l],
                             w_glu_b[l], b_glu[l], TT=128, t_valid=Tp)
        xp = _out_proj(oa, ob, oc, xp, gate, g_post[l], w_out_b[l], tm=512)

        sr = slice(Bp, Bp + Bs)
        shift, scale, gate = (jnp.repeat(chunk(sr, k), PAD, axis=0).reshape(1, Bs * PAD, D_MODEL)
                              for k in range(3))
        q = _proj(xs, scale, shift, g_pre[l], w_cat[l], tabs_s, tm=Bs * PAD)
        q = {n: a.reshape(Bs, PAD, a.shape[-1]) for n, a in q.items()}
        oa, gla_s = _gla(q, w_lr_pad[l], b_gla_lr[l], g_gla[l], state_gla[l], TT=PAD, C=PAD, t_valid=Ts)
        ob, k_s, v_s = _swa_sample(q, cache_k, cache_v, l, Ts, shifted)
        shifted = (k_s, v_s)
        oc, re_s, im_s = _s5(q, ssm_in(state_ssm_re[l]), ssm_in(state_ssm_im[l]), bcat[l], cre[l],
                             cim[l], tab[l], ssm_d[l], w_glu_b[l], b_glu[l], TT=PAD, t_valid=Ts)
        flat = lambda a: a.reshape(1, Bs * PAD, a.shape[-1])
        xs = _out_proj(flat(oa), flat(ob), flat(oc), xs, gate, g_post[l], w_out_b[l], tm=Bs * PAD)

        for i, a in enumerate((gla_p, gla_s, re_p, im_p, re_s, im_s)):
            acc[i].append(a)

    st = [jnp.stack(a) for a in acc]
    ssm = lambda a, n: a.reshape(DEPTH, -1, SSM_GROUPS, SSM_STATE)[:, :n]
    y_sample = xs.reshape(Bs, PAD, D_MODEL)[:, :Ts]
    return (xp, y_sample, st[0], st[1], seq_first(tails[0]), seq_first(tails[1]),
            seq_first(shifted[0]), seq_first(shifted[1]),
            ssm(st[2], Bp), ssm(st[3], Bp), ssm(st[4], Bs), ssm(st[5], Bs))
```

```python
import functools

import numpy as np
import jax
import jax.numpy as jnp
from jax import lax
from jax.experimental import pallas as pl
from jax.experimental.pallas import tpu as pltpu

F32 = jnp.float32
BF16 = jnp.bfloat16
HIGHEST = lax.Precision.HIGHEST

D_MODEL = 1024
DEPTH = 4
PAST_LEN = 8192
GLA_HEADS = 4
GLA_DK = 32
GLA_DV = 64
GLA_WIDTH = GLA_HEADS * GLA_DV
GLA_RANK = 16
GLA_TAU = 16.0
GLA_CHUNK = 64
SWA_WIDTH = 512
SWA_HEADS = 8
SWA_HD = 64
DILATED_PATTERNS = ((128, 1), (512, 4), (2048, 16))
SWA_WMAX = 2048
SWA_QBLOCK = 128
ROPE_THETA = 500000.0
ROPE_DIMS = 16
SSM_WIDTH = 256
SSM_GROUP = 16
SSM_GROUPS = 16
SSM_STATE = 64
SSM_FLAT = SSM_GROUPS * SSM_STATE
NORM_EPS = 1e-6
NEG_BIG = -1e30

SAMPLE_PAD = 8
LANES = 128
VMEM_LIMIT = 48 * 1024 * 1024
MIX_DTYPE = BF16

W_COLS = (("a_qk", 256), ("a_v", 256), ("a_gate", 256), ("b_q", 512), ("b_k", 512),
          ("b_v", 512), ("b_gate", 512), ("c_u", 256), ("c_gate", 256), ("a_lr", 128))
W_OFF = {}
_o = 0
for _n, _w in W_COLS:
    W_OFF[_n] = (_o, _o + _w)
    _o += _w
W_TOTAL = _o


def _params(*sem):
    return pltpu.CompilerParams(dimension_semantics=sem, vmem_limit_bytes=VMEM_LIMIT)


def _dot(a, b):
    return jnp.dot(a.astype(BF16), b.astype(BF16), preferred_element_type=F32)


def _dot_nt(a, b):
    return lax.dot_general(a.astype(BF16), b.astype(BF16), (((1,), (1,)), ((), ())),
                           preferred_element_type=F32)


def _dot_tn(a, b):
    return lax.dot_general(a.astype(BF16), b.astype(BF16), (((0,), (0,)), ((), ())),
                           preferred_element_type=F32)


def _dot_f32(a, b):
    return jnp.dot(a, b, precision=HIGHEST, preferred_element_type=F32)


def _silu(x):
    return x * jax.nn.sigmoid(x)


def _mod_kernel(c_ref, w_ref, b_ref, o_ref):
    o_ref[0] = _dot_f32(_silu(c_ref[...]), w_ref[0]) + b_ref[0]


def _ada_mod(c_all, w_ada, b_ada):
    rows = c_all.shape[0]
    nt = 3 * D_MODEL // 1024
    return pl.pallas_call(
        _mod_kernel,
        grid=(DEPTH, nt),
        in_specs=[pl.BlockSpec((rows, D_MODEL), lambda l, n: (0, 0)),
                  pl.BlockSpec((1, D_MODEL, 1024), lambda l, n: (l, 0, n)),
                  pl.BlockSpec((1, 1, 1024), lambda l, n: (l, 0, n))],
        out_specs=pl.BlockSpec((1, rows, 1024), lambda l, n: (l, 0, n)),
        out_shape=jax.ShapeDtypeStruct((DEPTH, rows, 3 * D_MODEL), F32),
        compiler_params=_params("arbitrary", "arbitrary"),
        name="ada_mod",
    )(c_all, w_ada, b_ada.reshape(DEPTH, 1, 3 * D_MODEL))


def _rope(x, cos, sa, sb):
    outs = []
    for c in range(SWA_WIDTH // LANES):
        xc = x[:, LANES * c:LANES * (c + 1)]
        outs.append(xc * cos + pltpu.roll(xc, LANES - ROPE_DIMS // 2, 1) * sa
                    + pltpu.roll(xc, ROPE_DIMS // 2, 1) * sb)
    return jnp.concatenate(outs, axis=1)


def _proj_kernel(x_ref, sc_ref, sh_ref, g_ref, w_ref, cos_ref, sa_ref, sb_ref, *rest, emit_tail):
    n_out = len(W_COLS) + (2 if emit_tail else 0)
    outs = rest[len(rest) - n_out:]
    x = x_ref[0]
    ms = jnp.mean(x * x, axis=-1, keepdims=True)
    h = x * lax.rsqrt(ms + NORM_EPS) * g_ref[...]
    h = h * (1.0 + sc_ref[0]) + sh_ref[0]
    hb = h.astype(BF16)

    def mm(name):
        c0, c1 = W_OFF[name]
        return jnp.dot(hb, w_ref[:, c0:c1], preferred_element_type=F32)

    names = [n for n, _ in W_COLS]
    cos, sa, sb = cos_ref[...], sa_ref[...], sb_ref[...]
    for idx, name in enumerate(names):
        val = mm(name)
        if name in ("b_q", "b_k"):
            val = _rope(val, cos, sa, sb)
        outs[idx][0] = val
        if emit_tail and name == "b_k":
            outs[len(names)][0, 0] = val.T.reshape(SWA_HEADS, SWA_HD, val.shape[0])
        if emit_tail and name == "b_v":
            outs[len(names) + 1][0, 0] = val.T.reshape(SWA_HEADS, SWA_HD, val.shape[0])


def _proj(x, scale, shift, g_pre, w_cat, rope_tabs, *, tm, tail=None):
    nb, T, _ = x.shape
    nt = T // tm
    R = scale.shape[1]
    emit_tail = tail is not None
    if R == 1:
        mod_spec = pl.BlockSpec((1, 1, D_MODEL), lambda b, j: (b, 0, 0))
    else:
        mod_spec = pl.BlockSpec((1, tm, D_MODEL), lambda b, j: (b, j, 0))
    tab_spec = pl.BlockSpec((tm, LANES), lambda b, j: (j, 0))
    out_specs = [pl.BlockSpec((1, tm, w), lambda b, j: (b, j, 0)) for _, w in W_COLS]
    out_shape = [jax.ShapeDtypeStruct((nb, T, w), F32) for _, w in W_COLS]
    in_specs = [pl.BlockSpec((1, tm, D_MODEL), lambda b, j: (b, j, 0)),
                mod_spec, mod_spec,
                pl.BlockSpec((1, D_MODEL), lambda b, j: (0, 0)),
                pl.BlockSpec((D_MODEL, W_TOTAL), lambda b, j: (0, 0)),
                tab_spec, tab_spec, tab_spec]
    args = [x, scale, shift, g_pre.reshape(1, D_MODEL), w_cat, *rope_tabs]
    aliases = {}
    if emit_tail:
        l, bufs = tail
        keep = min(SWA_WMAX, T)
        first = (T - keep) // tm
        tail_spec = pl.BlockSpec((1, 1, SWA_HEADS, SWA_HD, tm),
                                 lambda b, j: (l, b, 0, 0, jnp.maximum(j - first, 0)))
        out_specs += [tail_spec, tail_spec]
        out_shape += [jax.ShapeDtypeStruct((DEPTH, nb, SWA_HEADS, SWA_HD, keep), F32)] * 2
        if bufs is not None:
            in_specs += [pl.BlockSpec(memory_space=pl.ANY)] * 2
            aliases = {len(args): len(W_COLS), len(args) + 1: len(W_COLS) + 1}
            args += list(bufs)
    res = pl.pallas_call(
        functools.partial(_proj_kernel, emit_tail=emit_tail),
        grid=(nb, nt),
        in_specs=in_specs,
        out_specs=out_specs,
        out_shape=out_shape,
        input_output_aliases=aliases,
        compiler_params=_params("arbitrary", "arbitrary"),
        name="in_proj",
    )(*args)
    names = [n for n, _ in W_COLS]
    d = dict(zip(names, res[:len(names)]))
    if emit_tail:
        d["k_tail"], d["v_tail"] = res[len(names)], res[len(names) + 1]
    return d


def _cumsum_rows(x, C):
    n = x.shape[0]
    blk = min(n, max(C, LANES))
    ri = lax.broadcasted_iota(jnp.int32, (blk, blk), 0)
    ci = lax.broadcasted_iota(jnp.int32, (blk, blk), 1)
    tri = jnp.where((ri >= ci) & (ri // C == ci // C), 1.0, 0.0).astype(BF16)
    hi = x.astype(BF16)
    r1 = x - hi.astype(F32)
    mid = r1.astype(BF16)
    lo = (r1 - mid.astype(F32)).astype(BF16)
    out = []
    for b in range(n // blk):
        rs = slice(b * blk, (b + 1) * blk)
        dot = lambda t: jnp.dot(tri, t[rs], preferred_element_type=F32)
        out.append(dot(hi) + dot(mid) + dot(lo))
    return jnp.concatenate(out, axis=0)


def _gla_kernel(qk_ref, v_ref, gate_ref, lr_ref, wlr_ref, blr_ref, gg_ref, s0_ref,
                o_ref, sf_ref, s_scr, o_scr, *, TT, C, t_valid, T):
    j = pl.program_id(1)
    H, DK, DV = GLA_HEADS, GLA_DK, GLA_DV
    HK, HV = H * DK, H * DV

    @pl.when(j == 0)
    def _():
        blocks = []
        for h in range(H):
            parts = [jnp.zeros((DK, DV), F32)] * H
            parts[h] = s0_ref[0, h]
            blocks.append(jnp.concatenate(parts, axis=1))
        s_scr[...] = jnp.concatenate(blocks, axis=0)

    z = _dot_f32(lr_ref[0], wlr_ref[...]) + blr_ref[...]
    glog = (jnp.minimum(z, 0.0) - jnp.log1p(jnp.exp(-jnp.abs(z)))) * (1.0 / GLA_TAU)
    qk = qk_ref[0]
    q = qk[:, :HK]
    k = qk[:, HK:]
    if t_valid < T:
        row = j * TT + lax.broadcasted_iota(jnp.int32, (TT, HK), 0)
        glog = jnp.where(row < t_valid, glog, 0.0)
        k = jnp.where(row < t_valid, k, 0.0)
    v = v_ref[0]
    nc = TT // C
    bc = _cumsum_rows(glog, C)
    bl = jnp.concatenate([jnp.broadcast_to(bc[(c + 1) * C - 1:(c + 1) * C], (C, HK))
                          for c in range(nc)], axis=0)
    qd = q * (DK ** -0.5) * jnp.exp(bc)
    kd = k * jnp.exp(-bc)
    kdl = k * jnp.exp(bl - bc)
    dec = jnp.exp(bl)

    i0 = lambda shape: lax.broadcasted_iota(jnp.int32, shape, 0)
    i1 = lambda shape: lax.broadcasted_iota(jnp.int32, shape, 1)
    own_cols = i0((H * C, HK)) // C == i1((H * C, HK)) // DK
    causal = i0((H * C, C)) % C >= i1((H * C, C))
    diag = i0((HK, HV)) // DK == i1((HK, HV)) // DV
    lane_head = i1((C, HV)) // DV
    s_prev = s_scr[...]
    for c in range(nc):
        rows = slice(c * C, (c + 1) * C)
        qc, kc, klc, vc = qd[rows], kd[rows], kdl[rows], v[rows]
        qz = jnp.where(own_cols, jnp.concatenate([qc] * H, axis=0), 0.0)
        att = jnp.where(causal, _dot_nt(qz, kc), 0.0)
        res = _dot(att, vc)
        intra = res[0:C]
        for h in range(1, H):
            intra = jnp.where(lane_head == h, res[h * C:(h + 1) * C], intra)
        o_scr[rows, :] = intra + _dot(qc, s_prev)
        dec_col = jnp.broadcast_to(dec[c * C:c * C + 1], (8, HK)).T[:, 0:1]
        s_prev = dec_col * s_prev + jnp.where(diag, _dot_tn(klc, vc), 0.0)
    s_scr[...] = s_prev

    o = o_scr[...]
    heads = []
    for h in range(H):
        seg = o[:, h * DV:(h + 1) * DV]
        ms = jnp.mean(seg * seg, axis=-1, keepdims=True)
        heads.append(seg * lax.rsqrt(ms + NORM_EPS))
    o_ref[0] = (jnp.concatenate(heads, axis=1) * gg_ref[...] * _silu(gate_ref[0])).astype(o_ref.dtype)
    for h in range(H):
        sf_ref[0, h] = s_scr[h * DK:(h + 1) * DK, h * DV:(h + 1) * DV]


def _gla(p, w_lr, b_lr, g_gla, s0, *, TT, C, t_valid):
    B, T, _ = p["a_qk"].shape
    HK = GLA_HEADS * GLA_DK
    row = lambda w: pl.BlockSpec((1, TT, w), lambda b, j: (b, j, 0))
    full = lambda r, c: pl.BlockSpec((r, c), lambda b, j: (0, 0))
    st_spec = pl.BlockSpec((1, GLA_HEADS, GLA_DK, GLA_DV), lambda b, j: (b, 0, 0, 0))
    return pl.pallas_call(
        functools.partial(_gla_kernel, TT=TT, C=C, t_valid=t_valid, T=T),
        grid=(B, T // TT),
        in_specs=[row(2 * HK), row(GLA_WIDTH), row(GLA_WIDTH), row(LANES),
                  full(LANES, HK), full(1, HK), full(1, GLA_WIDTH), st_spec],
        out_specs=[row(GLA_WIDTH), st_spec],
        out_shape=[jax.ShapeDtypeStruct((B, T, GLA_WIDTH), MIX_DTYPE),
                   jax.ShapeDtypeStruct((B, GLA_HEADS, GLA_DK, GLA_DV), F32)],
        scratch_shapes=[pltpu.VMEM((HK, GLA_WIDTH), F32), pltpu.VMEM((TT, GLA_WIDTH), F32)],
        compiler_params=_params("arbitrary", "arbitrary"),
        name="gla_mix",
    )(p["a_qk"], p["a_v"], p["a_gate"], p["a_lr"], w_lr, b_lr.reshape(1, HK),
      g_gla.reshape(1, GLA_WIDTH), s0)


def _swa_kernel(q_ref, k_ref, v_ref, g_ref, o_ref, qd_scr, kd_scr, vd_scr, op_scr, lse_scr,
                stage_scr, *, T):
    QB = SWA_QBLOCK
    CP = 256
    (_, d0), (_, d1), (_, d2) = DILATED_PATTERNS
    assert d0 == 1 and d2 % d1 == 0
    dd = d2 // d1

    def regroup(src_ref, dst_scr, scale):
        cast = lambda x: (x * scale).astype(BF16)
        dst = lambda j: pl.ds(pl.multiple_of(j * CP, CP), CP)

        def copy0(j, carry):
            dst_scr[0, dst(j), :] = cast(src_ref[0, dst(j), :])
            return carry

        def copy1(j, carry):
            per = T // d1 // CP
            x = src_ref[0, pl.ds(j // per + d1 * CP * (j % per), CP, stride=d1), :]
            stage_scr[dst(j), :] = x
            dst_scr[1, dst(j), :] = cast(x)
            return carry

        def copy2(j, carry):
            per = T // d2 // CP
            r = j // per
            start = (r % d1) * (T // d1) + r // d1 + dd * CP * (j % per)
            dst_scr[2, dst(j), :] = cast(stage_scr[pl.ds(start, CP, stride=dd), :])
            return carry

        for body in (copy0, copy1, copy2):
            lax.fori_loop(0, T // CP, body, 0)

    regroup(q_ref, qd_scr, SWA_HD ** -0.5)
    regroup(k_ref, kd_scr, 1.0)
    regroup(v_ref, vd_scr, 1.0)

    ri = lax.broadcasted_iota(jnp.int32, (2 * QB, 2 * QB), 0) % QB
    ci = lax.broadcasted_iota(jnp.int32, (2 * QB, 2 * QB), 1)
    band = (ci >= ri) & (ci <= ri + QB)
    bias_band = jnp.where(band, 0.0, NEG_BIG)
    bias_first = jnp.where(band & (ci >= QB), 0.0, NEG_BIG)
    lo_q = lax.broadcasted_iota(jnp.int32, (QB, LANES), 1) < SWA_HD
    ones = jnp.ones((2 * QB, LANES), BF16)

    def step(i, carry):
        cur = pl.ds(pl.multiple_of(i * QB, QB), QB)
        prev = pl.ds(pl.multiple_of(jnp.maximum(i - 1, 0) * QB, QB), QB)
        for p, (_, d) in enumerate(DILATED_PATTERNS):
            nblk = T // d // QB
            r = i // nblk
            n = i % nblk
            q = qd_scr[p, cur, :]
            qz = jnp.concatenate([jnp.where(lo_q, q, 0), jnp.where(lo_q, 0, q)], axis=0)
            k = jnp.concatenate([kd_scr[p, prev, :], kd_scr[p, cur, :]], axis=0)
            v = jnp.concatenate([vd_scr[p, prev, :], vd_scr[p, cur, :]], axis=0)
            s = lax.dot_general(qz, k, (((1,), (1,)), ((), ())), preferred_element_type=F32)
            s = s + jnp.where(n > 0, bias_band, bias_first)
            m = jnp.max(s, axis=-1, keepdims=True)
            e = jnp.exp(s - m).astype(BF16)
            ox = jnp.dot(e, jnp.concatenate([v, ones], axis=1), preferred_element_type=F32)
            num = jnp.where(lo_q, ox[:QB, :LANES], ox[QB:, :LANES])
            den = jnp.where(lo_q, ox[:QB, LANES:], ox[QB:, LANES:])
            mx = jnp.where(lo_q, jnp.broadcast_to(m[:QB], (QB, LANES)),
                           jnp.broadcast_to(m[QB:], (QB, LANES)))
            start = r + d * QB * n
            dst = pl.ds(start, QB) if d == 1 else pl.ds(start, QB, stride=d)
            op_scr[p, dst, :] = num / den
            lse_scr[p, dst, :] = mx + jnp.log(den)
        return carry

    lax.fori_loop(0, T // QB, step, 0, unroll=4)

    CH = 512

    def combine(i, carry):
        rs = pl.ds(pl.multiple_of(i * CH, CH), CH)
        l0, l1, l2 = lse_scr[0, rs, :], lse_scr[1, rs, :], lse_scr[2, rs, :]
        mx = jnp.maximum(jnp.maximum(l0, l1), l2)
        w0, w1, w2 = jnp.exp(l0 - mx), jnp.exp(l1 - mx), jnp.exp(l2 - mx)
        mixed = (w0 * op_scr[0, rs, :] + w1 * op_scr[1, rs, :] + w2 * op_scr[2, rs, :]) / (w0 + w1 + w2)
        o_ref[0, rs, :] = (mixed * _silu(g_ref[0, rs, :])).astype(o_ref.dtype)
        return carry

    lax.fori_loop(0, T // CH, combine, 0)


def _swa_prompt(p):
    B, T, _ = p["b_q"].shape
    spec = pl.BlockSpec((1, T, LANES), lambda b, hp: (b, 0, hp))
    return pl.pallas_call(
        functools.partial(_swa_kernel, T=T),
        grid=(B, SWA_WIDTH // LANES),
        in_specs=[spec] * 4,
        out_specs=spec,
        out_shape=jax.ShapeDtypeStruct((B, T, SWA_WIDTH), MIX_DTYPE),
        scratch_shapes=[pltpu.VMEM((len(DILATED_PATTERNS), T, LANES), BF16)] * 3
        + [pltpu.VMEM((len(DILATED_PATTERNS), T, LANES), F32)] * 2
        + [pltpu.VMEM((T, LANES), F32)],
        compiler_params=_params("arbitrary", "arbitrary"),
        name="swa_prompt",
    )(p["b_q"], p["b_k"], p["b_v"], p["b_gate"])


def _swa_dec_kernel(ck_ref, cv_ref, q_ref, k_ref, v_ref, g_ref, mult_ref, multf_ref, *rest, L, Tn):
    o_ref, ko_ref, vo_ref = rest[-3:]
    HG = ck_ref.shape[2]
    lane = lax.broadcasted_iota(jnp.int32, (HG, SWA_HD, LANES), 2)

    def shifted(c_ref, new_rows):
        r = pltpu.roll(c_ref[0, 0], L - Tn, 2)
        moved = pltpu.roll(new_rows, SAMPLE_PAD - Tn, 0)
        pad = jnp.concatenate([jnp.zeros((LANES - SAMPLE_PAD, HG * SWA_HD), F32), moved], axis=0)
        tile = pad.T.reshape(HG, SWA_HD, LANES)
        last = jnp.where(lane >= LANES - Tn, tile, r[:, :, L - LANES:])
        return jnp.concatenate([r[:, :, :L - LANES], last], axis=2)

    kout = shifted(ck_ref, k_ref[0])
    vout = shifted(cv_ref, v_ref[0])
    ko_ref[0, 0] = kout
    vo_ref[0, 0] = vout

    q = q_ref[0] * (SWA_HD ** -0.5)
    q3 = jnp.stack([q[:, h * SWA_HD:(h + 1) * SWA_HD] for h in range(HG)], axis=0).astype(BF16)
    qk = lambda kk: jnp.einsum('hid,hdr->hir', q3, kk.astype(BF16), preferred_element_type=F32)
    pv = lambda pp, vv: jnp.einsum('hir,hdr->hid', pp.astype(BF16), vv.astype(BF16),
                                   preferred_element_type=F32)
    mult = mult_ref[...][None]
    multf = multf_ref[...][None]
    s = jnp.where(mult > 0, qk(kout), NEG_BIG)
    sf = jnp.where(multf > 0, qk(ck_ref[0, 0, :, :, 0:LANES]), NEG_BIG)
    m = jnp.maximum(jnp.max(s, axis=-1, keepdims=True), jnp.max(sf, axis=-1, keepdims=True))
    pe = jnp.exp(s - m) * mult
    pf = jnp.exp(sf - m) * multf
    den = jnp.sum(pe, axis=-1, keepdims=True) + jnp.sum(pf, axis=-1, keepdims=True)
    den = jnp.where(den > 0, den, 1.0)
    o3 = (pv(pe, vout) + pv(pf, cv_ref[0, 0, :, :, 0:LANES])) / den
    o = jnp.concatenate([o3[h] for h in range(HG)], axis=1)
    o_ref[0] = (o * _silu(g_ref[0])).astype(o_ref.dtype)


def _mult_tables(L, Tn):
    assert L == SWA_WMAX
    qpos = SWA_WMAX + np.arange(SAMPLE_PAD)[:, None]

    def count(kpos):
        off = qpos - kpos[None, :]
        cnt = np.zeros(off.shape, np.float32)
        for w, d in DILATED_PATTERNS:
            cnt += ((off >= 0) & (off <= w) & (off % d == 0)).astype(np.float32)
        cnt[Tn:] = 0.0
        return cnt

    mult = count(np.arange(L) + Tn)
    multf = count(np.arange(LANES))
    multf[:, Tn:] = 0.0
    return jnp.asarray(mult), jnp.asarray(multf)


def _swa_sample(p, cache_k, cache_v, l, Tn, bufs):
    B = p["b_q"].shape[0]
    L = cache_k.shape[-1]
    HG = 8
    mult, multf = _mult_tables(L, Tn)
    cspec = pl.BlockSpec((1, 1, HG, SWA_HD, L), lambda b, g: (l, b, g, 0, 0))
    nspec = pl.BlockSpec((1, SAMPLE_PAD, HG * SWA_HD), lambda b, g: (b, 0, g))
    const = lambda a: pl.BlockSpec(a.shape, lambda b, g: (0, 0))
    stacked = jax.ShapeDtypeStruct(cache_k.shape, F32)
    in_specs = [cspec, cspec, nspec, nspec, nspec, nspec, const(mult), const(multf)]
    args = [cache_k, cache_v, p["b_q"], p["b_k"], p["b_v"], p["b_gate"], mult, multf]
    aliases = {}
    if bufs is not None:
        in_specs += [pl.BlockSpec(memory_space=pl.ANY)] * 2
        aliases = {len(args): 1, len(args) + 1: 2}
        args += list(bufs)
    return pl.pallas_call(
        functools.partial(_swa_dec_kernel, L=L, Tn=Tn),
        grid=(B, SWA_HEADS // HG),
        in_specs=in_specs,
        out_specs=[nspec, cspec, cspec],
        out_shape=[jax.ShapeDtypeStruct((B, SAMPLE_PAD, SWA_WIDTH), MIX_DTYPE), stacked, stacked],
        input_output_aliases=aliases,
        compiler_params=_params("arbitrary", "arbitrary"),
        name="swa_sample",
    )(*args)


def _s5prep_kernel(lr_ref, li_ref, ldt_ref, bre_ref, bim_ref, bcat_ref, tab_ref):
    lr = lr_ref[0]
    li = li_ref[0]
    dt = jnp.exp(ldt_ref[0])
    mag = jnp.exp(lr * dt)
    br = mag * jnp.cos(li * dt)
    bi = mag * jnp.sin(li * dt)
    den = lr * lr + li * li
    nr = br - 1.0
    cr = (nr * lr + bi * li) / den
    ci = (bi * lr - nr * li) / den
    bre = bre_ref[0]
    bim = bim_ref[0]
    bcat_ref[0, :, 0:SSM_FLAT] = (cr * bre - ci * bim).astype(BF16)
    bcat_ref[0, :, SSM_FLAT:2 * SSM_FLAT] = (cr * bim + ci * bre).astype(BF16)
    tab_ref[0, 0] = jnp.broadcast_to(br, (8, SSM_FLAT))
    tab_ref[0, 1] = jnp.broadcast_to(bi, (8, SSM_FLAT))


def _s5_prep(lam_re, lam_im, log_dt, b_re, b_im):
    eye = jnp.eye(SSM_GROUPS, dtype=F32)

    def block_diag_b(b):
        bt = jnp.transpose(b, (0, 1, 3, 2))
        return (bt[:, :, :, None, :] * eye[None, :, None, :, None]).reshape(DEPTH, SSM_WIDTH, SSM_FLAT)

    flat = lambda a: a.reshape(DEPTH, 1, SSM_FLAT)
    ldt = jnp.broadcast_to(log_dt[:, :, None], (DEPTH, SSM_GROUPS, SSM_STATE))
    vec = pl.BlockSpec((1, 1, SSM_FLAT), lambda l: (l, 0, 0))
    mat = pl.BlockSpec((1, SSM_WIDTH, SSM_FLAT), lambda l: (l, 0, 0))
    return pl.pallas_call(
        _s5prep_kernel,
        grid=(DEPTH,),
        in_specs=[vec, vec, vec, mat, mat],
        out_specs=[pl.BlockSpec((1, SSM_WIDTH, 2 * SSM_FLAT), lambda l: (l, 0, 0)),
                   pl.BlockSpec((1, 2, 8, SSM_FLAT), lambda l: (l, 0, 0, 0))],
        out_shape=[jax.ShapeDtypeStruct((DEPTH, SSM_WIDTH, 2 * SSM_FLAT), BF16),
                   jax.ShapeDtypeStruct((DEPTH, 2, 8, SSM_FLAT), F32)],
        compiler_params=_params("arbitrary"),
        name="s5_prep",
    )(flat(lam_re), flat(lam_im), flat(ldt), block_diag_b(b_re), block_diag_b(b_im))


def _s5_kernel(u_ref, gate_ref, x0r_ref, x0i_ref, bcat_ref, cre_ref, cim_ref, tab_ref, d_ref,
               wglu_ref, bglu_ref, o_ref, sr_ref, si_ref, xr_scr, xi_scr, cr_scr, ci_scr,
               *, TT, B, last_tile, last_t):
    j = pl.program_id(0)
    NBG, NL = xr_scr.shape[0], xr_scr.shape[1]
    split = lambda a: jnp.stack([a[:, LANES * i:LANES * (i + 1)] for i in range(NL)], axis=0)
    merge = lambda a: jnp.concatenate([a[i] for i in range(NL)], axis=1)

    @pl.when(j == 0)
    def _():
        xr_scr[...] = jnp.zeros(xr_scr.shape, F32)
        xi_scr[...] = jnp.zeros(xi_scr.shape, F32)
        for g in range(NBG):
            cr_scr[g] = split(x0r_ref[g])
            ci_scr[g] = split(x0i_ref[g])

    u = u_ref[...].reshape(B * TT, SSM_WIDTH)
    bu = _dot(u, bcat_ref[...])
    for b in range(B):
        g, s = divmod(b, 8)
        rows = pl.ds(s, TT, stride=8)
        for i in range(NL):
            xr_scr[g, i, rows, :] = bu[b * TT:(b + 1) * TT, LANES * i:LANES * (i + 1)]
            xi_scr[g, i, rows, :] = bu[b * TT:(b + 1) * TT, SSM_FLAT + LANES * i:SSM_FLAT + LANES * (i + 1)]

    lam_r = split(tab_ref[0])
    lam_i = split(tab_ref[1])

    def step(t, carry):
        rows = pl.ds(pl.multiple_of(t * 8, 8), 8)
        out = []
        for g in range(NBG):
            cr, ci = carry[2 * g], carry[2 * g + 1]
            nr = lam_r * cr - lam_i * ci + xr_scr[g, :, rows, :]
            ni = lam_r * ci + lam_i * cr + xi_scr[g, :, rows, :]
            xr_scr[g, :, rows, :] = nr
            xi_scr[g, :, rows, :] = ni
            out += [nr, ni]
        return tuple(out)

    init = tuple(ref[g] for g in range(NBG) for ref in (cr_scr, ci_scr))
    fin = lax.fori_loop(0, TT, step, init, unroll=min(TT, 8))
    for g in range(NBG):
        cr_scr[g] = fin[2 * g]
        ci_scr[g] = fin[2 * g + 1]

    @pl.when(j == last_tile)
    def _():
        rows = slice(last_t * 8, last_t * 8 + 8)
        for g in range(NBG):
            sr_ref[g] = merge(xr_scr[g, :, rows, :])
            si_ref[g] = merge(xi_scr[g, :, rows, :])

    def gather(scr):
        per_b = []
        for b in range(B):
            g, s = divmod(b, 8)
            rows = pl.ds(s, TT, stride=8)
            per_b.append(jnp.concatenate([scr[g, i, rows, :] for i in range(NL)], axis=1))
        return jnp.concatenate(per_b, axis=0)

    y = _dot(gather(xr_scr), cre_ref[...]) - _dot(gather(xi_scr), cim_ref[...]) + d_ref[...] * u
    z = jax.nn.gelu(y)
    out = z * jax.nn.sigmoid(_dot(z, wglu_ref[...]) + bglu_ref[...])
    out = out * _silu(gate_ref[...].reshape(B * TT, SSM_WIDTH))
    o_ref[...] = out.reshape(B, TT, SSM_WIDTH).astype(o_ref.dtype)


def _s5(p, x0r, x0i, bcat, cre, cim, tab, d_skip, w_glu, b_glu, *, TT, t_valid):
    B, T, _ = p["c_u"].shape
    NBG = x0r.shape[0]
    NL = SSM_FLAT // LANES
    last = t_valid - 1
    row = pl.BlockSpec((B, TT, SSM_WIDTH), lambda j: (0, j, 0))
    st = pl.BlockSpec((NBG, 8, SSM_FLAT), lambda j: (0, 0, 0))
    full = lambda a: pl.BlockSpec(a.shape, lambda j: (0,) * a.ndim)
    d2 = d_skip.reshape(1, SSM_WIDTH)
    bg2 = b_glu.reshape(1, SSM_WIDTH)
    state = jax.ShapeDtypeStruct((NBG, 8, SSM_FLAT), F32)
    return pl.pallas_call(
        functools.partial(_s5_kernel, TT=TT, B=B, last_tile=last // TT, last_t=last % TT),
        grid=(T // TT,),
        in_specs=[row, row, st, st, full(bcat), full(cre), full(cim), full(tab), full(d2),
                  full(w_glu), full(bg2)],
        out_specs=[row, st, st],
        out_shape=[jax.ShapeDtypeStruct((B, T, SSM_WIDTH), MIX_DTYPE), state, state],
        scratch_shapes=[pltpu.VMEM((NBG, NL, TT * 8, LANES), F32)] * 2
        + [pltpu.VMEM((NBG, NL, 8, LANES), F32)] * 2,
        compiler_params=_params("arbitrary"),
        name="s5_mix",
    )(p["c_u"], p["c_gate"], x0r, x0i, bcat, cre, cim, tab, d2, w_glu, bg2)


def _out_kernel(oa_ref, ob_ref, oc_ref, x_ref, gate_ref, g_ref, w_ref, o_ref):
    a1 = GLA_WIDTH
    a2 = GLA_WIDTH + SWA_WIDTH
    y = (jnp.dot(oa_ref[0].astype(BF16), w_ref[0:a1, :], preferred_element_type=F32)
         + jnp.dot(ob_ref[0].astype(BF16), w_ref[a1:a2, :], preferred_element_type=F32)
         + jnp.dot(oc_ref[0].astype(BF16), w_ref[a2:, :], preferred_element_type=F32))
    ms = jnp.mean(y * y, axis=-1, keepdims=True)
    yn = y * lax.rsqrt(ms + NORM_EPS) * g_ref[...]
    o_ref[0] = x_ref[0] + gate_ref[0] * yn


def _out_proj(oa, ob, oc, x, gate, g_post, w_out, *, tm):
    nb, T, _ = x.shape
    R = gate.shape[1]
    row = lambda w: pl.BlockSpec((1, tm, w), lambda b, j: (b, j, 0))
    if R == 1:
        gate_spec = pl.BlockSpec((1, 1, D_MODEL), lambda b, j: (b, 0, 0))
    else:
        gate_spec = row(D_MODEL)
    return pl.pallas_call(
        _out_kernel,
        grid=(nb, T // tm),
        in_specs=[row(GLA_WIDTH), row(SWA_WIDTH), row(SSM_WIDTH), row(D_MODEL), gate_spec,
                  pl.BlockSpec((1, D_MODEL), lambda b, j: (0, 0)),
                  pl.BlockSpec((D_MODEL, D_MODEL), lambda b, j: (0, 0))],
        out_specs=row(D_MODEL),
        out_shape=jax.ShapeDtypeStruct((nb, T, D_MODEL), F32),
        compiler_params=_params("arbitrary", "arbitrary"),
        name="out_proj",
    )(oa, ob, oc, x, gate, g_post.reshape(1, D_MODEL), w_out)


def _rope_tables(pos):
    half = ROPE_DIMS // 2
    inv = ROPE_THETA ** (-jnp.arange(half, dtype=F32) * (2.0 / ROPE_DIMS))
    ang = pos.astype(F32)[:, None] * inv[None, :]
    cos, sin = jnp.cos(ang), jnp.sin(ang)
    n = pos.shape[0]
    rest = SWA_HD - ROPE_DIMS
    head = lambda a, b, fill: jnp.concatenate([a, b, jnp.full((n, rest), fill, F32)], axis=1)
    zero = jnp.zeros((n, half), F32)
    reps = LANES // SWA_HD
    return (jnp.tile(head(cos, cos, 1.0), (1, reps)),
            jnp.tile(head(-sin, zero, 0.0), (1, reps)),
            jnp.tile(head(zero, sin, 0.0), (1, reps)))


def _rearrange_w_in(w_in):
    GK = GLA_HEADS * GLA_DK
    o = np.cumsum([0, GK, GK, GLA_WIDTH, GLA_RANK, GLA_WIDTH, SWA_WIDTH, SWA_WIDTH, SWA_WIDTH,
                   SWA_WIDTH, SSM_WIDTH, SSM_WIDTH])
    piece = lambda i: w_in[:, :, o[i]:o[i + 1]]
    lr = jnp.pad(piece(3), ((0, 0), (0, 0), (0, LANES - GLA_RANK)))
    cat = jnp.concatenate([piece(0), piece(1), piece(2), piece(4), piece(5), piece(6), piece(7),
                           piece(8), piece(9), piece(10), lr], axis=-1)
    return cat.astype(BF16)


def kernel(x_prompt, x_sample, c_prompt, c_sample, state_gla, cache_swa_k, cache_swa_v, state_ssm_re, state_ssm_im, w_ada, b_ada, g_pre, g_post, w_in, w_gla_lr, b_gla_lr, g_gla, ssm_lambda_re, ssm_lambda_im, ssm_log_dt, ssm_b_re, ssm_b_im, ssm_c_re, ssm_c_im, ssm_d, w_glu, b_glu, w_out):
    Bp, Tp, _ = x_prompt.shape
    Bs, Ts, _ = x_sample.shape
    PAD = SAMPLE_PAD
    Lc = cache_swa_k.shape[2]

    w_cat = _rearrange_w_in(w_in)
    w_out_b = w_out.astype(BF16)
    w_glu_b = w_glu.astype(BF16)
    w_lr_pad = jnp.pad(w_gla_lr, ((0, 0), (0, LANES - GLA_RANK), (0, 0)))
    eye = jnp.eye(SSM_GROUPS, dtype=F32)

    def block_diag_c(c):
        ct = jnp.transpose(c, (0, 1, 3, 2))
        return (ct[:, :, :, None, :] * eye[None, :, None, :, None]).reshape(DEPTH, SSM_FLAT, SSM_WIDTH)

    cre = block_diag_c(ssm_c_re).astype(BF16)
    cim = block_diag_c(ssm_c_im).astype(BF16)
    bcat, tab = _s5_prep(ssm_lambda_re, ssm_lambda_im, ssm_log_dt, ssm_b_re, ssm_b_im)

    n_c = Bp + Bs
    c_rows = -(-n_c // 8) * 8
    c_all = jnp.pad(jnp.concatenate([c_prompt, c_sample], axis=0), ((0, c_rows - n_c), (0, 0)))
    mod = _ada_mod(c_all, w_ada, b_ada)

    tabs_p = _rope_tables(jnp.arange(Tp))
    tabs_s = _rope_tables(PAST_LEN + jnp.arange(Bs * PAD) % PAD)

    xp = x_prompt
    xs = jnp.pad(x_sample, ((0, 0), (0, PAD - Ts), (0, 0))).reshape(1, Bs * PAD, D_MODEL)
    seq_last = lambda a: jnp.transpose(a, (0, 1, 3, 4, 2))
    seq_first = lambda a: jnp.transpose(a, (0, 1, 4, 2, 3))
    cache_k = seq_last(cache_swa_k)
    cache_v = seq_last(cache_swa_v)
    zero_gla = jnp.zeros((Bp, GLA_HEADS, GLA_DK, GLA_DV), F32)
    groups = lambda n: -(-n // 8)
    ssm_in = lambda a: jnp.pad(a.reshape(a.shape[0], SSM_FLAT),
                               ((0, groups(a.shape[0]) * 8 - a.shape[0]), (0, 0))
                               ).reshape(groups(a.shape[0]), 8, SSM_FLAT)
    zero_ssm = jnp.zeros((groups(Bp), 8, SSM_FLAT), F32)

    acc = [[] for _ in range(6)]
    tails = None
    shifted = None
    for l in range(DEPTH):
        def chunk(rows, k):
            return mod[l, rows, k * D_MODEL:(k + 1) * D_MODEL]

        pr = slice(0, Bp)
        shift, scale, gate = (chunk(pr, k).reshape(Bp, 1, D_MODEL) for k in range(3))
        p = _proj(xp, scale, shift, g_pre[l], w_cat[l], tabs_p, tm=512, tail=(l, tails))
        tails = (p["k_tail"], p["v_tail"])
        oa, gla_p = _gla(p, w_lr_pad[l], b_gla_lr[l], g_gla[l], zero_gla, TT=1024, C=GLA_CHUNK, t_valid=Tp)
        ob = _swa_prompt(p)
        oc, re_p, im_p = _s5(p, zero_ssm, zero_ssm, bcat[l], cre[l], cim[l], tab[l], ssm_d[l],
                             w_glu_b[l], b_glu[l], TT=128, t_valid=Tp)
        xp = _out_proj(oa, ob, oc, xp, gate, g_post[l], w_out_b[l], tm=512)

        sr = slice(Bp, Bp + Bs)
        shift, scale, gate = (jnp.repeat(chunk(sr, k), PAD, axis=0).reshape(1, Bs * PAD, D_MODEL)
                              for k in range(3))
        q = _proj(xs, scale, shift, g_pre[l], w_cat[l], tabs_s, tm=Bs * PAD)
        q = {n: a.reshape(Bs, PAD, a.shape[-1]) for n, a in q.items()}
        oa, gla_s = _gla(q, w_lr_pad[l], b_gla_lr[l], g_gla[l], state_gla[l], TT=PAD, C=PAD, t_valid=Ts)
        ob, k_s, v_s = _swa_sample(q, cache_k, cache_v, l, Ts, shifted)
        shifted = (k_s, v_s)
        oc, re_s, im_s = _s5(q, ssm_in(state_ssm_re[l]), ssm_in(state_ssm_im[l]), bcat[l], cre[l],
                             cim[l], tab[l], ssm_d[l], w_glu_b[l], b_glu[l], TT=PAD, t_valid=Ts)
        flat = lambda a: a.reshape(1, Bs * PAD, a.shape[-1])
        xs = _out_proj(flat(oa), flat(ob), flat(oc), xs, gate, g_post[l], w_out_b[l], tm=Bs * PAD)

        for i, a in enumerate((gla_p, gla_s, re_p, im_p, re_s, im_s)):
            acc[i].append(a)

    st = [jnp.stack(a) for a in acc]
    ssm = lambda a, n: a.reshape(DEPTH, -1, SSM_GROUPS, SSM_STATE)[:, :n]
    y_sample = xs.reshape(Bs, PAD, D_MODEL)[:, :Ts]
    return (xp, y_sample, st[0], st[1], seq_first(tails[0]), seq_first(tails[1]),
            seq_first(shifted[0]), seq_first(shifted[1]),
            ssm(st[2], Bp), ssm(st[3], Bp), ssm(st[4], Bs), ssm(st[5], Bs))
```

```python
import functools

import numpy as np
import jax
import jax.numpy as jnp
from jax import lax
from jax.experimental import pallas as pl
from jax.experimental.pallas import tpu as pltpu

F32 = jnp.float32
BF16 = jnp.bfloat16
HIGHEST = lax.Precision.HIGHEST

D_MODEL = 1024
DEPTH = 4
PAST_LEN = 8192
GLA_HEADS = 4
GLA_DK = 32
GLA_DV = 64
GLA_WIDTH = GLA_HEADS * GLA_DV
GLA_RANK = 16
GLA_TAU = 16.0
GLA_CHUNK = 64
SWA_WIDTH = 512
SWA_HEADS = 8
SWA_HD = 64
DILATED_PATTERNS = ((128, 1), (512, 4), (2048, 16))
SWA_WMAX = 2048
SWA_QBLOCK = 128
ROPE_THETA = 500000.0
ROPE_DIMS = 16
SSM_WIDTH = 256
SSM_GROUP = 16
SSM_GROUPS = 16
SSM_STATE = 64
SSM_FLAT = SSM_GROUPS * SSM_STATE
NORM_EPS = 1e-6
NEG_BIG = -1e30

SAMPLE_PAD = 8
LANES = 128
VMEM_LIMIT = 48 * 1024 * 1024
MIX_DTYPE = BF16

W_COLS = (("a_qk", 256), ("a_v", 256), ("a_gate", 256), ("b_q", 512), ("b_k", 512),
          ("b_v", 512), ("b_gate", 512), ("c_u", 256), ("c_gate", 256), ("a_lr", 128))
W_OFF = {}
_o = 0
for _n, _w in W_COLS:
    W_OFF[_n] = (_o, _o + _w)
    _o += _w
W_TOTAL = _o


def _params(*sem):
    return pltpu.CompilerParams(dimension_semantics=sem, vmem_limit_bytes=VMEM_LIMIT)


def _dot(a, b):
    return jnp.dot(a.astype(BF16), b.astype(BF16), preferred_element_type=F32)


def _dot_nt(a, b):
    return lax.dot_general(a.astype(BF16), b.astype(BF16), (((1,), (1,)), ((), ())),
                           preferred_element_type=F32)


def _dot_tn(a, b):
    return lax.dot_general(a.astype(BF16), b.astype(BF16), (((0,), (0,)), ((), ())),
                           preferred_element_type=F32)


def _dot_f32(a, b):
    return jnp.dot(a, b, precision=HIGHEST, preferred_element_type=F32)


def _silu(x):
    return x * jax.nn.sigmoid(x)


def _mod_kernel(c_ref, w_ref, b_ref, o_ref):
    o_ref[0] = _dot_f32(_silu(c_ref[...]), w_ref[0]) + b_ref[0]


def _ada_mod(c_all, w_ada, b_ada):
    rows = c_all.shape[0]
    nt = 3 * D_MODEL // 1024
    return pl.pallas_call(
        _mod_kernel,
        grid=(DEPTH, nt),
        in_specs=[pl.BlockSpec((rows, D_MODEL), lambda l, n: (0, 0)),
                  pl.BlockSpec((1, D_MODEL, 1024), lambda l, n: (l, 0, n)),
                  pl.BlockSpec((1, 1, 1024), lambda l, n: (l, 0, n))],
        out_specs=pl.BlockSpec((1, rows, 1024), lambda l, n: (l, 0, n)),
        out_shape=jax.ShapeDtypeStruct((DEPTH, rows, 3 * D_MODEL), F32),
        compiler_params=_params("arbitrary", "arbitrary"),
        name="ada_mod",
    )(c_all, w_ada, b_ada.reshape(DEPTH, 1, 3 * D_MODEL))


def _rope(x, cos, sa, sb):
    outs = []
    for c in range(SWA_WIDTH // LANES):
        xc = x[:, LANES * c:LANES * (c + 1)]
        outs.append(xc * cos + pltpu.roll(xc, LANES - ROPE_DIMS // 2, 1) * sa
                    + pltpu.roll(xc, ROPE_DIMS // 2, 1) * sb)
    return jnp.concatenate(outs, axis=1)


def _proj_kernel(x_ref, sc_ref, sh_ref, g_ref, w_ref, cos_ref, sa_ref, sb_ref, *rest, emit_tail):
    n_out = len(W_COLS) + (2 if emit_tail else 0)
    outs = rest[len(rest) - n_out:]
    x = x_ref[0]
    ms = jnp.mean(x * x, axis=-1, keepdims=True)
    h = x * lax.rsqrt(ms + NORM_EPS) * g_ref[...]
    h = h * (1.0 + sc_ref[0]) + sh_ref[0]
    hb = h.astype(BF16)

    def mm(name):
        c0, c1 = W_OFF[name]
        return jnp.dot(hb, w_ref[:, c0:c1], preferred_element_type=F32)

    names = [n for n, _ in W_COLS]
    cos, sa, sb = cos_ref[...], sa_ref[...], sb_ref[...]
    for idx, name in enumerate(names):
        val = mm(name)
        if name in ("b_q", "b_k"):
            val = _rope(val, cos, sa, sb)
        outs[idx][0] = val
        if emit_tail and name == "b_k":
            outs[len(names)][0, 0] = val.T.reshape(SWA_HEADS, SWA_HD, val.shape[0])
        if emit_tail and name == "b_v":
            outs[len(names) + 1][0, 0] = val.T.reshape(SWA_HEADS, SWA_HD, val.shape[0])


def _proj(x, scale, shift, g_pre, w_cat, rope_tabs, *, tm, tail=None):
    nb, T, _ = x.shape
    nt = T // tm
    R = scale.shape[1]
    emit_tail = tail is not None
    if R == 1:
        mod_spec = pl.BlockSpec((1, 1, D_MODEL), lambda b, j: (b, 0, 0))
    else:
        mod_spec = pl.BlockSpec((1, tm, D_MODEL), lambda b, j: (b, j, 0))
    tab_spec = pl.BlockSpec((tm, LANES), lambda b, j: (j, 0))
    out_specs = [pl.BlockSpec((1, tm, w), lambda b, j: (b, j, 0)) for _, w in W_COLS]
    out_shape = [jax.ShapeDtypeStruct((nb, T, w), F32) for _, w in W_COLS]
    in_specs = [pl.BlockSpec((1, tm, D_MODEL), lambda b, j: (b, j, 0)),
                mod_spec, mod_spec,
                pl.BlockSpec((1, D_MODEL), lambda b, j: (0, 0)),
                pl.BlockSpec((D_MODEL, W_TOTAL), lambda b, j: (0, 0)),
                tab_spec, tab_spec, tab_spec]
    args = [x, scale, shift, g_pre.reshape(1, D_MODEL), w_cat, *rope_tabs]
    aliases = {}
    if emit_tail:
        l, bufs = tail
        keep = min(SWA_WMAX, T)
        first = (T - keep) // tm
        tail_spec = pl.BlockSpec((1, 1, SWA_HEADS, SWA_HD, tm),
                                 lambda b, j: (l, b, 0, 0, jnp.maximum(j - first, 0)))
        out_specs += [tail_spec, tail_spec]
        out_shape += [jax.ShapeDtypeStruct((DEPTH, nb, SWA_HEADS, SWA_HD, keep), F32)] * 2
        if bufs is not None:
            in_specs += [pl.BlockSpec(memory_space=pl.ANY)] * 2
            aliases = {len(args): len(W_COLS), len(args) + 1: len(W_COLS) + 1}
            args += list(bufs)
    res = pl.pallas_call(
        functools.partial(_proj_kernel, emit_tail=emit_tail),
        grid=(nb, nt),
        in_specs=in_specs,
        out_specs=out_specs,
        out_shape=out_shape,
        input_output_aliases=aliases,
        compiler_params=_params("arbitrary", "arbitrary"),
        name="in_proj",
    )(*args)
    names = [n for n, _ in W_COLS]
    d = dict(zip(names, res[:len(names)]))
    if emit_tail:
        d["k_tail"], d["v_tail"] = res[len(names)], res[len(names) + 1]
    return d


def _cumsum_rows(x, C):
    n = x.shape[0]
    blk = min(n, max(C, LANES))
    ri = lax.broadcasted_iota(jnp.int32, (blk, blk), 0)
    ci = lax.broadcasted_iota(jnp.int32, (blk, blk), 1)
    tri = jnp.where((ri >= ci) & (ri // C == ci // C), 1.0, 0.0).astype(BF16)
    hi = x.astype(BF16)
    r1 = x - hi.astype(F32)
    mid = r1.astype(BF16)
    lo = (r1 - mid.astype(F32)).astype(BF16)
    out = []
    for b in range(n // blk):
        rs = slice(b * blk, (b + 1) * blk)
        dot = lambda t: jnp.dot(tri, t[rs], preferred_element_type=F32)
        out.append(dot(hi) + dot(mid) + dot(lo))
    return jnp.concatenate(out, axis=0)


def _gla_kernel(qk_ref, v_ref, gate_ref, lr_ref, wlr_ref, blr_ref, gg_ref, s0_ref,
                o_ref, sf_ref, s_scr, o_scr, *, TT, C, t_valid, T, own_state):
    j = pl.program_id(1)
    H, DK, DV = GLA_HEADS, GLA_DK, GLA_DV
    HK, HV = H * DK, H * DV
    i0 = lambda shape: lax.broadcasted_iota(jnp.int32, shape, 0)
    i1 = lambda shape: lax.broadcasted_iota(jnp.int32, shape, 1)
    own_cols = i0((H * C, HK)) // C == i1((H * C, HK)) // DK
    causal = i0((H * C, C)) % C >= i1((H * C, C))
    diag = i0((HK, HV)) // DK == i1((HK, HV)) // DV
    lane_head = i1((C, HV)) // DV

    def block_diag(s):
        return jnp.where(diag, jnp.concatenate([s.reshape(HK, DV)] * H, axis=1), 0.0)

    if not own_state:
        @pl.when(j == 0)
        def _():
            s_scr[...] = block_diag(s0_ref[0])

    z = _dot_f32(lr_ref[0], wlr_ref[...]) + blr_ref[...]
    glog = (jnp.minimum(z, 0.0) - jnp.log1p(jnp.exp(-jnp.abs(z)))) * (1.0 / GLA_TAU)
    qk = qk_ref[0]
    q = qk[:, :HK]
    k = qk[:, HK:]
    if t_valid < (C if own_state else T):
        row = j * TT + lax.broadcasted_iota(jnp.int32, (TT, HK), 0)
        row = row % C if own_state else row
        glog = jnp.where(row < t_valid, glog, 0.0)
        k = jnp.where(row < t_valid, k, 0.0)
    v = v_ref[0]
    nc = TT // C
    bc = _cumsum_rows(glog, C)
    bl = jnp.concatenate([jnp.broadcast_to(bc[(c + 1) * C - 1:(c + 1) * C], (C, HK))
                          for c in range(nc)], axis=0)
    qd = q * (DK ** -0.5) * jnp.exp(bc)
    kd = k * jnp.exp(-bc)
    kdl = k * jnp.exp(bl - bc)
    dec = jnp.exp(bl)
    def take_heads(s_bd, dst):
        for h in range(H):
            dst[h] = s_bd[h * DK:(h + 1) * DK, h * DV:(h + 1) * DV]

    s_prev = None if own_state else s_scr[...]
    for c in range(nc):
        rows = slice(c * C, (c + 1) * C)
        qc, kc, klc, vc = qd[rows], kd[rows], kdl[rows], v[rows]
        if own_state:
            s_prev = block_diag(s0_ref[c])
        qz = jnp.where(own_cols, jnp.concatenate([qc] * H, axis=0), 0.0)
        att = jnp.where(causal, _dot_nt(qz, kc), 0.0)
        res = _dot(att, vc)
        intra = res[0:C]
        for h in range(1, H):
            intra = jnp.where(lane_head == h, res[h * C:(h + 1) * C], intra)
        o_scr[rows, :] = intra + _dot(qc, s_prev)
        dec_col = jnp.broadcast_to(dec[c * C:c * C + 1], (8, HK)).T[:, 0:1]
        s_prev = dec_col * s_prev + jnp.where(diag, _dot_tn(klc, vc), 0.0)
        if own_state:
            take_heads(s_prev, sf_ref.at[c])
    if not own_state:
        s_scr[...] = s_prev
        take_heads(s_prev, sf_ref.at[0])

    o = o_scr[...]
    sq = o * o
    hi = sq.astype(BF16)
    r1 = sq - hi.astype(F32)
    mid = r1.astype(BF16)
    lo = (r1 - mid.astype(F32)).astype(BF16)
    same_head = jnp.where(i0((HV, HV)) // DV == i1((HV, HV)) // DV, 1.0, 0.0).astype(BF16)
    hsum = lambda t: jnp.dot(t, same_head, preferred_element_type=F32)
    ms = (hsum(hi) + hsum(mid) + hsum(lo)) * (1.0 / DV)
    o_ref[0] = (o * lax.rsqrt(ms + NORM_EPS) * gg_ref[...] * _silu(gate_ref[0])).astype(o_ref.dtype)


def _gla(p, w_lr, b_lr, g_gla, s0, *, TT, C, t_valid, own_state=False):
    B, T, _ = p["a_qk"].shape
    HK = GLA_HEADS * GLA_DK
    row = lambda w: pl.BlockSpec((1, TT, w), lambda b, j: (b, j, 0))
    full = lambda r, c: pl.BlockSpec((r, c), lambda b, j: (0, 0))
    if own_state:
        st_spec = pl.BlockSpec((TT // C, GLA_HEADS, GLA_DK, GLA_DV), lambda b, j: (j, 0, 0, 0))
    else:
        st_spec = pl.BlockSpec((1, GLA_HEADS, GLA_DK, GLA_DV), lambda b, j: (b, 0, 0, 0))
    return pl.pallas_call(
        functools.partial(_gla_kernel, TT=TT, C=C, t_valid=t_valid, T=T, own_state=own_state),
        grid=(B, T // TT),
        in_specs=[row(2 * HK), row(GLA_WIDTH), row(GLA_WIDTH), row(LANES),
                  full(LANES, HK), full(1, HK), full(1, GLA_WIDTH), st_spec],
        out_specs=[row(GLA_WIDTH), st_spec],
        out_shape=[jax.ShapeDtypeStruct((B, T, GLA_WIDTH), MIX_DTYPE),
                   jax.ShapeDtypeStruct(s0.shape, F32)],
        scratch_shapes=[pltpu.VMEM((HK, GLA_WIDTH), F32), pltpu.VMEM((TT, GLA_WIDTH), F32)],
        compiler_params=_params("arbitrary", "arbitrary"),
        name="gla_mix",
    )(p["a_qk"], p["a_v"], p["a_gate"], p["a_lr"], w_lr, b_lr.reshape(1, HK),
      g_gla.reshape(1, GLA_WIDTH), s0)


def _swa_kernel(q_ref, k_ref, v_ref, g_ref, o_ref, qd_scr, kd_scr, vd_scr, op_scr, lse_scr,
                stage_scr, *, T):
    QB = SWA_QBLOCK
    CP = 256
    (_, d0), (_, d1), (_, d2) = DILATED_PATTERNS
    assert d0 == 1 and d2 % d1 == 0
    dd = d2 // d1

    def regroup(src_ref, dst_scr, scale):
        cast = lambda x: (x * scale).astype(BF16)
        dst = lambda j: pl.ds(pl.multiple_of(j * CP, CP), CP)

        def copy0(j, carry):
            dst_scr[0, dst(j), :] = cast(src_ref[0, dst(j), :])
            return carry

        def copy1(j, carry):
            per = T // d1 // CP
            x = src_ref[0, pl.ds(j // per + d1 * CP * (j % per), CP, stride=d1), :]
            stage_scr[dst(j), :] = x
            dst_scr[1, dst(j), :] = cast(x)
            return carry

        def copy2(j, carry):
            per = T // d2 // CP
            r = j // per
            start = (r % d1) * (T // d1) + r // d1 + dd * CP * (j % per)
            dst_scr[2, dst(j), :] = cast(stage_scr[pl.ds(start, CP, stride=dd), :])
            return carry

        for body in (copy0, copy1, copy2):
            lax.fori_loop(0, T // CP, body, 0)

    regroup(q_ref, qd_scr, SWA_HD ** -0.5)
    regroup(k_ref, kd_scr, 1.0)
    regroup(v_ref, vd_scr, 1.0)

    ri = lax.broadcasted_iota(jnp.int32, (2 * QB, 2 * QB), 0) % QB
    ci = lax.broadcasted_iota(jnp.int32, (2 * QB, 2 * QB), 1)
    band = (ci >= ri) & (ci <= ri + QB)
    bias_band = jnp.where(band, 0.0, NEG_BIG)
    bias_first = jnp.where(band & (ci >= QB), 0.0, NEG_BIG)
    lo_q = lax.broadcasted_iota(jnp.int32, (QB, LANES), 1) < SWA_HD
    ones = jnp.ones((2 * QB, LANES), BF16)

    def step(i, carry):
        cur = pl.ds(pl.multiple_of(i * QB, QB), QB)
        prev = pl.ds(pl.multiple_of(jnp.maximum(i - 1, 0) * QB, QB), QB)
        for p, (_, d) in enumerate(DILATED_PATTERNS):
            nblk = T // d // QB
            r = i // nblk
            n = i % nblk
            q = qd_scr[p, cur, :]
            qz = jnp.concatenate([jnp.where(lo_q, q, 0), jnp.where(lo_q, 0, q)], axis=0)
            k = jnp.concatenate([kd_scr[p, prev, :], kd_scr[p, cur, :]], axis=0)
            v = jnp.concatenate([vd_scr[p, prev, :], vd_scr[p, cur, :]], axis=0)
            s = lax.dot_general(qz, k, (((1,), (1,)), ((), ())), preferred_element_type=F32)
            s = s + jnp.where(n > 0, bias_band, bias_first)
            m = jnp.max(s, axis=-1, keepdims=True)
            e = jnp.exp(s - m).astype(BF16)
            ox = jnp.dot(e, jnp.concatenate([v, ones], axis=1), preferred_element_type=F32)
            num = jnp.where(lo_q, ox[:QB, :LANES], ox[QB:, :LANES])
            den = jnp.where(lo_q, ox[:QB, LANES:], ox[QB:, LANES:])
            mx = jnp.where(lo_q, jnp.broadcast_to(m[:QB], (QB, LANES)),
                           jnp.broadcast_to(m[QB:], (QB, LANES)))
            start = r + d * QB * n
            dst = pl.ds(start, QB) if d == 1 else pl.ds(start, QB, stride=d)
            op_scr[p, dst, :] = num / den
            lse_scr[p, dst, :] = mx + jnp.log(den)
        return carry

    lax.fori_loop(0, T // QB, step, 0, unroll=4)

    CH = 512

    def combine(i, carry):
        rs = pl.ds(pl.multiple_of(i * CH, CH), CH)
        l0, l1, l2 = lse_scr[0, rs, :], lse_scr[1, rs, :], lse_scr[2, rs, :]
        mx = jnp.maximum(jnp.maximum(l0, l1), l2)
        w0, w1, w2 = jnp.exp(l0 - mx), jnp.exp(l1 - mx), jnp.exp(l2 - mx)
        mixed = (w0 * op_scr[0, rs, :] + w1 * op_scr[1, rs, :] + w2 * op_scr[2, rs, :]) / (w0 + w1 + w2)
        o_ref[0, rs, :] = (mixed * _silu(g_ref[0, rs, :])).astype(o_ref.dtype)
        return carry

    lax.fori_loop(0, T // CH, combine, 0)


def _swa_prompt(p):
    B, T, _ = p["b_q"].shape
    spec = pl.BlockSpec((1, T, LANES), lambda b, hp: (b, 0, hp))
    return pl.pallas_call(
        functools.partial(_swa_kernel, T=T),
        grid=(B, SWA_WIDTH // LANES),
        in_specs=[spec] * 4,
        out_specs=spec,
        out_shape=jax.ShapeDtypeStruct((B, T, SWA_WIDTH), MIX_DTYPE),
        scratch_shapes=[pltpu.VMEM((len(DILATED_PATTERNS), T, LANES), BF16)] * 3
        + [pltpu.VMEM((len(DILATED_PATTERNS), T, LANES), F32)] * 2
        + [pltpu.VMEM((T, LANES), F32)],
        compiler_params=_params("arbitrary", "arbitrary"),
        name="swa_prompt",
    )(p["b_q"], p["b_k"], p["b_v"], p["b_gate"])


def _swa_dec_kernel(ck_ref, cv_ref, q_ref, k_ref, v_ref, g_ref, mult_ref, multf_ref, *rest, L, Tn):
    o_ref, ko_ref, vo_ref = rest[-3:]
    HG = ck_ref.shape[2]
    lane = lax.broadcasted_iota(jnp.int32, (HG, SWA_HD, LANES), 2)

    def shifted(c_ref, new_rows):
        r = pltpu.roll(c_ref[0, 0], L - Tn, 2)
        moved = pltpu.roll(new_rows, SAMPLE_PAD - Tn, 0)
        pad = jnp.concatenate([jnp.zeros((LANES - SAMPLE_PAD, HG * SWA_HD), F32), moved], axis=0)
        tile = pad.T.reshape(HG, SWA_HD, LANES)
        last = jnp.where(lane >= LANES - Tn, tile, r[:, :, L - LANES:])
        return jnp.concatenate([r[:, :, :L - LANES], last], axis=2)

    kout = shifted(ck_ref, k_ref[0])
    vout = shifted(cv_ref, v_ref[0])
    ko_ref[0, 0] = kout
    vo_ref[0, 0] = vout

    q = q_ref[0] * (SWA_HD ** -0.5)
    q3 = jnp.stack([q[:, h * SWA_HD:(h + 1) * SWA_HD] for h in range(HG)], axis=0).astype(BF16)
    qk = lambda kk: jnp.einsum('hid,hdr->hir', q3, kk.astype(BF16), preferred_element_type=F32)
    pv = lambda pp, vv: jnp.einsum('hir,hdr->hid', pp.astype(BF16), vv.astype(BF16),
                                   preferred_element_type=F32)
    mult = mult_ref[...][None]
    multf = multf_ref[...][None]
    s = jnp.where(mult > 0, qk(kout), NEG_BIG)
    sf = jnp.where(multf > 0, qk(ck_ref[0, 0, :, :, 0:LANES]), NEG_BIG)
    m = jnp.maximum(jnp.max(s, axis=-1, keepdims=True), jnp.max(sf, axis=-1, keepdims=True))
    pe = jnp.exp(s - m) * mult
    pf = jnp.exp(sf - m) * multf
    den = jnp.sum(pe, axis=-1, keepdims=True) + jnp.sum(pf, axis=-1, keepdims=True)
    den = jnp.where(den > 0, den, 1.0)
    o3 = (pv(pe, vout) + pv(pf, cv_ref[0, 0, :, :, 0:LANES])) / den
    o = jnp.concatenate([o3[h] for h in range(HG)], axis=1)
    o_ref[0] = (o * _silu(g_ref[0])).astype(o_ref.dtype)


def _mult_tables(L, Tn):
    assert L == SWA_WMAX
    qpos = SWA_WMAX + np.arange(SAMPLE_PAD)[:, None]

    def count(kpos):
        off = qpos - kpos[None, :]
        cnt = np.zeros(off.shape, np.float32)
        for w, d in DILATED_PATTERNS:
            cnt += ((off >= 0) & (off <= w) & (off % d == 0)).astype(np.float32)
        cnt[Tn:] = 0.0
        return cnt

    mult = count(np.arange(L) + Tn)
    multf = count(np.arange(LANES))
    multf[:, Tn:] = 0.0
    return jnp.asarray(mult), jnp.asarray(multf)


def _swa_sample(p, cache_k, cache_v, l, Tn, bufs):
    B = p["b_q"].shape[0]
    L = cache_k.shape[-1]
    HG = 8
    mult, multf = _mult_tables(L, Tn)
    cspec = pl.BlockSpec((1, 1, HG, SWA_HD, L), lambda b, g: (l, b, g, 0, 0))
    nspec = pl.BlockSpec((1, SAMPLE_PAD, HG * SWA_HD), lambda b, g: (b, 0, g))
    const = lambda a: pl.BlockSpec(a.shape, lambda b, g: (0, 0))
    stacked = jax.ShapeDtypeStruct(cache_k.shape, F32)
    in_specs = [cspec, cspec, nspec, nspec, nspec, nspec, const(mult), const(multf)]
    args = [cache_k, cache_v, p["b_q"], p["b_k"], p["b_v"], p["b_gate"], mult, multf]
    aliases = {}
    if bufs is not None:
        in_specs += [pl.BlockSpec(memory_space=pl.ANY)] * 2
        aliases = {len(args): 1, len(args) + 1: 2}
        args += list(bufs)
    return pl.pallas_call(
        functools.partial(_swa_dec_kernel, L=L, Tn=Tn),
        grid=(B, SWA_HEADS // HG),
        in_specs=in_specs,
        out_specs=[nspec, cspec, cspec],
        out_shape=[jax.ShapeDtypeStruct((B, SAMPLE_PAD, SWA_WIDTH), MIX_DTYPE), stacked, stacked],
        input_output_aliases=aliases,
        compiler_params=_params("arbitrary", "arbitrary"),
        name="swa_sample",
    )(*args)


def _s5prep_kernel(lr_ref, li_ref, ldt_ref, bre_ref, bim_ref, bcat_ref, tab_ref):
    lr = lr_ref[0]
    li = li_ref[0]
    dt = jnp.exp(ldt_ref[0])
    mag = jnp.exp(lr * dt)
    br = mag * jnp.cos(li * dt)
    bi = mag * jnp.sin(li * dt)
    den = lr * lr + li * li
    nr = br - 1.0
    cr = (nr * lr + bi * li) / den
    ci = (bi * lr - nr * li) / den
    bre = bre_ref[0]
    bim = bim_ref[0]
    bcat_ref[0, :, 0:SSM_FLAT] = (cr * bre - ci * bim).astype(BF16)
    bcat_ref[0, :, SSM_FLAT:2 * SSM_FLAT] = (cr * bim + ci * bre).astype(BF16)
    tab_ref[0, 0] = jnp.broadcast_to(br, (8, SSM_FLAT))
    tab_ref[0, 1] = jnp.broadcast_to(bi, (8, SSM_FLAT))


def _s5_prep(lam_re, lam_im, log_dt, b_re, b_im):
    eye = jnp.eye(SSM_GROUPS, dtype=F32)

    def block_diag_b(b):
        bt = jnp.transpose(b, (0, 1, 3, 2))
        return (bt[:, :, :, None, :] * eye[None, :, None, :, None]).reshape(DEPTH, SSM_WIDTH, SSM_FLAT)

    flat = lambda a: a.reshape(DEPTH, 1, SSM_FLAT)
    ldt = jnp.broadcast_to(log_dt[:, :, None], (DEPTH, SSM_GROUPS, SSM_STATE))
    vec = pl.BlockSpec((1, 1, SSM_FLAT), lambda l: (l, 0, 0))
    mat = pl.BlockSpec((1, SSM_WIDTH, SSM_FLAT), lambda l: (l, 0, 0))
    return pl.pallas_call(
        _s5prep_kernel,
        grid=(DEPTH,),
        in_specs=[vec, vec, vec, mat, mat],
        out_specs=[pl.BlockSpec((1, SSM_WIDTH, 2 * SSM_FLAT), lambda l: (l, 0, 0)),
                   pl.BlockSpec((1, 2, 8, SSM_FLAT), lambda l: (l, 0, 0, 0))],
        out_shape=[jax.ShapeDtypeStruct((DEPTH, SSM_WIDTH, 2 * SSM_FLAT), BF16),
                   jax.ShapeDtypeStruct((DEPTH, 2, 8, SSM_FLAT), F32)],
        compiler_params=_params("arbitrary"),
        name="s5_prep",
    )(flat(lam_re), flat(lam_im), flat(ldt), block_diag_b(b_re), block_diag_b(b_im))


def _s5_kernel(u_ref, gate_ref, x0r_ref, x0i_ref, bcat_ref, cre_ref, cim_ref, tab_ref, d_ref,
               wglu_ref, bglu_ref, o_ref, sr_ref, si_ref, xr_scr, xi_scr, cr_scr, ci_scr,
               *, TT, B, last_tile, last_t):
    j = pl.program_id(0)
    NBG, NL = xr_scr.shape[0], xr_scr.shape[1]
    split = lambda a: jnp.stack([a[:, LANES * i:LANES * (i + 1)] for i in range(NL)], axis=0)
    merge = lambda a: jnp.concatenate([a[i] for i in range(NL)], axis=1)

    @pl.when(j == 0)
    def _():
        xr_scr[...] = jnp.zeros(xr_scr.shape, F32)
        xi_scr[...] = jnp.zeros(xi_scr.shape, F32)
        for g in range(NBG):
            cr_scr[g] = split(x0r_ref[g])
            ci_scr[g] = split(x0i_ref[g])

    u = u_ref[...].reshape(B * TT, SSM_WIDTH)
    bu = _dot(u, bcat_ref[...])
    for b in range(B):
        g, s = divmod(b, 8)
        rows = pl.ds(s, TT, stride=8)
        for i in range(NL):
            xr_scr[g, i, rows, :] = bu[b * TT:(b + 1) * TT, LANES * i:LANES * (i + 1)]
            xi_scr[g, i, rows, :] = bu[b * TT:(b + 1) * TT, SSM_FLAT + LANES * i:SSM_FLAT + LANES * (i + 1)]

    lam_r = split(tab_ref[0])
    lam_i = split(tab_ref[1])

    def step(t, carry):
        rows = pl.ds(pl.multiple_of(t * 8, 8), 8)
        out = []
        for g in range(NBG):
            cr, ci = carry[2 * g], carry[2 * g + 1]
            nr = lam_r * cr - lam_i * ci + xr_scr[g, :, rows, :]
            ni = lam_r * ci + lam_i * cr + xi_scr[g, :, rows, :]
            xr_scr[g, :, rows, :] = nr
            xi_scr[g, :, rows, :] = ni
            out += [nr, ni]
        return tuple(out)

    init = tuple(ref[g] for g in range(NBG) for ref in (cr_scr, ci_scr))
    fin = lax.fori_loop(0, TT, step, init, unroll=min(TT, 8))
    for g in range(NBG):
        cr_scr[g] = fin[2 * g]
        ci_scr[g] = fin[2 * g + 1]

    @pl.when(j == last_tile)
    def _():
        rows = slice(last_t * 8, last_t * 8 + 8)
        for g in range(NBG):
            sr_ref[g] = merge(xr_scr[g, :, rows, :])
            si_ref[g] = merge(xi_scr[g, :, rows, :])

    def gather(scr):
        per_b = []
        for b in range(B):
            g, s = divmod(b, 8)
            rows = pl.ds(s, TT, stride=8)
            per_b.append(jnp.concatenate([scr[g, i, rows, :] for i in range(NL)], axis=1))
        return jnp.concatenate(per_b, axis=0)

    y = _dot(gather(xr_scr), cre_ref[...]) - _dot(gather(xi_scr), cim_ref[...]) + d_ref[...] * u
    z = jax.nn.gelu(y)
    out = z * jax.nn.sigmoid(_dot(z, wglu_ref[...]) + bglu_ref[...])
    out = out * _silu(gate_ref[...].reshape(B * TT, SSM_WIDTH))
    o_ref[...] = out.reshape(B, TT, SSM_WIDTH).astype(o_ref.dtype)


def _s5(p, x0r, x0i, bcat, cre, cim, tab, d_skip, w_glu, b_glu, *, TT, t_valid):
    B, T, _ = p["c_u"].shape
    NBG = x0r.shape[0]
    NL = SSM_FLAT // LANES
    last = t_valid - 1
    row = pl.BlockSpec((B, TT, SSM_WIDTH), lambda j: (0, j, 0))
    st = pl.BlockSpec((NBG, 8, SSM_FLAT), lambda j: (0, 0, 0))
    full = lambda a: pl.BlockSpec(a.shape, lambda j: (0,) * a.ndim)
    d2 = d_skip.reshape(1, SSM_WIDTH)
    bg2 = b_glu.reshape(1, SSM_WIDTH)
    state = jax.ShapeDtypeStruct((NBG, 8, SSM_FLAT), F32)
    return pl.pallas_call(
        functools.partial(_s5_kernel, TT=TT, B=B, last_tile=last // TT, last_t=last % TT),
        grid=(T // TT,),
        in_specs=[row, row, st, st, full(bcat), full(cre), full(cim), full(tab), full(d2),
                  full(w_glu), full(bg2)],
        out_specs=[row, st, st],
        out_shape=[jax.ShapeDtypeStruct((B, T, SSM_WIDTH), MIX_DTYPE), state, state],
        scratch_shapes=[pltpu.VMEM((NBG, NL, TT * 8, LANES), F32)] * 2
        + [pltpu.VMEM((NBG, NL, 8, LANES), F32)] * 2,
        compiler_params=_params("arbitrary"),
        name="s5_mix",
    )(p["c_u"], p["c_gate"], x0r, x0i, bcat, cre, cim, tab, d2, w_glu, bg2)


def _out_kernel(oa_ref, ob_ref, oc_ref, x_ref, gate_ref, g_ref, w_ref, o_ref):
    a1 = GLA_WIDTH
    a2 = GLA_WIDTH + SWA_WIDTH
    y = (jnp.dot(oa_ref[0].astype(BF16), w_ref[0:a1, :], preferred_element_type=F32)
         + jnp.dot(ob_ref[0].astype(BF16), w_ref[a1:a2, :], preferred_element_type=F32)
         + jnp.dot(oc_ref[0].astype(BF16), w_ref[a2:, :], preferred_element_type=F32))
    ms = jnp.mean(y * y, axis=-1, keepdims=True)
    yn = y * lax.rsqrt(ms + NORM_EPS) * g_ref[...]
    o_ref[0] = x_ref[0] + gate_ref[0] * yn


def _out_proj(oa, ob, oc, x, gate, g_post, w_out, *, tm):
    nb, T, _ = x.shape
    R = gate.shape[1]
    row = lambda w: pl.BlockSpec((1, tm, w), lambda b, j: (b, j, 0))
    if R == 1:
        gate_spec = pl.BlockSpec((1, 1, D_MODEL), lambda b, j: (b, 0, 0))
    else:
        gate_spec = row(D_MODEL)
    return pl.pallas_call(
        _out_kernel,
        grid=(nb, T // tm),
        in_specs=[row(GLA_WIDTH), row(SWA_WIDTH), row(SSM_WIDTH), row(D_MODEL), gate_spec,
                  pl.BlockSpec((1, D_MODEL), lambda b, j: (0, 0)),
                  pl.BlockSpec((D_MODEL, D_MODEL), lambda b, j: (0, 0))],
        out_specs=row(D_MODEL),
        out_shape=jax.ShapeDtypeStruct((nb, T, D_MODEL), F32),
        compiler_params=_params("arbitrary", "arbitrary"),
        name="out_proj",
    )(oa, ob, oc, x, gate, g_post.reshape(1, D_MODEL), w_out)


def _rope_tables(pos):
    half = ROPE_DIMS // 2
    inv = ROPE_THETA ** (-jnp.arange(half, dtype=F32) * (2.0 / ROPE_DIMS))
    ang = pos.astype(F32)[:, None] * inv[None, :]
    cos, sin = jnp.cos(ang), jnp.sin(ang)
    n = pos.shape[0]
    rest = SWA_HD - ROPE_DIMS
    head = lambda a, b, fill: jnp.concatenate([a, b, jnp.full((n, rest), fill, F32)], axis=1)
    zero = jnp.zeros((n, half), F32)
    reps = LANES // SWA_HD
    return (jnp.tile(head(cos, cos, 1.0), (1, reps)),
            jnp.tile(head(-sin, zero, 0.0), (1, reps)),
            jnp.tile(head(zero, sin, 0.0), (1, reps)))


def _rearrange_w_in(w_in):
    GK = GLA_HEADS * GLA_DK
    o = np.cumsum([0, GK, GK, GLA_WIDTH, GLA_RANK, GLA_WIDTH, SWA_WIDTH, SWA_WIDTH, SWA_WIDTH,
                   SWA_WIDTH, SSM_WIDTH, SSM_WIDTH])
    piece = lambda i: w_in[:, :, o[i]:o[i + 1]]
    lr = jnp.pad(piece(3), ((0, 0), (0, 0), (0, LANES - GLA_RANK)))
    cat = jnp.concatenate([piece(0), piece(1), piece(2), piece(4), piece(5), piece(6), piece(7),
                           piece(8), piece(9), piece(10), lr], axis=-1)
    return cat.astype(BF16)


def kernel(x_prompt, x_sample, c_prompt, c_sample, state_gla, cache_swa_k, cache_swa_v, state_ssm_re, state_ssm_im, w_ada, b_ada, g_pre, g_post, w_in, w_gla_lr, b_gla_lr, g_gla, ssm_lambda_re, ssm_lambda_im, ssm_log_dt, ssm_b_re, ssm_b_im, ssm_c_re, ssm_c_im, ssm_d, w_glu, b_glu, w_out):
    Bp, Tp, _ = x_prompt.shape
    Bs, Ts, _ = x_sample.shape
    PAD = SAMPLE_PAD
    Lc = cache_swa_k.shape[2]

    w_cat = _rearrange_w_in(w_in)
    w_out_b = w_out.astype(BF16)
    w_glu_b = w_glu.astype(BF16)
    w_lr_pad = jnp.pad(w_gla_lr, ((0, 0), (0, LANES - GLA_RANK), (0, 0)))
    eye = jnp.eye(SSM_GROUPS, dtype=F32)

    def block_diag_c(c):
        ct = jnp.transpose(c, (0, 1, 3, 2))
        return (ct[:, :, :, None, :] * eye[None, :, None, :, None]).reshape(DEPTH, SSM_FLAT, SSM_WIDTH)

    cre = block_diag_c(ssm_c_re).astype(BF16)
    cim = block_diag_c(ssm_c_im).astype(BF16)
    bcat, tab = _s5_prep(ssm_lambda_re, ssm_lambda_im, ssm_log_dt, ssm_b_re, ssm_b_im)

    n_c = Bp + Bs
    c_rows = -(-n_c // 8) * 8
    c_all = jnp.pad(jnp.concatenate([c_prompt, c_sample], axis=0), ((0, c_rows - n_c), (0, 0)))
    mod = _ada_mod(c_all, w_ada, b_ada)

    tabs_p = _rope_tables(jnp.arange(Tp))
    tabs_s = _rope_tables(PAST_LEN + jnp.arange(Bs * PAD) % PAD)

    xp = x_prompt
    xs = jnp.pad(x_sample, ((0, 0), (0, PAD - Ts), (0, 0))).reshape(1, Bs * PAD, D_MODEL)
    seq_last = lambda a: jnp.transpose(a, (0, 1, 3, 4, 2))
    seq_first = lambda a: jnp.transpose(a, (0, 1, 4, 2, 3))
    cache_k = seq_last(cache_swa_k)
    cache_v = seq_last(cache_swa_v)
    zero_gla = jnp.zeros((Bp, GLA_HEADS, GLA_DK, GLA_DV), F32)
    groups = lambda n: -(-n // 8)
    ssm_in = lambda a: jnp.pad(a.reshape(a.shape[0], SSM_FLAT),
                               ((0, groups(a.shape[0]) * 8 - a.shape[0]), (0, 0))
                               ).reshape(groups(a.shape[0]), 8, SSM_FLAT)
    zero_ssm = jnp.zeros((groups(Bp), 8, SSM_FLAT), F32)

    acc = [[] for _ in range(6)]
    tails = None
    shifted = None
    for l in range(DEPTH):
        def chunk(rows, k):
            return mod[l, rows, k * D_MODEL:(k + 1) * D_MODEL]

        pr = slice(0, Bp)
        shift, scale, gate = (chunk(pr, k).reshape(Bp, 1, D_MODEL) for k in range(3))
        p = _proj(xp, scale, shift, g_pre[l], w_cat[l], tabs_p, tm=512, tail=(l, tails))
        tails = (p["k_tail"], p["v_tail"])
        oa, gla_p = _gla(p, w_lr_pad[l], b_gla_lr[l], g_gla[l], zero_gla, TT=1024, C=GLA_CHUNK, t_valid=Tp)
        ob = _swa_prompt(p)
        oc, re_p, im_p = _s5(p, zero_ssm, zero_ssm, bcat[l], cre[l], cim[l], tab[l], ssm_d[l],
                             w_glu_b[l], b_glu[l], TT=128, t_valid=Tp)
        xp = _out_proj(oa, ob, oc, xp, gate, g_post[l], w_out_b[l], tm=512)

        sr = slice(Bp, Bp + Bs)
        shift, scale, gate = (jnp.repeat(chunk(sr, k), PAD, axis=0).reshape(1, Bs * PAD, D_MODEL)
                              for k in range(3))
        qf = _proj(xs, scale, shift, g_pre[l], w_cat[l], tabs_s, tm=Bs * PAD)
        q = {n: a.reshape(Bs, PAD, a.shape[-1]) for n, a in qf.items()}
        oa, gla_s = _gla(qf, w_lr_pad[l], b_gla_lr[l], g_gla[l], state_gla[l], TT=Bs * PAD, C=PAD,
                         t_valid=Ts, own_state=True)
        ob, k_s, v_s = _swa_sample(q, cache_k, cache_v, l, Ts, shifted)
        shifted = (k_s, v_s)
        oc, re_s, im_s = _s5(q, ssm_in(state_ssm_re[l]), ssm_in(state_ssm_im[l]), bcat[l], cre[l],
                             cim[l], tab[l], ssm_d[l], w_glu_b[l], b_glu[l], TT=PAD, t_valid=Ts)
        flat = lambda a: a.reshape(1, Bs * PAD, a.shape[-1])
        xs = _out_proj(flat(oa), flat(ob), flat(oc), xs, gate, g_post[l], w_out_b[l], tm=Bs * PAD)

        for i, a in enumerate((gla_p, gla_s, re_p, im_p, re_s, im_s)):
            acc[i].append(a)

    st = [jnp.stack(a) for a in acc]
    ssm = lambda a, n: a.reshape(DEPTH, -1, SSM_GROUPS, SSM_STATE)[:, :n]
    y_sample = xs.reshape(Bs, PAD, D_MODEL)[:, :Ts]
    return (xp, y_sample, st[0], st[1], seq_first(tails[0]), seq_first(tails[1]),
            seq_first(shifted[0]), seq_first(shifted[1]),
            ssm(st[2], Bp), ssm(st[3], Bp), ssm(st[4], Bs), ssm(st[5], Bs))
```

```python
import functools

import numpy as np
import jax
import jax.numpy as jnp
from jax import lax
from jax.experimental import pallas as pl
from jax.experimental.pallas import tpu as pltpu

F32 = jnp.float32
BF16 = jnp.bfloat16
HIGHEST = lax.Precision.HIGHEST

D_MODEL = 1024
DEPTH = 4
PAST_LEN = 8192
GLA_HEADS = 4
GLA_DK = 32
GLA_DV = 64
GLA_WIDTH = GLA_HEADS * GLA_DV
GLA_RANK = 16
GLA_TAU = 16.0
GLA_CHUNK = 64
SWA_WIDTH = 512
SWA_HEADS = 8
SWA_HD = 64
DILATED_PATTERNS = ((128, 1), (512, 4), (2048, 16))
SWA_WMAX = 2048
SWA_QBLOCK = 128
ROPE_THETA = 500000.0
ROPE_DIMS = 16
SSM_WIDTH = 256
SSM_GROUP = 16
SSM_GROUPS = 16
SSM_STATE = 64
SSM_FLAT = SSM_GROUPS * SSM_STATE
NORM_EPS = 1e-6
NEG_BIG = -1e30

SAMPLE_PAD = 8
LANES = 128
VMEM_LIMIT = 48 * 1024 * 1024
MIX_DTYPE = BF16

W_COLS = (("a_qk", 256), ("a_v", 256), ("a_gate", 256), ("b_q", 512), ("b_k", 512),
          ("b_v", 512), ("b_gate", 512), ("c_u", 256), ("c_gate", 256), ("a_lr", 128))
W_OFF = {}
_o = 0
for _n, _w in W_COLS:
    W_OFF[_n] = (_o, _o + _w)
    _o += _w
W_TOTAL = _o
SWA_PIECES = ("b_q", "b_k", "b_v", "b_gate")


def _params(*sem):
    return pltpu.CompilerParams(dimension_semantics=sem, vmem_limit_bytes=VMEM_LIMIT)


def _dot(a, b):
    return jnp.dot(a.astype(BF16), b.astype(BF16), preferred_element_type=F32)


def _dot_nt(a, b):
    return lax.dot_general(a.astype(BF16), b.astype(BF16), (((1,), (1,)), ((), ())),
                           preferred_element_type=F32)


def _dot_tn(a, b):
    return lax.dot_general(a.astype(BF16), b.astype(BF16), (((0,), (0,)), ((), ())),
                           preferred_element_type=F32)


def _dot_f32(a, b):
    return jnp.dot(a, b, precision=HIGHEST, preferred_element_type=F32)


def _silu(x):
    return x * jax.nn.sigmoid(x)


def _mod_kernel(c_ref, w_ref, b_ref, o_ref):
    o_ref[0] = _dot_f32(_silu(c_ref[...]), w_ref[0]) + b_ref[0]


def _ada_mod(c_all, w_ada, b_ada):
    rows = c_all.shape[0]
    nt = 3 * D_MODEL // 1024
    return pl.pallas_call(
        _mod_kernel,
        grid=(DEPTH, nt),
        in_specs=[pl.BlockSpec((rows, D_MODEL), lambda l, n: (0, 0)),
                  pl.BlockSpec((1, D_MODEL, 1024), lambda l, n: (l, 0, n)),
                  pl.BlockSpec((1, 1, 1024), lambda l, n: (l, 0, n))],
        out_specs=pl.BlockSpec((1, rows, 1024), lambda l, n: (l, 0, n)),
        out_shape=jax.ShapeDtypeStruct((DEPTH, rows, 3 * D_MODEL), F32),
        compiler_params=_params("arbitrary", "arbitrary"),
        name="ada_mod",
    )(c_all, w_ada, b_ada.reshape(DEPTH, 1, 3 * D_MODEL))


def _rope(x, cos, sa, sb):
    outs = []
    for c in range(SWA_WIDTH // LANES):
        xc = x[:, LANES * c:LANES * (c + 1)]
        outs.append(xc * cos + pltpu.roll(xc, LANES - ROPE_DIMS // 2, 1) * sa
                    + pltpu.roll(xc, ROPE_DIMS // 2, 1) * sb)
    return jnp.concatenate(outs, axis=1)


def _proj_kernel(x_ref, sc_ref, sh_ref, g_ref, w_ref, cos_ref, sa_ref, sb_ref, *rest, emit_tail):
    n_out = len(W_COLS) + (2 if emit_tail else 0)
    outs = rest[len(rest) - n_out:]
    x = x_ref[0]
    ms = jnp.mean(x * x, axis=-1, keepdims=True)
    h = x * lax.rsqrt(ms + NORM_EPS) * g_ref[...]
    h = h * (1.0 + sc_ref[0]) + sh_ref[0]
    hb = h.astype(BF16)

    def mm(name):
        c0, c1 = W_OFF[name]
        return jnp.dot(hb, w_ref[:, c0:c1], preferred_element_type=F32)

    names = [n for n, _ in W_COLS]
    cos, sa, sb = cos_ref[...], sa_ref[...], sb_ref[...]
    for idx, name in enumerate(names):
        val = mm(name)
        if name in ("b_q", "b_k"):
            val = _rope(val, cos, sa, sb)
        if emit_tail and name in SWA_PIECES:
            for hp in range(SWA_WIDTH // LANES):
                outs[idx][0, hp] = val[:, hp * LANES:(hp + 1) * LANES]
        else:
            outs[idx][0] = val
        if emit_tail and name == "b_k":
            outs[len(names)][0, 0] = val.T.reshape(SWA_HEADS, SWA_HD, val.shape[0])
        if emit_tail and name == "b_v":
            outs[len(names) + 1][0, 0] = val.T.reshape(SWA_HEADS, SWA_HD, val.shape[0])


def _proj(x, scale, shift, g_pre, w_cat, rope_tabs, *, tm, tail=None):
    nb, T, _ = x.shape
    nt = T // tm
    R = scale.shape[1]
    emit_tail = tail is not None
    if R == 1:
        mod_spec = pl.BlockSpec((1, 1, D_MODEL), lambda b, j: (b, 0, 0))
    else:
        mod_spec = pl.BlockSpec((1, tm, D_MODEL), lambda b, j: (b, j, 0))
    tab_spec = pl.BlockSpec((tm, LANES), lambda b, j: (j, 0))
    out_specs, out_shape = [], []
    for name, w in W_COLS:
        if emit_tail and name in SWA_PIECES:
            out_specs.append(pl.BlockSpec((1, w // LANES, tm, LANES), lambda b, j: (b, 0, j, 0)))
            out_shape.append(jax.ShapeDtypeStruct((nb, w // LANES, T, LANES), F32))
        else:
            out_specs.append(pl.BlockSpec((1, tm, w), lambda b, j: (b, j, 0)))
            out_shape.append(jax.ShapeDtypeStruct((nb, T, w), F32))
    in_specs = [pl.BlockSpec((1, tm, D_MODEL), lambda b, j: (b, j, 0)),
                mod_spec, mod_spec,
                pl.BlockSpec((1, D_MODEL), lambda b, j: (0, 0)),
                pl.BlockSpec((D_MODEL, W_TOTAL), lambda b, j: (0, 0)),
                tab_spec, tab_spec, tab_spec]
    args = [x, scale, shift, g_pre.reshape(1, D_MODEL), w_cat, *rope_tabs]
    aliases = {}
    if emit_tail:
        l, bufs = tail
        keep = min(SWA_WMAX, T)
        first = (T - keep) // tm
        tail_spec = pl.BlockSpec((1, 1, SWA_HEADS, SWA_HD, tm),
                                 lambda b, j: (l, b, 0, 0, jnp.maximum(j - first, 0)))
        out_specs += [tail_spec, tail_spec]
        out_shape += [jax.ShapeDtypeStruct((DEPTH, nb, SWA_HEADS, SWA_HD, keep), F32)] * 2
        if bufs is not None:
            in_specs += [pl.BlockSpec(memory_space=pl.ANY)] * 2
            aliases = {len(args): len(W_COLS), len(args) + 1: len(W_COLS) + 1}
            args += list(bufs)
    res = pl.pallas_call(
        functools.partial(_proj_kernel, emit_tail=emit_tail),
        grid=(nb, nt),
        in_specs=in_specs,
        out_specs=out_specs,
        out_shape=out_shape,
        input_output_aliases=aliases,
        compiler_params=_params("arbitrary", "arbitrary"),
        name="in_proj",
    )(*args)
    names = [n for n, _ in W_COLS]
    d = dict(zip(names, res[:len(names)]))
    if emit_tail:
        d["k_tail"], d["v_tail"] = res[len(names)], res[len(names) + 1]
    return d


def _cumsum_rows(x, C):
    n = x.shape[0]
    blk = min(n, max(C, LANES))
    ri = lax.broadcasted_iota(jnp.int32, (blk, blk), 0)
    ci = lax.broadcasted_iota(jnp.int32, (blk, blk), 1)
    tri = jnp.where((ri >= ci) & (ri // C == ci // C), 1.0, 0.0).astype(BF16)
    hi = x.astype(BF16)
    r1 = x - hi.astype(F32)
    mid = r1.astype(BF16)
    lo = (r1 - mid.astype(F32)).astype(BF16)
    out = []
    for b in range(n // blk):
        rs = slice(b * blk, (b + 1) * blk)
        dot = lambda t: jnp.dot(tri, t[rs], preferred_element_type=F32)
        out.append(dot(hi) + dot(mid) + dot(lo))
    return jnp.concatenate(out, axis=0)


def _gla_kernel(qk_ref, v_ref, gate_ref, lr_ref, wlr_ref, blr_ref, gg_ref, s0_ref,
                o_ref, sf_ref, s_scr, o_scr, *, TT, C, t_valid, T, own_state):
    j = pl.program_id(1)
    H, DK, DV = GLA_HEADS, GLA_DK, GLA_DV
    HK, HV = H * DK, H * DV
    i0 = lambda shape: lax.broadcasted_iota(jnp.int32, shape, 0)
    i1 = lambda shape: lax.broadcasted_iota(jnp.int32, shape, 1)
    own_cols = i0((H * C, HK)) // C == i1((H * C, HK)) // DK
    causal = i0((H * C, C)) % C >= i1((H * C, C))
    diag = i0((HK, HV)) // DK == i1((HK, HV)) // DV
    lane_head = i1((C, HV)) // DV

    def block_diag(s):
        return jnp.where(diag, jnp.concatenate([s.reshape(HK, DV)] * H, axis=1), 0.0)

    if not own_state:
        @pl.when(j == 0)
        def _():
            s_scr[...] = block_diag(s0_ref[0])

    z = _dot_f32(lr_ref[0], wlr_ref[...]) + blr_ref[...]
    glog = (jnp.minimum(z, 0.0) - jnp.log1p(jnp.exp(-jnp.abs(z)))) * (1.0 / GLA_TAU)
    qk = qk_ref[0]
    q = qk[:, :HK]
    k = qk[:, HK:]
    if t_valid < (C if own_state else T):
        row = j * TT + lax.broadcasted_iota(jnp.int32, (TT, HK), 0)
        row = row % C if own_state else row
        glog = jnp.where(row < t_valid, glog, 0.0)
        k = jnp.where(row < t_valid, k, 0.0)
    v = v_ref[0]
    nc = TT // C
    bc = _cumsum_rows(glog, C)
    bl = jnp.concatenate([jnp.broadcast_to(bc[(c + 1) * C - 1:(c + 1) * C], (C, HK))
                          for c in range(nc)], axis=0)
    qd = q * (DK ** -0.5) * jnp.exp(bc)
    kd = k * jnp.exp(-bc)
    kdl = k * jnp.exp(bl - bc)
    dec = jnp.exp(bl)
    def take_heads(s_bd, dst):
        for h in range(H):
            dst[h] = s_bd[h * DK:(h + 1) * DK, h * DV:(h + 1) * DV]

    s_prev = None if own_state else s_scr[...]
    for c in range(nc):
        rows = slice(c * C, (c + 1) * C)
        qc, kc, klc, vc = qd[rows], kd[rows], kdl[rows], v[rows]
        if own_state:
            s_prev = block_diag(s0_ref[c])
        qz = jnp.where(own_cols, jnp.concatenate([qc] * H, axis=0), 0.0)
        att = jnp.where(causal, _dot_nt(qz, kc), 0.0)
        res = _dot(att, vc)
        intra = res[0:C]
        for h in range(1, H):
            intra = jnp.where(lane_head == h, res[h * C:(h + 1) * C], intra)
        o_scr[rows, :] = intra + _dot(qc, s_prev)
        dec_col = jnp.broadcast_to(dec[c * C:c * C + 1], (8, HK)).T[:, 0:1]
        s_prev = dec_col * s_prev + jnp.where(diag, _dot_tn(klc, vc), 0.0)
        if own_state:
            take_heads(s_prev, sf_ref.at[c])
    if not own_state:
        s_scr[...] = s_prev
        take_heads(s_prev, sf_ref.at[0])

    o = o_scr[...]
    sq = o * o
    hi = sq.astype(BF16)
    r1 = sq - hi.astype(F32)
    mid = r1.astype(BF16)
    lo = (r1 - mid.astype(F32)).astype(BF16)
    same_head = jnp.where(i0((HV, HV)) // DV == i1((HV, HV)) // DV, 1.0, 0.0).astype(BF16)
    hsum = lambda t: jnp.dot(t, same_head, preferred_element_type=F32)
    ms = (hsum(hi) + hsum(mid) + hsum(lo)) * (1.0 / DV)
    o_ref[0] = (o * lax.rsqrt(ms + NORM_EPS) * gg_ref[...] * _silu(gate_ref[0])).astype(o_ref.dtype)


def _gla(p, w_lr, b_lr, g_gla, s0, *, TT, C, t_valid, own_state=False):
    B, T, _ = p["a_qk"].shape
    HK = GLA_HEADS * GLA_DK
    row = lambda w: pl.BlockSpec((1, TT, w), lambda b, j: (b, j, 0))
    full = lambda r, c: pl.BlockSpec((r, c), lambda b, j: (0, 0))
    if own_state:
        st_spec = pl.BlockSpec((TT // C, GLA_HEADS, GLA_DK, GLA_DV), lambda b, j: (j, 0, 0, 0))
    else:
        st_spec = pl.BlockSpec((1, GLA_HEADS, GLA_DK, GLA_DV), lambda b, j: (b, 0, 0, 0))
    return pl.pallas_call(
        functools.partial(_gla_kernel, TT=TT, C=C, t_valid=t_valid, T=T, own_state=own_state),
        grid=(B, T // TT),
        in_specs=[row(2 * HK), row(GLA_WIDTH), row(GLA_WIDTH), row(LANES),
                  full(LANES, HK), full(1, HK), full(1, GLA_WIDTH), st_spec],
        out_specs=[row(GLA_WIDTH), st_spec],
        out_shape=[jax.ShapeDtypeStruct((B, T, GLA_WIDTH), MIX_DTYPE),
                   jax.ShapeDtypeStruct(s0.shape, F32)],
        scratch_shapes=[pltpu.VMEM((HK, GLA_WIDTH), F32), pltpu.VMEM((TT, GLA_WIDTH), F32)],
        compiler_params=_params("arbitrary", "arbitrary"),
        name="gla_mix",
    )(p["a_qk"], p["a_v"], p["a_gate"], p["a_lr"], w_lr, b_lr.reshape(1, HK),
      g_gla.reshape(1, GLA_WIDTH), s0)


def _swa_kernel(q_ref, k_ref, v_ref, g_ref, o_ref, qd_scr, kd_scr, vd_scr, op_scr, lse_scr,
                stage_scr, *, T):
    QB = SWA_QBLOCK
    CP = 256
    (_, d0), (_, d1), (_, d2) = DILATED_PATTERNS
    assert d0 == 1 and d2 % d1 == 0
    dd = d2 // d1

    def regroup(src_ref, dst_scr, scale):
        cast = lambda x: (x * scale).astype(BF16)
        dst = lambda j: pl.ds(pl.multiple_of(j * CP, CP), CP)

        def copy0(j, carry):
            dst_scr[0, dst(j), :] = cast(src_ref[0, 0, dst(j), :])
            return carry

        def copy1(j, carry):
            per = T // d1 // CP
            x = src_ref[0, 0, pl.ds(j // per + d1 * CP * (j % per), CP, stride=d1), :]
            stage_scr[dst(j), :] = x
            dst_scr[1, dst(j), :] = cast(x)
            return carry

        def copy2(j, carry):
            per = T // d2 // CP
            r = j // per
            start = (r % d1) * (T // d1) + r // d1 + dd * CP * (j % per)
            dst_scr[2, dst(j), :] = cast(stage_scr[pl.ds(start, CP, stride=dd), :])
            return carry

        for body in (copy0, copy1, copy2):
            lax.fori_loop(0, T // CP, body, 0)

    regroup(q_ref, qd_scr, SWA_HD ** -0.5)
    regroup(k_ref, kd_scr, 1.0)
    regroup(v_ref, vd_scr, 1.0)

    ri = lax.broadcasted_iota(jnp.int32, (2 * QB, 2 * QB), 0) % QB
    ci = lax.broadcasted_iota(jnp.int32, (2 * QB, 2 * QB), 1)
    band = (ci >= ri) & (ci <= ri + QB)
    bias_band = jnp.where(band, 0.0, NEG_BIG)
    bias_first = jnp.where(band & (ci >= QB), 0.0, NEG_BIG)
    lo_q = lax.broadcasted_iota(jnp.int32, (QB, LANES), 1) < SWA_HD
    ones = jnp.ones((2 * QB, LANES), BF16)

    def step(i, carry):
        cur = pl.ds(pl.multiple_of(i * QB, QB), QB)
        prev = pl.ds(pl.multiple_of(jnp.maximum(i - 1, 0) * QB, QB), QB)
        for p, (_, d) in enumerate(DILATED_PATTERNS):
            nblk = T // d // QB
            r = i // nblk
            n = i % nblk
            q = qd_scr[p, cur, :]
            qz = jnp.concatenate([jnp.where(lo_q, q, 0), jnp.where(lo_q, 0, q)], axis=0)
            k = jnp.concatenate([kd_scr[p, prev, :], kd_scr[p, cur, :]], axis=0)
            v = jnp.concatenate([vd_scr[p, prev, :], vd_scr[p, cur, :]], axis=0)
            s = lax.dot_general(qz, k, (((1,), (1,)), ((), ())), preferred_element_type=F32)
            s = s + jnp.where(n > 0, bias_band, bias_first)
            m = jnp.max(s, axis=-1, keepdims=True)
            e = jnp.exp(s - m).astype(BF16)
            ox = jnp.dot(e, jnp.concatenate([v, ones], axis=1), preferred_element_type=F32)
            num = jnp.where(lo_q, ox[:QB, :LANES], ox[QB:, :LANES])
            den = jnp.where(lo_q, ox[:QB, LANES:], ox[QB:, LANES:])
            mx = jnp.where(lo_q, jnp.broadcast_to(m[:QB], (QB, LANES)),
                           jnp.broadcast_to(m[QB:], (QB, LANES)))
            start = r + d * QB * n
            dst = pl.ds(start, QB) if d == 1 else pl.ds(start, QB, stride=d)
            op_scr[p, dst, :] = num / den
            lse_scr[p, dst, :] = mx + jnp.log(den)
        return carry

    lax.fori_loop(0, T // QB, step, 0, unroll=4)

    CH = 512

    def combine(i, carry):
        rs = pl.ds(pl.multiple_of(i * CH, CH), CH)
        l0, l1, l2 = lse_scr[0, rs, :], lse_scr[1, rs, :], lse_scr[2, rs, :]
        mx = jnp.maximum(jnp.maximum(l0, l1), l2)
        w0, w1, w2 = jnp.exp(l0 - mx), jnp.exp(l1 - mx), jnp.exp(l2 - mx)
        mixed = (w0 * op_scr[0, rs, :] + w1 * op_scr[1, rs, :] + w2 * op_scr[2, rs, :]) / (w0 + w1 + w2)
        o_ref[0, 0, rs, :] = (mixed * _silu(g_ref[0, 0, rs, :])).astype(o_ref.dtype)
        return carry

    lax.fori_loop(0, T // CH, combine, 0)


def _swa_prompt(p):
    B, NP, T, _ = p["b_q"].shape
    spec = pl.BlockSpec((1, 1, T, LANES), lambda b, hp: (b, hp, 0, 0))
    return pl.pallas_call(
        functools.partial(_swa_kernel, T=T),
        grid=(B, NP),
        in_specs=[spec] * 4,
        out_specs=spec,
        out_shape=jax.ShapeDtypeStruct((B, NP, T, LANES), MIX_DTYPE),
        scratch_shapes=[pltpu.VMEM((len(DILATED_PATTERNS), T, LANES), BF16)] * 3
        + [pltpu.VMEM((len(DILATED_PATTERNS), T, LANES), F32)] * 2
        + [pltpu.VMEM((T, LANES), F32)],
        compiler_params=_params("arbitrary", "arbitrary"),
        name="swa_prompt",
    )(p["b_q"], p["b_k"], p["b_v"], p["b_gate"])


def _swa_dec_kernel(ck_ref, cv_ref, q_ref, k_ref, v_ref, g_ref, mult_ref, multf_ref, *rest, L, Tn):
    o_ref, ko_ref, vo_ref = rest[-3:]
    HG = ck_ref.shape[2]
    lane = lax.broadcasted_iota(jnp.int32, (HG, SWA_HD, LANES), 2)

    def shifted(c_ref, new_rows):
        r = pltpu.roll(c_ref[0, 0], L - Tn, 2)
        moved = pltpu.roll(new_rows, SAMPLE_PAD - Tn, 0)
        pad = jnp.concatenate([jnp.zeros((LANES - SAMPLE_PAD, HG * SWA_HD), F32), moved], axis=0)
        tile = pad.T.reshape(HG, SWA_HD, LANES)
        last = jnp.where(lane >= LANES - Tn, tile, r[:, :, L - LANES:])
        return jnp.concatenate([r[:, :, :L - LANES], last], axis=2)

    kout = shifted(ck_ref, k_ref[0])
    vout = shifted(cv_ref, v_ref[0])
    ko_ref[0, 0] = kout
    vo_ref[0, 0] = vout

    q = q_ref[0] * (SWA_HD ** -0.5)
    q3 = jnp.stack([q[:, h * SWA_HD:(h + 1) * SWA_HD] for h in range(HG)], axis=0).astype(BF16)
    qk = lambda kk: jnp.einsum('hid,hdr->hir', q3, kk.astype(BF16), preferred_element_type=F32)
    pv = lambda pp, vv: jnp.einsum('hir,hdr->hid', pp.astype(BF16), vv.astype(BF16),
                                   preferred_element_type=F32)
    mult = mult_ref[...][None]
    multf = multf_ref[...][None]
    s = jnp.where(mult > 0, qk(kout), NEG_BIG)
    sf = jnp.where(multf > 0, qk(ck_ref[0, 0, :, :, 0:LANES]), NEG_BIG)
    m = jnp.maximum(jnp.max(s, axis=-1, keepdims=True), jnp.max(sf, axis=-1, keepdims=True))
    pe = jnp.exp(s - m) * mult
    pf = jnp.exp(sf - m) * multf
    den = jnp.sum(pe, axis=-1, keepdims=True) + jnp.sum(pf, axis=-1, keepdims=True)
    den = jnp.where(den > 0, den, 1.0)
    o3 = (pv(pe, vout) + pv(pf, cv_ref[0, 0, :, :, 0:LANES])) / den
    o = jnp.concatenate([o3[h] for h in range(HG)], axis=1)
    o_ref[0] = (o * _silu(g_ref[0])).astype(o_ref.dtype)


def _mult_tables(L, Tn):
    assert L == SWA_WMAX
    qpos = SWA_WMAX + np.arange(SAMPLE_PAD)[:, None]

    def count(kpos):
        off = qpos - kpos[None, :]
        cnt = np.zeros(off.shape, np.float32)
        for w, d in DILATED_PATTERNS:
            cnt += ((off >= 0) & (off <= w) & (off % d == 0)).astype(np.float32)
        cnt[Tn:] = 0.0
        return cnt

    mult = count(np.arange(L) + Tn)
    multf = count(np.arange(LANES))
    multf[:, Tn:] = 0.0
    return jnp.asarray(mult), jnp.asarray(multf)


def _swa_sample(p, cache_k, cache_v, l, Tn, bufs):
    B = p["b_q"].shape[0]
    L = cache_k.shape[-1]
    HG = 8
    mult, multf = _mult_tables(L, Tn)
    cspec = pl.BlockSpec((1, 1, HG, SWA_HD, L), lambda b, g: (l, b, g, 0, 0))
    nspec = pl.BlockSpec((1, SAMPLE_PAD, HG * SWA_HD), lambda b, g: (b, 0, g))
    const = lambda a: pl.BlockSpec(a.shape, lambda b, g: (0, 0))
    stacked = jax.ShapeDtypeStruct(cache_k.shape, F32)
    in_specs = [cspec, cspec, nspec, nspec, nspec, nspec, const(mult), const(multf)]
    args = [cache_k, cache_v, p["b_q"], p["b_k"], p["b_v"], p["b_gate"], mult, multf]
    aliases = {}
    if bufs is not None:
        in_specs += [pl.BlockSpec(memory_space=pl.ANY)] * 2
        aliases = {len(args): 1, len(args) + 1: 2}
        args += list(bufs)
    return pl.pallas_call(
        functools.partial(_swa_dec_kernel, L=L, Tn=Tn),
        grid=(B, SWA_HEADS // HG),
        in_specs=in_specs,
        out_specs=[nspec, cspec, cspec],
        out_shape=[jax.ShapeDtypeStruct((B, SAMPLE_PAD, SWA_WIDTH), MIX_DTYPE), stacked, stacked],
        input_output_aliases=aliases,
        compiler_params=_params("arbitrary", "arbitrary"),
        name="swa_sample",
    )(*args)


def _s5prep_kernel(lr_ref, li_ref, ldt_ref, bre_ref, bim_ref, bcat_ref, tab_ref):
    lr = lr_ref[0]
    li = li_ref[0]
    dt = jnp.exp(ldt_ref[0])
    mag = jnp.exp(lr * dt)
    br = mag * jnp.cos(li * dt)
    bi = mag * jnp.sin(li * dt)
    den = lr * lr + li * li
    nr = br - 1.0
    cr = (nr * lr + bi * li) / den
    ci = (bi * lr - nr * li) / den
    bre = bre_ref[0]
    bim = bim_ref[0]
    bcat_ref[0, :, 0:SSM_FLAT] = (cr * bre - ci * bim).astype(BF16)
    bcat_ref[0, :, SSM_FLAT:2 * SSM_FLAT] = (cr * bim + ci * bre).astype(BF16)
    tab_ref[0, 0] = jnp.broadcast_to(br, (8, SSM_FLAT))
    tab_ref[0, 1] = jnp.broadcast_to(bi, (8, SSM_FLAT))


def _s5_prep(lam_re, lam_im, log_dt, b_re, b_im):
    eye = jnp.eye(SSM_GROUPS, dtype=F32)

    def block_diag_b(b):
        bt = jnp.transpose(b, (0, 1, 3, 2))
        return (bt[:, :, :, None, :] * eye[None, :, None, :, None]).reshape(DEPTH, SSM_WIDTH, SSM_FLAT)

    flat = lambda a: a.reshape(DEPTH, 1, SSM_FLAT)
    ldt = jnp.broadcast_to(log_dt[:, :, None], (DEPTH, SSM_GROUPS, SSM_STATE))
    vec = pl.BlockSpec((1, 1, SSM_FLAT), lambda l: (l, 0, 0))
    mat = pl.BlockSpec((1, SSM_WIDTH, SSM_FLAT), lambda l: (l, 0, 0))
    return pl.pallas_call(
        _s5prep_kernel,
        grid=(DEPTH,),
        in_specs=[vec, vec, vec, mat, mat],
        out_specs=[pl.BlockSpec((1, SSM_WIDTH, 2 * SSM_FLAT), lambda l: (l, 0, 0)),
                   pl.BlockSpec((1, 2, 8, SSM_FLAT), lambda l: (l, 0, 0, 0))],
        out_shape=[jax.ShapeDtypeStruct((DEPTH, SSM_WIDTH, 2 * SSM_FLAT), BF16),
                   jax.ShapeDtypeStruct((DEPTH, 2, 8, SSM_FLAT), F32)],
        compiler_params=_params("arbitrary"),
        name="s5_prep",
    )(flat(lam_re), flat(lam_im), flat(ldt), block_diag_b(b_re), block_diag_b(b_im))


def _s5_kernel(u_ref, gate_ref, x0r_ref, x0i_ref, bcat_ref, cre_ref, cim_ref, tab_ref, d_ref,
               wglu_ref, bglu_ref, o_ref, sr_ref, si_ref, xr_scr, xi_scr, cr_scr, ci_scr,
               *, TT, B, last_tile, last_t):
    j = pl.program_id(0)
    NBG, NL = xr_scr.shape[0], xr_scr.shape[1]
    split = lambda a: jnp.stack([a[:, LANES * i:LANES * (i + 1)] for i in range(NL)], axis=0)
    merge = lambda a: jnp.concatenate([a[i] for i in range(NL)], axis=1)

    @pl.when(j == 0)
    def _():
        xr_scr[...] = jnp.zeros(xr_scr.shape, F32)
        xi_scr[...] = jnp.zeros(xi_scr.shape, F32)
        for g in range(NBG):
            cr_scr[g] = split(x0r_ref[g])
            ci_scr[g] = split(x0i_ref[g])

    u = u_ref[...].reshape(B * TT, SSM_WIDTH)
    bu = _dot(u, bcat_ref[...])
    for b in range(B):
        g, s = divmod(b, 8)
        rows = pl.ds(s, TT, stride=8)
        for i in range(NL):
            xr_scr[g, i, rows, :] = bu[b * TT:(b + 1) * TT, LANES * i:LANES * (i + 1)]
            xi_scr[g, i, rows, :] = bu[b * TT:(b + 1) * TT, SSM_FLAT + LANES * i:SSM_FLAT + LANES * (i + 1)]

    lam_r = split(tab_ref[0])
    lam_i = split(tab_ref[1])

    def step(t, carry):
        rows = pl.ds(pl.multiple_of(t * 8, 8), 8)
        out = []
        for g in range(NBG):
            cr, ci = carry[2 * g], carry[2 * g + 1]
            nr = lam_r * cr - lam_i * ci + xr_scr[g, :, rows, :]
            ni = lam_r * ci + lam_i * cr + xi_scr[g, :, rows, :]
            xr_scr[g, :, rows, :] = nr
            xi_scr[g, :, rows, :] = ni
            out += [nr, ni]
        return tuple(out)

    init = tuple(ref[g] for g in range(NBG) for ref in (cr_scr, ci_scr))
    fin = lax.fori_loop(0, TT, step, init, unroll=min(TT, 8))
    for g in range(NBG):
        cr_scr[g] = fin[2 * g]
        ci_scr[g] = fin[2 * g + 1]

    @pl.when(j == last_tile)
    def _():
        rows = slice(last_t * 8, last_t * 8 + 8)
        for g in range(NBG):
            sr_ref[g] = merge(xr_scr[g, :, rows, :])
            si_ref[g] = merge(xi_scr[g, :, rows, :])

    def gather(scr):
        per_b = []
        for b in range(B):
            g, s = divmod(b, 8)
            rows = pl.ds(s, TT, stride=8)
            per_b.append(jnp.concatenate([scr[g, i, rows, :] for i in range(NL)], axis=1))
        return jnp.concatenate(per_b, axis=0)

    y = _dot(gather(xr_scr), cre_ref[...]) - _dot(gather(xi_scr), cim_ref[...]) + d_ref[...] * u
    z = jax.nn.gelu(y)
    out = z * jax.nn.sigmoid(_dot(z, wglu_ref[...]) + bglu_ref[...])
    out = out * _silu(gate_ref[...].reshape(B * TT, SSM_WIDTH))
    o_ref[...] = out.reshape(B, TT, SSM_WIDTH).astype(o_ref.dtype)


def _s5(p, x0r, x0i, bcat, cre, cim, tab, d_skip, w_glu, b_glu, *, TT, t_valid):
    B, T, _ = p["c_u"].shape
    NBG = x0r.shape[0]
    NL = SSM_FLAT // LANES
    last = t_valid - 1
    row = pl.BlockSpec((B, TT, SSM_WIDTH), lambda j: (0, j, 0))
    st = pl.BlockSpec((NBG, 8, SSM_FLAT), lambda j: (0, 0, 0))
    full = lambda a: pl.BlockSpec(a.shape, lambda j: (0,) * a.ndim)
    d2 = d_skip.reshape(1, SSM_WIDTH)
    bg2 = b_glu.reshape(1, SSM_WIDTH)
    state = jax.ShapeDtypeStruct((NBG, 8, SSM_FLAT), F32)
    return pl.pallas_call(
        functools.partial(_s5_kernel, TT=TT, B=B, last_tile=last // TT, last_t=last % TT),
        grid=(T // TT,),
        in_specs=[row, row, st, st, full(bcat), full(cre), full(cim), full(tab), full(d2),
                  full(w_glu), full(bg2)],
        out_specs=[row, st, st],
        out_shape=[jax.ShapeDtypeStruct((B, T, SSM_WIDTH), MIX_DTYPE), state, state],
        scratch_shapes=[pltpu.VMEM((NBG, NL, TT * 8, LANES), F32)] * 2
        + [pltpu.VMEM((NBG, NL, 8, LANES), F32)] * 2,
        compiler_params=_params("arbitrary"),
        name="s5_mix",
    )(p["c_u"], p["c_gate"], x0r, x0i, bcat, cre, cim, tab, d2, w_glu, bg2)


def _out_kernel(oa_ref, ob_ref, oc_ref, x_ref, gate_ref, g_ref, w_ref, o_ref):
    a1 = GLA_WIDTH
    a2 = GLA_WIDTH + SWA_WIDTH
    if len(ob_ref.shape) == 4:
        ob = jnp.concatenate([ob_ref[0, hp] for hp in range(ob_ref.shape[1])], axis=1)
    else:
        ob = ob_ref[0]
    y = (jnp.dot(oa_ref[0].astype(BF16), w_ref[0:a1, :], preferred_element_type=F32)
         + jnp.dot(ob.astype(BF16), w_ref[a1:a2, :], preferred_element_type=F32)
         + jnp.dot(oc_ref[0].astype(BF16), w_ref[a2:, :], preferred_element_type=F32))
    ms = jnp.mean(y * y, axis=-1, keepdims=True)
    yn = y * lax.rsqrt(ms + NORM_EPS) * g_ref[...]
    o_ref[0] = x_ref[0] + gate_ref[0] * yn


def _out_proj(oa, ob, oc, x, gate, g_post, w_out, *, tm):
    nb, T, _ = x.shape
    R = gate.shape[1]
    row = lambda w: pl.BlockSpec((1, tm, w), lambda b, j: (b, j, 0))
    if R == 1:
        gate_spec = pl.BlockSpec((1, 1, D_MODEL), lambda b, j: (b, 0, 0))
    else:
        gate_spec = row(D_MODEL)
    if ob.ndim == 4:
        ob_spec = pl.BlockSpec((1, ob.shape[1], tm, LANES), lambda b, j: (b, 0, j, 0))
    else:
        ob_spec = row(SWA_WIDTH)
    return pl.pallas_call(
        _out_kernel,
        grid=(nb, T // tm),
        in_specs=[row(GLA_WIDTH), ob_spec, row(SSM_WIDTH), row(D_MODEL), gate_spec,
                  pl.BlockSpec((1, D_MODEL), lambda b, j: (0, 0)),
                  pl.BlockSpec((D_MODEL, D_MODEL), lambda b, j: (0, 0))],
        out_specs=row(D_MODEL),
        out_shape=jax.ShapeDtypeStruct((nb, T, D_MODEL), F32),
        compiler_params=_params("arbitrary", "arbitrary"),
        name="out_proj",
    )(oa, ob, oc, x, gate, g_post.reshape(1, D_MODEL), w_out)


def _rope_tables(pos):
    half = ROPE_DIMS // 2
    inv = ROPE_THETA ** (-jnp.arange(half, dtype=F32) * (2.0 / ROPE_DIMS))
    ang = pos.astype(F32)[:, None] * inv[None, :]
    cos, sin = jnp.cos(ang), jnp.sin(ang)
    n = pos.shape[0]
    rest = SWA_HD - ROPE_DIMS
    head = lambda a, b, fill: jnp.concatenate([a, b, jnp.full((n, rest), fill, F32)], axis=1)
    zero = jnp.zeros((n, half), F32)
    reps = LANES // SWA_HD
    return (jnp.tile(head(cos, cos, 1.0), (1, reps)),
            jnp.tile(head(-sin, zero, 0.0), (1, reps)),
            jnp.tile(head(zero, sin, 0.0), (1, reps)))


def _rearrange_w_in(w_in):
    GK = GLA_HEADS * GLA_DK
    o = np.cumsum([0, GK, GK, GLA_WIDTH, GLA_RANK, GLA_WIDTH, SWA_WIDTH, SWA_WIDTH, SWA_WIDTH,
                   SWA_WIDTH, SSM_WIDTH, SSM_WIDTH])
    piece = lambda i: w_in[:, :, o[i]:o[i + 1]]
    lr = jnp.pad(piece(3), ((0, 0), (0, 0), (0, LANES - GLA_RANK)))
    cat = jnp.concatenate([piece(0), piece(1), piece(2), piece(4), piece(5), piece(6), piece(7),
                           piece(8), piece(9), piece(10), lr], axis=-1)
    return cat.astype(BF16)


def kernel(x_prompt, x_sample, c_prompt, c_sample, state_gla, cache_swa_k, cache_swa_v, state_ssm_re, state_ssm_im, w_ada, b_ada, g_pre, g_post, w_in, w_gla_lr, b_gla_lr, g_gla, ssm_lambda_re, ssm_lambda_im, ssm_log_dt, ssm_b_re, ssm_b_im, ssm_c_re, ssm_c_im, ssm_d, w_glu, b_glu, w_out):
    Bp, Tp, _ = x_prompt.shape
    Bs, Ts, _ = x_sample.shape
    PAD = SAMPLE_PAD
    Lc = cache_swa_k.shape[2]

    w_cat = _rearrange_w_in(w_in)
    w_out_b = w_out.astype(BF16)
    w_glu_b = w_glu.astype(BF16)
    w_lr_pad = jnp.pad(w_gla_lr, ((0, 0), (0, LANES - GLA_RANK), (0, 0)))
    eye = jnp.eye(SSM_GROUPS, dtype=F32)

    def block_diag_c(c):
        ct = jnp.transpose(c, (0, 1, 3, 2))
        return (ct[:, :, :, None, :] * eye[None, :, None, :, None]).reshape(DEPTH, SSM_FLAT, SSM_WIDTH)

    cre = block_diag_c(ssm_c_re).astype(BF16)
    cim = block_diag_c(ssm_c_im).astype(BF16)
    bcat, tab = _s5_prep(ssm_lambda_re, ssm_lambda_im, ssm_log_dt, ssm_b_re, ssm_b_im)

    n_c = Bp + Bs
    c_rows = -(-n_c // 8) * 8
    c_all = jnp.pad(jnp.concatenate([c_prompt, c_sample], axis=0), ((0, c_rows - n_c), (0, 0)))
    mod = _ada_mod(c_all, w_ada, b_ada)

    tabs_p = _rope_tables(jnp.arange(Tp))
    tabs_s = _rope_tables(PAST_LEN + jnp.arange(Bs * PAD) % PAD)

    xp = x_prompt
    xs = jnp.pad(x_sample, ((0, 0), (0, PAD - Ts), (0, 0))).reshape(1, Bs * PAD, D_MODEL)
    seq_last = lambda a: jnp.transpose(a, (0, 1, 3, 4, 2))
    seq_first = lambda a: jnp.transpose(a, (0, 1, 4, 2, 3))
    cache_k = seq_last(cache_swa_k)
    cache_v = seq_last(cache_swa_v)
    zero_gla = jnp.zeros((Bp, GLA_HEADS, GLA_DK, GLA_DV), F32)
    groups = lambda n: -(-n // 8)
    ssm_in = lambda a: jnp.pad(a.reshape(a.shape[0], SSM_FLAT),
                               ((0, groups(a.shape[0]) * 8 - a.shape[0]), (0, 0))
                               ).reshape(groups(a.shape[0]), 8, SSM_FLAT)
    zero_ssm = jnp.zeros((groups(Bp), 8, SSM_FLAT), F32)

    acc = [[] for _ in range(6)]
    tails = None
    shifted = None
    for l in range(DEPTH):
        def chunk(rows, k):
            return mod[l, rows, k * D_MODEL:(k + 1) * D_MODEL]

        pr = slice(0, Bp)
        shift, scale, gate = (chunk(pr, k).reshape(Bp, 1, D_MODEL) for k in range(3))
        p = _proj(xp, scale, shift, g_pre[l], w_cat[l], tabs_p, tm=512, tail=(l, tails))
        tails = (p["k_tail"], p["v_tail"])
        oa, gla_p = _gla(p, w_lr_pad[l], b_gla_lr[l], g_gla[l], zero_gla, TT=1024, C=GLA_CHUNK, t_valid=Tp)
        ob = _swa_prompt(p)
        oc, re_p, im_p = _s5(p, zero_ssm, zero_ssm, bcat[l], cre[l], cim[l], tab[l], ssm_d[l],
                             w_glu_b[l], b_glu[l], TT=128, t_valid=Tp)
        xp = _out_proj(oa, ob, oc, xp, gate, g_post[l], w_out_b[l], tm=512)

        sr = slice(Bp, Bp + Bs)
        shift, scale, gate = (jnp.repeat(chunk(sr, k), PAD, axis=0).reshape(1, Bs * PAD, D_MODEL)
                              for k in range(3))
        qf = _proj(xs, scale, shift, g_pre[l], w_cat[l], tabs_s, tm=Bs * PAD)
        q = {n: a.reshape(Bs, PAD, a.shape[-1]) for n, a in qf.items()}
        oa, gla_s = _gla(qf, w_lr_pad[l], b_gla_lr[l], g_gla[l], state_gla[l], TT=Bs * PAD, C=PAD,
                         t_valid=Ts, own_state=True)
        ob, k_s, v_s = _swa_sample(q, cache_k, cache_v, l, Ts, shifted)
        shifted = (k_s, v_s)
        oc, re_s, im_s = _s5(q, ssm_in(state_ssm_re[l]), ssm_in(state_ssm_im[l]), bcat[l], cre[l],
                             cim[l], tab[l], ssm_d[l], w_glu_b[l], b_glu[l], TT=PAD, t_valid=Ts)
        flat = lambda a: a.reshape(1, Bs * PAD, a.shape[-1])
        xs = _out_proj(flat(oa), flat(ob), flat(oc), xs, gate, g_post[l], w_out_b[l], tm=Bs * PAD)

        for i, a in enumerate((gla_p, gla_s, re_p, im_p, re_s, im_s)):
            acc[i].append(a)

    st = [jnp.stack(a) for a in acc]
    ssm = lambda a, n: a.reshape(DEPTH, -1, SSM_GROUPS, SSM_STATE)[:, :n]
    y_sample = xs.reshape(Bs, PAD, D_MODEL)[:, :Ts]
    return (xp, y_sample, st[0], st[1], seq_first(tails[0]), seq_first(tails[1]),
            seq_first(shifted[0]), seq_first(shifted[1]),
            ssm(st[2], Bp), ssm(st[3], Bp), ssm(st[4], Bs), ssm(st[5], Bs))
```

```python
import functools

import numpy as np
import jax
import jax.numpy as jnp
from jax import lax
from jax.experimental import pallas as pl
from jax.experimental.pallas import tpu as pltpu

F32 = jnp.float32
BF16 = jnp.bfloat16

D_MODEL = 1024
DEPTH = 4
PAST_LEN = 8192
GLA_HEADS = 4
GLA_DK = 32
GLA_DV = 64
GLA_WIDTH = GLA_HEADS * GLA_DV
GLA_RANK = 16
GLA_TAU = 16.0
GLA_CHUNK = 64
SWA_WIDTH = 512
SWA_HEADS = 8
SWA_HD = 64
DILATED_PATTERNS = ((128, 1), (512, 4), (2048, 16))
SWA_WMAX = 2048
SWA_QBLOCK = 128
ROPE_THETA = 500000.0
ROPE_DIMS = 16
SSM_WIDTH = 256
SSM_GROUP = 16
SSM_GROUPS = 16
SSM_STATE = 64
SSM_FLAT = SSM_GROUPS * SSM_STATE
NORM_EPS = 1e-6
NEG_BIG = -1e30

SAMPLE_PAD = 8
LANES = 128
VMEM_LIMIT = 48 * 1024 * 1024
MIX_DTYPE = BF16

W_COLS = (("a_qk", 256), ("a_v", 256), ("a_gate", 256), ("b_q", 512), ("b_k", 512),
          ("b_v", 512), ("b_gate", 512), ("c_u", 256), ("c_gate", 256), ("a_lr", 128))
W_OFF = {}
_o = 0
for _n, _w in W_COLS:
    W_OFF[_n] = (_o, _o + _w)
    _o += _w
W_TOTAL = _o
SWA_PIECES = ("b_q", "b_k", "b_v", "b_gate")


def _params(*sem):
    return pltpu.CompilerParams(dimension_semantics=sem, vmem_limit_bytes=VMEM_LIMIT)


def _dot(a, b):
    return jnp.dot(a.astype(BF16), b.astype(BF16), preferred_element_type=F32)


def _dot_nt(a, b):
    return lax.dot_general(a.astype(BF16), b.astype(BF16), (((1,), (1,)), ((), ())),
                           preferred_element_type=F32)


def _dot_tn(a, b):
    return lax.dot_general(a.astype(BF16), b.astype(BF16), (((0,), (0,)), ((), ())),
                           preferred_element_type=F32)


def _dot_f32(a, b):
    a_hi, b_hi = a.astype(BF16), b.astype(BF16)
    a_lo = (a - a_hi.astype(F32)).astype(BF16)
    b_lo = (b - b_hi.astype(F32)).astype(BF16)
    dot = lambda x, y: jnp.dot(x, y, preferred_element_type=F32)
    return dot(a_hi, b_hi) + (dot(a_hi, b_lo) + dot(a_lo, b_hi))


def _silu(x):
    return x * jax.nn.sigmoid(x)


def _mod_kernel(c_ref, w_ref, b_ref, o_ref):
    o_ref[0] = _dot_f32(_silu(c_ref[...]), w_ref[0]) + b_ref[0]


def _ada_mod(c_all, w_ada, b_ada):
    rows = c_all.shape[0]
    nt = 3 * D_MODEL // 1024
    return pl.pallas_call(
        _mod_kernel,
        grid=(DEPTH, nt),
        in_specs=[pl.BlockSpec((rows, D_MODEL), lambda l, n: (0, 0)),
                  pl.BlockSpec((1, D_MODEL, 1024), lambda l, n: (l, 0, n)),
                  pl.BlockSpec((1, 1, 1024), lambda l, n: (l, 0, n))],
        out_specs=pl.BlockSpec((1, rows, 1024), lambda l, n: (l, 0, n)),
        out_shape=jax.ShapeDtypeStruct((DEPTH, rows, 3 * D_MODEL), F32),
        compiler_params=_params("arbitrary", "arbitrary"),
        name="ada_mod",
    )(c_all, w_ada, b_ada.reshape(DEPTH, 1, 3 * D_MODEL))


def _rope(x, cos, sa, sb):
    outs = []
    for c in range(SWA_WIDTH // LANES):
        xc = x[:, LANES * c:LANES * (c + 1)]
        outs.append(xc * cos + pltpu.roll(xc, LANES - ROPE_DIMS // 2, 1) * sa
                    + pltpu.roll(xc, ROPE_DIMS // 2, 1) * sb)
    return jnp.concatenate(outs, axis=1)


def _proj_kernel(x_ref, sc_ref, sh_ref, g_ref, w_ref, cos_ref, sa_ref, sb_ref, *rest, emit_tail):
    n_out = len(W_COLS) + (2 if emit_tail else 0)
    outs = rest[len(rest) - n_out:]
    x = x_ref[0]
    ms = jnp.mean(x * x, axis=-1, keepdims=True)
    h = x * lax.rsqrt(ms + NORM_EPS) * g_ref[...]
    h = h * (1.0 + sc_ref[0]) + sh_ref[0]
    hb = h.astype(BF16)

    def mm(name):
        c0, c1 = W_OFF[name]
        return jnp.dot(hb, w_ref[0, :, c0:c1], preferred_element_type=F32)

    names = [n for n, _ in W_COLS]
    cos, sa, sb = cos_ref[...], sa_ref[...], sb_ref[...]
    for idx, name in enumerate(names):
        val = mm(name)
        if name in ("b_q", "b_k"):
            val = _rope(val, cos, sa, sb)
        if emit_tail and name in SWA_PIECES:
            for hp in range(SWA_WIDTH // LANES):
                outs[idx][0, hp] = val[:, hp * LANES:(hp + 1) * LANES]
        else:
            outs[idx][0] = val
        if emit_tail and name == "b_k":
            outs[len(names)][0, 0] = val.T.reshape(SWA_HEADS, SWA_HD, val.shape[0])
        if emit_tail and name == "b_v":
            outs[len(names) + 1][0, 0] = val.T.reshape(SWA_HEADS, SWA_HD, val.shape[0])


def _proj(x, scale, shift, g_pre, w_cat, rope_tabs, *, tm, layer, tail=None):
    nb, T, _ = x.shape
    nt = T // tm
    R = scale.shape[1]
    emit_tail = tail is not None
    if R == 1:
        mod_spec = pl.BlockSpec((1, 1, D_MODEL), lambda b, j: (b, 0, 0))
    else:
        mod_spec = pl.BlockSpec((1, tm, D_MODEL), lambda b, j: (b, j, 0))
    tab_spec = pl.BlockSpec((tm, LANES), lambda b, j: (j, 0))
    out_specs, out_shape = [], []
    for name, w in W_COLS:
        if emit_tail and name in SWA_PIECES:
            out_specs.append(pl.BlockSpec((1, w // LANES, tm, LANES), lambda b, j: (b, 0, j, 0)))
            out_shape.append(jax.ShapeDtypeStruct((nb, w // LANES, T, LANES), F32))
        else:
            out_specs.append(pl.BlockSpec((1, tm, w), lambda b, j: (b, j, 0)))
            out_shape.append(jax.ShapeDtypeStruct((nb, T, w), F32))
    in_specs = [pl.BlockSpec((1, tm, D_MODEL), lambda b, j: (b, j, 0)),
                mod_spec, mod_spec,
                pl.BlockSpec((1, D_MODEL), lambda b, j: (0, 0)),
                pl.BlockSpec((1, D_MODEL, W_TOTAL), lambda b, j: (layer, 0, 0)),
                tab_spec, tab_spec, tab_spec]
    args = [x, scale, shift, g_pre.reshape(1, D_MODEL), w_cat, *rope_tabs]
    aliases = {}
    if emit_tail:
        l, bufs = tail
        keep = min(SWA_WMAX, T)
        first = (T - keep) // tm
        tail_spec = pl.BlockSpec((1, 1, SWA_HEADS, SWA_HD, tm),
                                 lambda b, j: (l, b, 0, 0, jnp.maximum(j - first, 0)))
        out_specs += [tail_spec, tail_spec]
        out_shape += [jax.ShapeDtypeStruct((DEPTH, nb, SWA_HEADS, SWA_HD, keep), F32)] * 2
        if bufs is not None:
            in_specs += [pl.BlockSpec(memory_space=pl.ANY)] * 2
            aliases = {len(args): len(W_COLS), len(args) + 1: len(W_COLS) + 1}
            args += list(bufs)
    res = pl.pallas_call(
        functools.partial(_proj_kernel, emit_tail=emit_tail),
        grid=(nb, nt),
        in_specs=in_specs,
        out_specs=out_specs,
        out_shape=out_shape,
        input_output_aliases=aliases,
        compiler_params=_params("arbitrary", "arbitrary"),
        name="in_proj",
    )(*args)
    names = [n for n, _ in W_COLS]
    d = dict(zip(names, res[:len(names)]))
    if emit_tail:
        d["k_tail"], d["v_tail"] = res[len(names)], res[len(names) + 1]
    return d


def _cumsum_rows(x, C):
    n = x.shape[0]
    blk = min(n, max(C, LANES))
    ri = lax.broadcasted_iota(jnp.int32, (blk, blk), 0)
    ci = lax.broadcasted_iota(jnp.int32, (blk, blk), 1)
    tri = jnp.where((ri >= ci) & (ri // C == ci // C), 1.0, 0.0).astype(BF16)
    hi = x.astype(BF16)
    r1 = x - hi.astype(F32)
    mid = r1.astype(BF16)
    lo = (r1 - mid.astype(F32)).astype(BF16)
    out = []
    for b in range(n // blk):
        rs = slice(b * blk, (b + 1) * blk)
        dot = lambda t: jnp.dot(tri, t[rs], preferred_element_type=F32)
        out.append(dot(hi) + dot(mid) + dot(lo))
    return jnp.concatenate(out, axis=0)


def _gla_kernel(qk_ref, v_ref, gate_ref, lr_ref, wlr_ref, blr_ref, gg_ref, s0_ref,
                o_ref, sf_ref, s_scr, o_scr, *, TT, C, t_valid, T, own_state):
    j = pl.program_id(1)
    H, DK, DV = GLA_HEADS, GLA_DK, GLA_DV
    HK, HV = H * DK, H * DV
    i0 = lambda shape: lax.broadcasted_iota(jnp.int32, shape, 0)
    i1 = lambda shape: lax.broadcasted_iota(jnp.int32, shape, 1)
    own_cols = i0((H * C, HK)) // C == i1((H * C, HK)) // DK
    causal = i0((H * C, C)) % C >= i1((H * C, C))
    diag = i0((HK, HV)) // DK == i1((HK, HV)) // DV
    lane_head = i1((C, HV)) // DV

    def block_diag(s):
        return jnp.where(diag, jnp.concatenate([s.reshape(HK, DV)] * H, axis=1), 0.0)

    if not own_state:
        @pl.when(j == 0)
        def _():
            s_scr[...] = block_diag(s0_ref[0])

    z = _dot_f32(lr_ref[0], wlr_ref[...]) + blr_ref[...]
    glog = (jnp.minimum(z, 0.0) - jnp.log1p(jnp.exp(-jnp.abs(z)))) * (1.0 / GLA_TAU)
    qk = qk_ref[0]
    q = qk[:, :HK]
    k = qk[:, HK:]
    if t_valid < (C if own_state else T):
        row = j * TT + lax.broadcasted_iota(jnp.int32, (TT, HK), 0)
        row = row % C if own_state else row
        glog = jnp.where(row < t_valid, glog, 0.0)
        k = jnp.where(row < t_valid, k, 0.0)
    v = v_ref[0]
    nc = TT // C
    bc = _cumsum_rows(glog, C)
    bl = jnp.concatenate([jnp.broadcast_to(bc[(c + 1) * C - 1:(c + 1) * C], (C, HK))
                          for c in range(nc)], axis=0)
    qd = q * (DK ** -0.5) * jnp.exp(bc)
    kd = k * jnp.exp(-bc)
    kdl = k * jnp.exp(bl - bc)
    dec = jnp.exp(bl)
    def take_heads(s_bd, dst):
        for h in range(H):
            dst[h] = s_bd[h * DK:(h + 1) * DK, h * DV:(h + 1) * DV]

    s_prev = None if own_state else s_scr[...]
    for c in range(nc):
        rows = slice(c * C, (c + 1) * C)
        qc, kc, klc, vc = qd[rows], kd[rows], kdl[rows], v[rows]
        if own_state:
            s_prev = block_diag(s0_ref[c])
        qz = jnp.where(own_cols, jnp.concatenate([qc] * H, axis=0), 0.0)
        att = jnp.where(causal, _dot_nt(qz, kc), 0.0)
        res = _dot(att, vc)
        intra = res[0:C]
        for h in range(1, H):
            intra = jnp.where(lane_head == h, res[h * C:(h + 1) * C], intra)
        o_scr[rows, :] = intra + _dot(qc, s_prev)
        dec_col = jnp.broadcast_to(dec[c * C:c * C + 1], (8, HK)).T[:, 0:1]
        s_prev = dec_col * s_prev + jnp.where(diag, _dot_tn(klc, vc), 0.0)
        if own_state:
            take_heads(s_prev, sf_ref.at[c])
    if not own_state:
        s_scr[...] = s_prev
        take_heads(s_prev, sf_ref.at[0])

    o = o_scr[...]
    sq = o * o
    hi = sq.astype(BF16)
    r1 = sq - hi.astype(F32)
    mid = r1.astype(BF16)
    lo = (r1 - mid.astype(F32)).astype(BF16)
    same_head = jnp.where(i0((HV, HV)) // DV == i1((HV, HV)) // DV, 1.0, 0.0).astype(BF16)
    hsum = lambda t: jnp.dot(t, same_head, preferred_element_type=F32)
    ms = (hsum(hi) + hsum(mid) + hsum(lo)) * (1.0 / DV)
    o_ref[0] = (o * lax.rsqrt(ms + NORM_EPS) * gg_ref[...] * _silu(gate_ref[0])).astype(o_ref.dtype)


def _gla(p, w_lr, b_lr, g_gla, s0, *, TT, C, t_valid, own_state=False):
    B, T, _ = p["a_qk"].shape
    HK = GLA_HEADS * GLA_DK
    row = lambda w: pl.BlockSpec((1, TT, w), lambda b, j: (b, j, 0))
    full = lambda r, c: pl.BlockSpec((r, c), lambda b, j: (0, 0))
    if own_state:
        st_spec = pl.BlockSpec((TT // C, GLA_HEADS, GLA_DK, GLA_DV), lambda b, j: (j, 0, 0, 0))
    else:
        st_spec = pl.BlockSpec((1, GLA_HEADS, GLA_DK, GLA_DV), lambda b, j: (b, 0, 0, 0))
    return pl.pallas_call(
        functools.partial(_gla_kernel, TT=TT, C=C, t_valid=t_valid, T=T, own_state=own_state),
        grid=(B, T // TT),
        in_specs=[row(2 * HK), row(GLA_WIDTH), row(GLA_WIDTH), row(LANES),
                  full(LANES, HK), full(1, HK), full(1, GLA_WIDTH), st_spec],
        out_specs=[row(GLA_WIDTH), st_spec],
        out_shape=[jax.ShapeDtypeStruct((B, T, GLA_WIDTH), MIX_DTYPE),
                   jax.ShapeDtypeStruct(s0.shape, F32)],
        scratch_shapes=[pltpu.VMEM((HK, GLA_WIDTH), F32), pltpu.VMEM((TT, GLA_WIDTH), F32)],
        compiler_params=_params("arbitrary", "arbitrary"),
        name="gla_mix",
    )(p["a_qk"], p["a_v"], p["a_gate"], p["a_lr"], w_lr, b_lr.reshape(1, HK),
      g_gla.reshape(1, GLA_WIDTH), s0)


def _swa_kernel(q_ref, k_ref, v_ref, g_ref, o_ref, qd_scr, kd_scr, vd_scr, op_scr, lse_scr,
                stage_scr, *, T):
    QB = SWA_QBLOCK
    CP = 256
    (_, d0), (_, d1), (_, d2) = DILATED_PATTERNS
    assert d0 == 1 and d2 % d1 == 0
    dd = d2 // d1

    def regroup(src_ref, dst_scr, scale):
        cast = lambda x: (x * scale).astype(BF16)
        dst = lambda j: pl.ds(pl.multiple_of(j * CP, CP), CP)

        def copy0(j, carry):
            dst_scr[0, dst(j), :] = cast(src_ref[0, 0, dst(j), :])
            return carry

        def copy1(j, carry):
            per = T // d1 // CP
            x = src_ref[0, 0, pl.ds(j // per + d1 * CP * (j % per), CP, stride=d1), :]
            stage_scr[dst(j), :] = x
            dst_scr[1, dst(j), :] = cast(x)
            return carry

        def copy2(j, carry):
            per = T // d2 // CP
            r = j // per
            start = (r % d1) * (T // d1) + r // d1 + dd * CP * (j % per)
            dst_scr[2, dst(j), :] = cast(stage_scr[pl.ds(start, CP, stride=dd), :])
            return carry

        for body in (copy0, copy1, copy2):
            lax.fori_loop(0, T // CP, body, 0)

    regroup(q_ref, qd_scr, SWA_HD ** -0.5)
    regroup(k_ref, kd_scr, 1.0)
    regroup(v_ref, vd_scr, 1.0)

    ri = lax.broadcasted_iota(jnp.int32, (2 * QB, 2 * QB), 0) % QB
    ci = lax.broadcasted_iota(jnp.int32, (2 * QB, 2 * QB), 1)
    band = (ci >= ri) & (ci <= ri + QB)
    bias_band = jnp.where(band, 0.0, NEG_BIG)
    bias_first = jnp.where(band & (ci >= QB), 0.0, NEG_BIG)
    lo_q = lax.broadcasted_iota(jnp.int32, (QB, LANES), 1) < SWA_HD
    ones = jnp.ones((2 * QB, LANES), BF16)

    def step(i, carry):
        cur = pl.ds(pl.multiple_of(i * QB, QB), QB)
        prev = pl.ds(pl.multiple_of(jnp.maximum(i - 1, 0) * QB, QB), QB)
        for p, (_, d) in enumerate(DILATED_PATTERNS):
            nblk = T // d // QB
            r = i // nblk
            n = i % nblk
            q = qd_scr[p, cur, :]
            qz = jnp.concatenate([jnp.where(lo_q, q, 0), jnp.where(lo_q, 0, q)], axis=0)
            k = jnp.concatenate([kd_scr[p, prev, :], kd_scr[p, cur, :]], axis=0)
            v = jnp.concatenate([vd_scr[p, prev, :], vd_scr[p, cur, :]], axis=0)
            s = lax.dot_general(qz, k, (((1,), (1,)), ((), ())), preferred_element_type=F32)
            s = s + jnp.where(n > 0, bias_band, bias_first)
            m = jnp.max(s, axis=-1, keepdims=True)
            e = jnp.exp(s - m).astype(BF16)
            ox = jnp.dot(e, jnp.concatenate([v, ones], axis=1), preferred_element_type=F32)
            num = jnp.where(lo_q, ox[:QB, :LANES], ox[QB:, :LANES])
            den = jnp.where(lo_q, ox[:QB, LANES:], ox[QB:, LANES:])
            mx = jnp.where(lo_q, jnp.broadcast_to(m[:QB], (QB, LANES)),
                           jnp.broadcast_to(m[QB:], (QB, LANES)))
            start = r + d * QB * n
            dst = pl.ds(start, QB) if d == 1 else pl.ds(start, QB, stride=d)
            op_scr[p, dst, :] = num / den
            lse_scr[p, dst, :] = mx + jnp.log(den)
        return carry

    lax.fori_loop(0, T // QB, step, 0, unroll=4)

    CH = 512

    def combine(i, carry):
        rs = pl.ds(pl.multiple_of(i * CH, CH), CH)
        l0, l1, l2 = lse_scr[0, rs, :], lse_scr[1, rs, :], lse_scr[2, rs, :]
        mx = jnp.maximum(jnp.maximum(l0, l1), l2)
        w0, w1, w2 = jnp.exp(l0 - mx), jnp.exp(l1 - mx), jnp.exp(l2 - mx)
        mixed = (w0 * op_scr[0, rs, :] + w1 * op_scr[1, rs, :] + w2 * op_scr[2, rs, :]) / (w0 + w1 + w2)
        o_ref[0, 0, rs, :] = (mixed * _silu(g_ref[0, 0, rs, :])).astype(o_ref.dtype)
        return carry

    lax.fori_loop(0, T // CH, combine, 0)


def _swa_prompt(p):
    B, NP, T, _ = p["b_q"].shape
    spec = pl.BlockSpec((1, 1, T, LANES), lambda b, hp: (b, hp, 0, 0))
    return pl.pallas_call(
        functools.partial(_swa_kernel, T=T),
        grid=(B, NP),
        in_specs=[spec] * 4,
        out_specs=spec,
        out_shape=jax.ShapeDtypeStruct((B, NP, T, LANES), MIX_DTYPE),
        scratch_shapes=[pltpu.VMEM((len(DILATED_PATTERNS), T, LANES), BF16)] * 3
        + [pltpu.VMEM((len(DILATED_PATTERNS), T, LANES), F32)] * 2
        + [pltpu.VMEM((T, LANES), F32)],
        compiler_params=_params("arbitrary", "arbitrary"),
        name="swa_prompt",
    )(p["b_q"], p["b_k"], p["b_v"], p["b_gate"])


def _swa_dec_kernel(ck_ref, cv_ref, q_ref, k_ref, v_ref, g_ref, mult_ref, multf_ref, *rest, L, Tn):
    o_ref, ko_ref, vo_ref = rest[-3:]
    HG = ck_ref.shape[2]
    lane = lax.broadcasted_iota(jnp.int32, (HG, SWA_HD, LANES), 2)

    def shifted(c_ref, new_rows):
        r = pltpu.roll(c_ref[0, 0], L - Tn, 2)
        moved = pltpu.roll(new_rows, SAMPLE_PAD - Tn, 0)
        pad = jnp.concatenate([jnp.zeros((LANES - SAMPLE_PAD, HG * SWA_HD), F32), moved], axis=0)
        tile = pad.T.reshape(HG, SWA_HD, LANES)
        last = jnp.where(lane >= LANES - Tn, tile, r[:, :, L - LANES:])
        return jnp.concatenate([r[:, :, :L - LANES], last], axis=2)

    kout = shifted(ck_ref, k_ref[0])
    vout = shifted(cv_ref, v_ref[0])
    ko_ref[0, 0] = kout
    vo_ref[0, 0] = vout

    q = q_ref[0] * (SWA_HD ** -0.5)
    q3 = jnp.stack([q[:, h * SWA_HD:(h + 1) * SWA_HD] for h in range(HG)], axis=0).astype(BF16)
    qk = lambda kk: jnp.einsum('hid,hdr->hir', q3, kk.astype(BF16), preferred_element_type=F32)
    pv = lambda pp, vv: jnp.einsum('hir,hdr->hid', pp.astype(BF16), vv.astype(BF16),
                                   preferred_element_type=F32)
    mult = mult_ref[...][None]
    multf = multf_ref[...][None]
    s = jnp.where(mult > 0, qk(kout), NEG_BIG)
    sf = jnp.where(multf > 0, qk(ck_ref[0, 0, :, :, 0:LANES]), NEG_BIG)
    m = jnp.maximum(jnp.max(s, axis=-1, keepdims=True), jnp.max(sf, axis=-1, keepdims=True))
    pe = jnp.exp(s - m) * mult
    pf = jnp.exp(sf - m) * multf
    den = jnp.sum(pe, axis=-1, keepdims=True) + jnp.sum(pf, axis=-1, keepdims=True)
    den = jnp.where(den > 0, den, 1.0)
    o3 = (pv(pe, vout) + pv(pf, cv_ref[0, 0, :, :, 0:LANES])) / den
    o = jnp.concatenate([o3[h] for h in range(HG)], axis=1)
    o_ref[0] = (o * _silu(g_ref[0])).astype(o_ref.dtype)


def _mult_tables(L, Tn):
    assert L == SWA_WMAX
    qpos = SWA_WMAX + np.arange(SAMPLE_PAD)[:, None]

    def count(kpos):
        off = qpos - kpos[None, :]
        cnt = np.zeros(off.shape, np.float32)
        for w, d in DILATED_PATTERNS:
            cnt += ((off >= 0) & (off <= w) & (off % d == 0)).astype(np.float32)
        cnt[Tn:] = 0.0
        return cnt

    mult = count(np.arange(L) + Tn)
    multf = count(np.arange(LANES))
    multf[:, Tn:] = 0.0
    return jnp.asarray(mult), jnp.asarray(multf)


def _swa_sample(p, cache_k, cache_v, l, Tn, bufs):
    B = p["b_q"].shape[0]
    L = cache_k.shape[-1]
    HG = 8
    mult, multf = _mult_tables(L, Tn)
    cspec = pl.BlockSpec((1, 1, HG, SWA_HD, L), lambda b, g: (l, b, g, 0, 0))
    nspec = pl.BlockSpec((1, SAMPLE_PAD, HG * SWA_HD), lambda b, g: (b, 0, g))
    const = lambda a: pl.BlockSpec(a.shape, lambda b, g: (0, 0))
    stacked = jax.ShapeDtypeStruct(cache_k.shape, F32)
    in_specs = [cspec, cspec, nspec, nspec, nspec, nspec, const(mult), const(multf)]
    args = [cache_k, cache_v, p["b_q"], p["b_k"], p["b_v"], p["b_gate"], mult, multf]
    aliases = {}
    if bufs is not None:
        in_specs += [pl.BlockSpec(memory_space=pl.ANY)] * 2
        aliases = {len(args): 1, len(args) + 1: 2}
        args += list(bufs)
    return pl.pallas_call(
        functools.partial(_swa_dec_kernel, L=L, Tn=Tn),
        grid=(B, SWA_HEADS // HG),
        in_specs=in_specs,
        out_specs=[nspec, cspec, cspec],
        out_shape=[jax.ShapeDtypeStruct((B, SAMPLE_PAD, SWA_WIDTH), MIX_DTYPE), stacked, stacked],
        input_output_aliases=aliases,
        compiler_params=_params("arbitrary", "arbitrary"),
        name="swa_sample",
    )(*args)


def _s5prep_kernel(lr_ref, li_ref, ldt_ref, bre_ref, bim_ref, bcat_ref, tab_ref):
    lr = lr_ref[0]
    li = li_ref[0]
    dt = jnp.exp(ldt_ref[0])
    mag = jnp.exp(lr * dt)
    br = mag * jnp.cos(li * dt)
    bi = mag * jnp.sin(li * dt)
    den = lr * lr + li * li
    nr = br - 1.0
    cr = (nr * lr + bi * li) / den
    ci = (bi * lr - nr * li) / den
    bre = bre_ref[0]
    bim = bim_ref[0]
    bcat_ref[0, :, 0:SSM_FLAT] = (cr * bre - ci * bim).astype(BF16)
    bcat_ref[0, :, SSM_FLAT:2 * SSM_FLAT] = (cr * bim + ci * bre).astype(BF16)
    tab_ref[0, 0] = jnp.broadcast_to(br, (8, SSM_FLAT))
    tab_ref[0, 1] = jnp.broadcast_to(bi, (8, SSM_FLAT))


def _s5_prep(lam_re, lam_im, log_dt, b_re, b_im):
    eye = jnp.eye(SSM_GROUPS, dtype=F32)

    def block_diag_b(b):
        bt = jnp.transpose(b, (0, 1, 3, 2))
        return (bt[:, :, :, None, :] * eye[None, :, None, :, None]).reshape(DEPTH, SSM_WIDTH, SSM_FLAT)

    flat = lambda a: a.reshape(DEPTH, 1, SSM_FLAT)
    ldt = jnp.broadcast_to(log_dt[:, :, None], (DEPTH, SSM_GROUPS, SSM_STATE))
    vec = pl.BlockSpec((1, 1, SSM_FLAT), lambda l: (l, 0, 0))
    mat = pl.BlockSpec((1, SSM_WIDTH, SSM_FLAT), lambda l: (l, 0, 0))
    return pl.pallas_call(
        _s5prep_kernel,
        grid=(DEPTH,),
        in_specs=[vec, vec, vec, mat, mat],
        out_specs=[pl.BlockSpec((1, SSM_WIDTH, 2 * SSM_FLAT), lambda l: (l, 0, 0)),
                   pl.BlockSpec((1, 2, 8, SSM_FLAT), lambda l: (l, 0, 0, 0))],
        out_shape=[jax.ShapeDtypeStruct((DEPTH, SSM_WIDTH, 2 * SSM_FLAT), BF16),
                   jax.ShapeDtypeStruct((DEPTH, 2, 8, SSM_FLAT), F32)],
        compiler_params=_params("arbitrary"),
        name="s5_prep",
    )(flat(lam_re), flat(lam_im), flat(ldt), block_diag_b(b_re), block_diag_b(b_im))


def _s5_kernel(u_ref, gate_ref, x0r_ref, x0i_ref, bcat_ref, cre_ref, cim_ref, tab_ref, d_ref,
               wglu_ref, bglu_ref, o_ref, sr_ref, si_ref, xr_scr, xi_scr, cr_scr, ci_scr,
               *, TT, B, last_tile, last_t):
    j = pl.program_id(0)
    NBG, NL = xr_scr.shape[0], xr_scr.shape[1]
    split = lambda a: jnp.stack([a[:, LANES * i:LANES * (i + 1)] for i in range(NL)], axis=0)
    merge = lambda a: jnp.concatenate([a[i] for i in range(NL)], axis=1)

    @pl.when(j == 0)
    def _():
        xr_scr[...] = jnp.zeros(xr_scr.shape, F32)
        xi_scr[...] = jnp.zeros(xi_scr.shape, F32)
        for g in range(NBG):
            cr_scr[g] = split(x0r_ref[g])
            ci_scr[g] = split(x0i_ref[g])

    u = u_ref[...].reshape(B * TT, SSM_WIDTH)
    bu = _dot(u, bcat_ref[...])
    for b in range(B):
        g, s = divmod(b, 8)
        rows = pl.ds(s, TT, stride=8)
        for i in range(NL):
            xr_scr[g, i, rows, :] = bu[b * TT:(b + 1) * TT, LANES * i:LANES * (i + 1)]
            xi_scr[g, i, rows, :] = bu[b * TT:(b + 1) * TT, SSM_FLAT + LANES * i:SSM_FLAT + LANES * (i + 1)]

    lam_r = split(tab_ref[0])
    lam_i = split(tab_ref[1])

    def step(t, carry):
        rows = pl.ds(pl.multiple_of(t * 8, 8), 8)
        out = []
        for g in range(NBG):
            cr, ci = carry[2 * g], carry[2 * g + 1]
            nr = lam_r * cr - lam_i * ci + xr_scr[g, :, rows, :]
            ni = lam_r * ci + lam_i * cr + xi_scr[g, :, rows, :]
            xr_scr[g, :, rows, :] = nr
            xi_scr[g, :, rows, :] = ni
            out += [nr, ni]
        return tuple(out)

    init = tuple(ref[g] for g in range(NBG) for ref in (cr_scr, ci_scr))
    fin = lax.fori_loop(0, TT, step, init, unroll=min(TT, 8))
    for g in range(NBG):
        cr_scr[g] = fin[2 * g]
        ci_scr[g] = fin[2 * g + 1]

    @pl.when(j == last_tile)
    def _():
        rows = slice(last_t * 8, last_t * 8 + 8)
        for g in range(NBG):
            sr_ref[g] = merge(xr_scr[g, :, rows, :])
            si_ref[g] = merge(xi_scr[g, :, rows, :])

    def gather(scr):
        per_b = []
        for b in range(B):
            g, s = divmod(b, 8)
            rows = pl.ds(s, TT, stride=8)
            per_b.append(jnp.concatenate([scr[g, i, rows, :] for i in range(NL)], axis=1))
        return jnp.concatenate(per_b, axis=0)

    y = _dot(gather(xr_scr), cre_ref[...]) - _dot(gather(xi_scr), cim_ref[...]) + d_ref[...] * u
    z = jax.nn.gelu(y)
    out = z * jax.nn.sigmoid(_dot(z, wglu_ref[...]) + bglu_ref[...])
    out = out * _silu(gate_ref[...].reshape(B * TT, SSM_WIDTH))
    o_ref[...] = out.reshape(B, TT, SSM_WIDTH).astype(o_ref.dtype)


def _s5(p, x0r, x0i, bcat, cre, cim, tab, d_skip, w_glu, b_glu, *, TT, t_valid):
    B, T, _ = p["c_u"].shape
    NBG = x0r.shape[0]
    NL = SSM_FLAT // LANES
    last = t_valid - 1
    row = pl.BlockSpec((B, TT, SSM_WIDTH), lambda j: (0, j, 0))
    st = pl.BlockSpec((NBG, 8, SSM_FLAT), lambda j: (0, 0, 0))
    full = lambda a: pl.BlockSpec(a.shape, lambda j: (0,) * a.ndim)
    d2 = d_skip.reshape(1, SSM_WIDTH)
    bg2 = b_glu.reshape(1, SSM_WIDTH)
    state = jax.ShapeDtypeStruct((NBG, 8, SSM_FLAT), F32)
    return pl.pallas_call(
        functools.partial(_s5_kernel, TT=TT, B=B, last_tile=last // TT, last_t=last % TT),
        grid=(T // TT,),
        in_specs=[row, row, st, st, full(bcat), full(cre), full(cim), full(tab), full(d2),
                  full(w_glu), full(bg2)],
        out_specs=[row, st, st],
        out_shape=[jax.ShapeDtypeStruct((B, T, SSM_WIDTH), MIX_DTYPE), state, state],
        scratch_shapes=[pltpu.VMEM((NBG, NL, TT * 8, LANES), F32)] * 2
        + [pltpu.VMEM((NBG, NL, 8, LANES), F32)] * 2,
        compiler_params=_params("arbitrary"),
        name="s5_mix",
    )(p["c_u"], p["c_gate"], x0r, x0i, bcat, cre, cim, tab, d2, w_glu, bg2)


def _out_kernel(oa_ref, ob_ref, oc_ref, x_ref, gate_ref, g_ref, w_ref, o_ref):
    a1 = GLA_WIDTH
    a2 = GLA_WIDTH + SWA_WIDTH
    if len(ob_ref.shape) == 4:
        ob = jnp.concatenate([ob_ref[0, hp] for hp in range(ob_ref.shape[1])], axis=1)
    else:
        ob = ob_ref[0]
    y = (jnp.dot(oa_ref[0].astype(BF16), w_ref[0, 0:a1, :], preferred_element_type=F32)
         + jnp.dot(ob.astype(BF16), w_ref[0, a1:a2, :], preferred_element_type=F32)
         + jnp.dot(oc_ref[0].astype(BF16), w_ref[0, a2:, :], preferred_element_type=F32))
    ms = jnp.mean(y * y, axis=-1, keepdims=True)
    yn = y * lax.rsqrt(ms + NORM_EPS) * g_ref[...]
    o_ref[0] = x_ref[0] + gate_ref[0] * yn


def _out_proj(oa, ob, oc, x, gate, g_post, w_out, *, tm, layer):
    nb, T, _ = x.shape
    R = gate.shape[1]
    row = lambda w: pl.BlockSpec((1, tm, w), lambda b, j: (b, j, 0))
    if R == 1:
        gate_spec = pl.BlockSpec((1, 1, D_MODEL), lambda b, j: (b, 0, 0))
    else:
        gate_spec = row(D_MODEL)
    if ob.ndim == 4:
        ob_spec = pl.BlockSpec((1, ob.shape[1], tm, LANES), lambda b, j: (b, 0, j, 0))
    else:
        ob_spec = row(SWA_WIDTH)
    return pl.pallas_call(
        _out_kernel,
        grid=(nb, T // tm),
        in_specs=[row(GLA_WIDTH), ob_spec, row(SSM_WIDTH), row(D_MODEL), gate_spec,
                  pl.BlockSpec((1, D_MODEL), lambda b, j: (0, 0)),
                  pl.BlockSpec((1, D_MODEL, D_MODEL), lambda b, j: (layer, 0, 0))],
        out_specs=row(D_MODEL),
        out_shape=jax.ShapeDtypeStruct((nb, T, D_MODEL), F32),
        compiler_params=_params("arbitrary", "arbitrary"),
        name="out_proj",
    )(oa, ob, oc, x, gate, g_post.reshape(1, D_MODEL), w_out)


def _rope_tables(pos):
    half = ROPE_DIMS // 2
    inv = ROPE_THETA ** (-jnp.arange(half, dtype=F32) * (2.0 / ROPE_DIMS))
    ang = pos.astype(F32)[:, None] * inv[None, :]
    cos, sin = jnp.cos(ang), jnp.sin(ang)
    n = pos.shape[0]
    rest = SWA_HD - ROPE_DIMS
    head = lambda a, b, fill: jnp.concatenate([a, b, jnp.full((n, rest), fill, F32)], axis=1)
    zero = jnp.zeros((n, half), F32)
    reps = LANES // SWA_HD
    return (jnp.tile(head(cos, cos, 1.0), (1, reps)),
            jnp.tile(head(-sin, zero, 0.0), (1, reps)),
            jnp.tile(head(zero, sin, 0.0), (1, reps)))


def _rearrange_w_in(w_in):
    GK = GLA_HEADS * GLA_DK
    o = np.cumsum([0, GK, GK, GLA_WIDTH, GLA_RANK, GLA_WIDTH, SWA_WIDTH, SWA_WIDTH, SWA_WIDTH,
                   SWA_WIDTH, SSM_WIDTH, SSM_WIDTH])
    piece = lambda i: w_in[:, :, o[i]:o[i + 1]]
    lr = jnp.pad(piece(3), ((0, 0), (0, 0), (0, LANES - GLA_RANK)))
    cat = jnp.concatenate([piece(0), piece(1), piece(2), piece(4), piece(5), piece(6), piece(7),
                           piece(8), piece(9), piece(10), lr], axis=-1)
    return cat.astype(BF16)


def kernel(x_prompt, x_sample, c_prompt, c_sample, state_gla, cache_swa_k, cache_swa_v, state_ssm_re, state_ssm_im, w_ada, b_ada, g_pre, g_post, w_in, w_gla_lr, b_gla_lr, g_gla, ssm_lambda_re, ssm_lambda_im, ssm_log_dt, ssm_b_re, ssm_b_im, ssm_c_re, ssm_c_im, ssm_d, w_glu, b_glu, w_out):
    Bp, Tp, _ = x_prompt.shape
    Bs, Ts, _ = x_sample.shape
    PAD = SAMPLE_PAD
    Lc = cache_swa_k.shape[2]

    w_cat = _rearrange_w_in(w_in)
    w_out_b = w_out.astype(BF16)
    w_glu_b = w_glu.astype(BF16)
    w_lr_pad = jnp.pad(w_gla_lr, ((0, 0), (0, LANES - GLA_RANK), (0, 0)))
    eye = jnp.eye(SSM_GROUPS, dtype=F32)

    def block_diag_c(c):
        ct = jnp.transpose(c, (0, 1, 3, 2))
        return (ct[:, :, :, None, :] * eye[None, :, None, :, None]).reshape(DEPTH, SSM_FLAT, SSM_WIDTH)

    cre = block_diag_c(ssm_c_re).astype(BF16)
    cim = block_diag_c(ssm_c_im).astype(BF16)
    bcat, tab = _s5_prep(ssm_lambda_re, ssm_lambda_im, ssm_log_dt, ssm_b_re, ssm_b_im)

    n_c = Bp + Bs
    c_rows = -(-n_c // 8) * 8
    c_all = jnp.pad(jnp.concatenate([c_prompt, c_sample], axis=0), ((0, c_rows - n_c), (0, 0)))
    mod = _ada_mod(c_all, w_ada, b_ada)

    tabs_p = _rope_tables(jnp.arange(Tp))
    tabs_s = _rope_tables(PAST_LEN + jnp.arange(Bs * PAD) % PAD)

    xp = x_prompt
    xs = jnp.pad(x_sample, ((0, 0), (0, PAD - Ts), (0, 0))).reshape(1, Bs * PAD, D_MODEL)
    seq_last = lambda a: jnp.transpose(a, (0, 1, 3, 4, 2))
    seq_first = lambda a: jnp.transpose(a, (0, 1, 4, 2, 3))
    cache_k = seq_last(cache_swa_k)
    cache_v = seq_last(cache_swa_v)
    zero_gla = jnp.zeros((Bp, GLA_HEADS, GLA_DK, GLA_DV), F32)
    groups = lambda n: -(-n // 8)
    ssm_in = lambda a: jnp.pad(a.reshape(a.shape[0], SSM_FLAT),
                               ((0, groups(a.shape[0]) * 8 - a.shape[0]), (0, 0))
                               ).reshape(groups(a.shape[0]), 8, SSM_FLAT)
    zero_ssm = jnp.zeros((groups(Bp), 8, SSM_FLAT), F32)

    acc = [[] for _ in range(6)]
    tails = None
    shifted = None
    for l in range(DEPTH):
        def chunk(rows, k):
            return mod[l, rows, k * D_MODEL:(k + 1) * D_MODEL]

        pr = slice(0, Bp)
        shift, scale, gate = (chunk(pr, k).reshape(Bp, 1, D_MODEL) for k in range(3))
        p = _proj(xp, scale, shift, g_pre[l], w_cat, tabs_p, tm=512, layer=l, tail=(l, tails))
        tails = (p["k_tail"], p["v_tail"])
        oa, gla_p = _gla(p, w_lr_pad[l], b_gla_lr[l], g_gla[l], zero_gla, TT=1024, C=GLA_CHUNK, t_valid=Tp)
        ob = _swa_prompt(p)
        oc, re_p, im_p = _s5(p, zero_ssm, zero_ssm, bcat[l], cre[l], cim[l], tab[l], ssm_d[l],
                             w_glu_b[l], b_glu[l], TT=128, t_valid=Tp)
        xp = _out_proj(oa, ob, oc, xp, gate, g_post[l], w_out_b, tm=512, layer=l)

        sr = slice(Bp, Bp + Bs)
        shift, scale, gate = (jnp.repeat(chunk(sr, k), PAD, axis=0).reshape(1, Bs * PAD, D_MODEL)
                              for k in range(3))
        qf = _proj(xs, scale, shift, g_pre[l], w_cat, tabs_s, tm=Bs * PAD, layer=l)
        q = {n: a.reshape(Bs, PAD, a.shape[-1]) for n, a in qf.items()}
        oa, gla_s = _gla(qf, w_lr_pad[l], b_gla_lr[l], g_gla[l], state_gla[l], TT=Bs * PAD, C=PAD,
                         t_valid=Ts, own_state=True)
        ob, k_s, v_s = _swa_sample(q, cache_k, cache_v, l, Ts, shifted)
        shifted = (k_s, v_s)
        oc, re_s, im_s = _s5(q, ssm_in(state_ssm_re[l]), ssm_in(state_ssm_im[l]), bcat[l], cre[l],
                             cim[l], tab[l], ssm_d[l], w_glu_b[l], b_glu[l], TT=PAD, t_valid=Ts)
        flat = lambda a: a.reshape(1, Bs * PAD, a.shape[-1])
        xs = _out_proj(flat(oa), flat(ob), flat(oc), xs, gate, g_post[l], w_out_b, tm=Bs * PAD, layer=l)

        for i, a in enumerate((gla_p, gla_s, re_p, im_p, re_s, im_s)):
            acc[i].append(a)

    st = [jnp.stack(a) for a in acc]
    ssm = lambda a, n: a.reshape(DEPTH, -1, SSM_GROUPS, SSM_STATE)[:, :n]
    y_sample = xs.reshape(Bs, PAD, D_MODEL)[:, :Ts]
    return (xp, y_sample, st[0], st[1], seq_first(tails[0]), seq_first(tails[1]),
            seq_first(shifted[0]), seq_first(shifted[1]),
            ssm(st[2], Bp), ssm(st[3], Bp), ssm(st[4], Bs), ssm(st[5], Bs))
```

```python
import functools

import numpy as np
import jax
import jax.numpy as jnp
from jax import lax
from jax.experimental import pallas as pl
from jax.experimental.pallas import tpu as pltpu

F32 = jnp.float32
BF16 = jnp.bfloat16

D_MODEL = 1024
DEPTH = 4
PAST_LEN = 8192
GLA_HEADS = 4
GLA_DK = 32
GLA_DV = 64
GLA_WIDTH = GLA_HEADS * GLA_DV
GLA_RANK = 16
GLA_TAU = 16.0
GLA_CHUNK = 64
SWA_WIDTH = 512
SWA_HEADS = 8
SWA_HD = 64
DILATED_PATTERNS = ((128, 1), (512, 4), (2048, 16))
SWA_WMAX = 2048
SWA_QBLOCK = 128
ROPE_THETA = 500000.0
ROPE_DIMS = 16
SSM_WIDTH = 256
SSM_GROUP = 16
SSM_GROUPS = 16
SSM_STATE = 64
SSM_FLAT = SSM_GROUPS * SSM_STATE
NORM_EPS = 1e-6
NEG_BIG = -1e30

SAMPLE_PAD = 8
LANES = 128
VMEM_LIMIT = 48 * 1024 * 1024
MIX_DTYPE = BF16

W_COLS = (("a_qk", 256), ("a_v", 256), ("a_gate", 256), ("b_q", 512), ("b_k", 512),
          ("b_v", 512), ("b_gate", 512), ("c_u", 256), ("c_gate", 256), ("a_lr", 128))
W_OFF = {}
_o = 0
for _n, _w in W_COLS:
    W_OFF[_n] = (_o, _o + _w)
    _o += _w
W_TOTAL = _o
SWA_PIECES = ("b_q", "b_k", "b_v", "b_gate")


def _params(*sem):
    return pltpu.CompilerParams(dimension_semantics=sem, vmem_limit_bytes=VMEM_LIMIT)


def _dot(a, b):
    return jnp.dot(a.astype(BF16), b.astype(BF16), preferred_element_type=F32)


def _dot_nt(a, b):
    return lax.dot_general(a.astype(BF16), b.astype(BF16), (((1,), (1,)), ((), ())),
                           preferred_element_type=F32)


def _dot_tn(a, b):
    return lax.dot_general(a.astype(BF16), b.astype(BF16), (((0,), (0,)), ((), ())),
                           preferred_element_type=F32)


def _dot_f32(a, b):
    a_hi, b_hi = a.astype(BF16), b.astype(BF16)
    a_lo = (a - a_hi.astype(F32)).astype(BF16)
    b_lo = (b - b_hi.astype(F32)).astype(BF16)
    dot = lambda x, y: jnp.dot(x, y, preferred_element_type=F32)
    return dot(a_hi, b_hi) + (dot(a_hi, b_lo) + dot(a_lo, b_hi))


def _silu(x):
    return x * jax.nn.sigmoid(x)


def _mod_kernel(c_ref, w_ref, b_ref, o_ref):
    o_ref[0] = _dot_f32(_silu(c_ref[...]), w_ref[0]) + b_ref[0]


def _ada_mod(c_all, w_ada, b_ada):
    rows = c_all.shape[0]
    nt = 3 * D_MODEL // 1024
    return pl.pallas_call(
        _mod_kernel,
        grid=(DEPTH, nt),
        in_specs=[pl.BlockSpec((rows, D_MODEL), lambda l, n: (0, 0)),
                  pl.BlockSpec((1, D_MODEL, 1024), lambda l, n: (l, 0, n)),
                  pl.BlockSpec((1, 1, 1024), lambda l, n: (l, 0, n))],
        out_specs=pl.BlockSpec((1, rows, 1024), lambda l, n: (l, 0, n)),
        out_shape=jax.ShapeDtypeStruct((DEPTH, rows, 3 * D_MODEL), F32),
        compiler_params=_params("arbitrary", "arbitrary"),
        name="ada_mod",
    )(c_all, w_ada, b_ada.reshape(DEPTH, 1, 3 * D_MODEL))


def _rope(x, cos, sa, sb):
    outs = []
    for c in range(SWA_WIDTH // LANES):
        xc = x[:, LANES * c:LANES * (c + 1)]
        outs.append(xc * cos + pltpu.roll(xc, LANES - ROPE_DIMS // 2, 1) * sa
                    + pltpu.roll(xc, ROPE_DIMS // 2, 1) * sb)
    return jnp.concatenate(outs, axis=1)


def _proj_kernel(x_ref, sc_ref, sh_ref, g_ref, w_ref, cos_ref, sa_ref, sb_ref, *rest, emit_tail):
    n_out = len(W_COLS) + (2 if emit_tail else 0)
    outs = rest[len(rest) - n_out:]
    x = x_ref[0]
    ms = jnp.mean(x * x, axis=-1, keepdims=True)
    h = x * lax.rsqrt(ms + NORM_EPS) * g_ref[...]
    h = h * (1.0 + sc_ref[0]) + sh_ref[0]
    hb = h.astype(BF16)

    def mm(name):
        c0, c1 = W_OFF[name]
        return jnp.dot(hb, w_ref[0, :, c0:c1], preferred_element_type=F32)

    names = [n for n, _ in W_COLS]
    cos, sa, sb = cos_ref[...], sa_ref[...], sb_ref[...]
    for idx, name in enumerate(names):
        val = mm(name)
        if name in ("b_q", "b_k"):
            val = _rope(val, cos, sa, sb)
        if emit_tail and name in SWA_PIECES:
            for hp in range(SWA_WIDTH // LANES):
                outs[idx][0, hp] = val[:, hp * LANES:(hp + 1) * LANES]
        else:
            outs[idx][0] = val
        if emit_tail and name == "b_k":
            outs[len(names)][0, 0] = val.T.reshape(SWA_HEADS, SWA_HD, val.shape[0])
        if emit_tail and name == "b_v":
            outs[len(names) + 1][0, 0] = val.T.reshape(SWA_HEADS, SWA_HD, val.shape[0])


def _proj(x, scale, shift, g_pre, w_cat, rope_tabs, *, tm, layer, tail=None):
    nb, T, _ = x.shape
    nt = T // tm
    R = scale.shape[1]
    emit_tail = tail is not None
    if R == 1:
        mod_spec = pl.BlockSpec((1, 1, D_MODEL), lambda b, j: (b, 0, 0))
    else:
        mod_spec = pl.BlockSpec((1, tm, D_MODEL), lambda b, j: (b, j, 0))
    tab_spec = pl.BlockSpec((tm, LANES), lambda b, j: (j, 0))
    out_specs, out_shape = [], []
    for name, w in W_COLS:
        if emit_tail and name in SWA_PIECES:
            out_specs.append(pl.BlockSpec((1, w // LANES, tm, LANES), lambda b, j: (b, 0, j, 0)))
            out_shape.append(jax.ShapeDtypeStruct((nb, w // LANES, T, LANES), F32))
        else:
            out_specs.append(pl.BlockSpec((1, tm, w), lambda b, j: (b, j, 0)))
            out_shape.append(jax.ShapeDtypeStruct((nb, T, w), F32))
    in_specs = [pl.BlockSpec((1, tm, D_MODEL), lambda b, j: (b, j, 0)),
                mod_spec, mod_spec,
                pl.BlockSpec((1, D_MODEL), lambda b, j: (0, 0)),
                pl.BlockSpec((1, D_MODEL, W_TOTAL), lambda b, j: (layer, 0, 0)),
                tab_spec, tab_spec, tab_spec]
    args = [x, scale, shift, g_pre.reshape(1, D_MODEL), w_cat, *rope_tabs]
    aliases = {}
    if emit_tail:
        l, bufs = tail
        keep = min(SWA_WMAX, T)
        first = (T - keep) // tm
        tail_spec = pl.BlockSpec((1, 1, SWA_HEADS, SWA_HD, tm),
                                 lambda b, j: (l, b, 0, 0, jnp.maximum(j - first, 0)))
        out_specs += [tail_spec, tail_spec]
        out_shape += [jax.ShapeDtypeStruct((DEPTH, nb, SWA_HEADS, SWA_HD, keep), F32)] * 2
        if bufs is not None:
            in_specs += [pl.BlockSpec(memory_space=pl.ANY)] * 2
            aliases = {len(args): len(W_COLS), len(args) + 1: len(W_COLS) + 1}
            args += list(bufs)
    res = pl.pallas_call(
        functools.partial(_proj_kernel, emit_tail=emit_tail),
        grid=(nb, nt),
        in_specs=in_specs,
        out_specs=out_specs,
        out_shape=out_shape,
        input_output_aliases=aliases,
        compiler_params=_params("arbitrary", "arbitrary"),
        name="in_proj",
    )(*args)
    names = [n for n, _ in W_COLS]
    d = dict(zip(names, res[:len(names)]))
    if emit_tail:
        d["k_tail"], d["v_tail"] = res[len(names)], res[len(names) + 1]
    return d


def _cumsum_rows(x, C):
    n = x.shape[0]
    blk = min(n, max(C, LANES))
    ri = lax.broadcasted_iota(jnp.int32, (blk, blk), 0)
    ci = lax.broadcasted_iota(jnp.int32, (blk, blk), 1)
    tri = jnp.where((ri >= ci) & (ri // C == ci // C), 1.0, 0.0).astype(BF16)
    hi = x.astype(BF16)
    r1 = x - hi.astype(F32)
    mid = r1.astype(BF16)
    lo = (r1 - mid.astype(F32)).astype(BF16)
    out = []
    for b in range(n // blk):
        rs = slice(b * blk, (b + 1) * blk)
        dot = lambda t: jnp.dot(tri, t[rs], preferred_element_type=F32)
        out.append(dot(hi) + dot(mid) + dot(lo))
    return jnp.concatenate(out, axis=0)


def _gla_kernel(qk_ref, v_ref, gate_ref, lr_ref, wlr_ref, blr_ref, gg_ref, s0_ref,
                o_ref, sf_ref, s_scr, o_scr, *, TT, C, t_valid, T, own_state):
    j = pl.program_id(1)
    H, DK, DV = GLA_HEADS, GLA_DK, GLA_DV
    HK, HV = H * DK, H * DV
    i0 = lambda shape: lax.broadcasted_iota(jnp.int32, shape, 0)
    i1 = lambda shape: lax.broadcasted_iota(jnp.int32, shape, 1)
    own_cols = i0((H * C, HK)) // C == i1((H * C, HK)) // DK
    causal = i0((H * C, C)) % C >= i1((H * C, C))
    diag = i0((HK, HV)) // DK == i1((HK, HV)) // DV
    lane_head = i1((C, HV)) // DV

    def block_diag(s):
        return jnp.where(diag, jnp.concatenate([s.reshape(HK, DV)] * H, axis=1), 0.0)

    if not own_state:
        @pl.when(j == 0)
        def _():
            s_scr[...] = block_diag(s0_ref[0])

    z = _dot_f32(lr_ref[0], wlr_ref[...]) + blr_ref[...]
    glog = (jnp.minimum(z, 0.0) - jnp.log1p(jnp.exp(-jnp.abs(z)))) * (1.0 / GLA_TAU)
    qk = qk_ref[0]
    q = qk[:, :HK]
    k = qk[:, HK:]
    if t_valid < (C if own_state else T):
        row = j * TT + lax.broadcasted_iota(jnp.int32, (TT, HK), 0)
        row = row % C if own_state else row
        glog = jnp.where(row < t_valid, glog, 0.0)
        k = jnp.where(row < t_valid, k, 0.0)
    v = v_ref[0]
    nc = TT // C
    bc = _cumsum_rows(glog, C)
    bl = jnp.concatenate([jnp.broadcast_to(bc[(c + 1) * C - 1:(c + 1) * C], (C, HK))
                          for c in range(nc)], axis=0)
    qd = q * (DK ** -0.5) * jnp.exp(bc)
    kd = k * jnp.exp(-bc)
    kdl = k * jnp.exp(bl - bc)
    dec = jnp.exp(bl)
    def take_heads(s_bd, dst):
        for h in range(H):
            dst[h] = s_bd[h * DK:(h + 1) * DK, h * DV:(h + 1) * DV]

    s_prev = None if own_state else s_scr[...]
    for c in range(nc):
        rows = slice(c * C, (c + 1) * C)
        qc, kc, klc, vc = qd[rows], kd[rows], kdl[rows], v[rows]
        if own_state:
            s_prev = block_diag(s0_ref[c])
        qz = jnp.where(own_cols, jnp.concatenate([qc] * H, axis=0), 0.0)
        att = jnp.where(causal, _dot_nt(qz, kc), 0.0)
        res = _dot(att, vc)
        intra = res[0:C]
        for h in range(1, H):
            intra = jnp.where(lane_head == h, res[h * C:(h + 1) * C], intra)
        o_scr[rows, :] = intra + _dot(qc, s_prev)
        dec_col = jnp.broadcast_to(dec[c * C:c * C + 1], (8, HK)).T[:, 0:1]
        s_prev = dec_col * s_prev + jnp.where(diag, _dot_tn(klc, vc), 0.0)
        if own_state:
            take_heads(s_prev, sf_ref.at[c])
    if not own_state:
        s_scr[...] = s_prev
        take_heads(s_prev, sf_ref.at[0])

    o = o_scr[...]
    sq = o * o
    hi = sq.astype(BF16)
    r1 = sq - hi.astype(F32)
    mid = r1.astype(BF16)
    lo = (r1 - mid.astype(F32)).astype(BF16)
    same_head = jnp.where(i0((HV, HV)) // DV == i1((HV, HV)) // DV, 1.0, 0.0).astype(BF16)
    hsum = lambda t: jnp.dot(t, same_head, preferred_element_type=F32)
    ms = (hsum(hi) + hsum(mid) + hsum(lo)) * (1.0 / DV)
    o_ref[0] = (o * lax.rsqrt(ms + NORM_EPS) * gg_ref[...] * _silu(gate_ref[0])).astype(o_ref.dtype)


def _gla(p, w_lr, b_lr, g_gla, s0, *, TT, C, t_valid, own_state=False):
    B, T, _ = p["a_qk"].shape
    HK = GLA_HEADS * GLA_DK
    row = lambda w: pl.BlockSpec((1, TT, w), lambda b, j: (b, j, 0))
    full = lambda r, c: pl.BlockSpec((r, c), lambda b, j: (0, 0))
    if own_state:
        st_spec = pl.BlockSpec((TT // C, GLA_HEADS, GLA_DK, GLA_DV), lambda b, j: (j, 0, 0, 0))
    else:
        st_spec = pl.BlockSpec((1, GLA_HEADS, GLA_DK, GLA_DV), lambda b, j: (b, 0, 0, 0))
    return pl.pallas_call(
        functools.partial(_gla_kernel, TT=TT, C=C, t_valid=t_valid, T=T, own_state=own_state),
        grid=(B, T // TT),
        in_specs=[row(2 * HK), row(GLA_WIDTH), row(GLA_WIDTH), row(LANES),
                  full(LANES, HK), full(1, HK), full(1, GLA_WIDTH), st_spec],
        out_specs=[row(GLA_WIDTH), st_spec],
        out_shape=[jax.ShapeDtypeStruct((B, T, GLA_WIDTH), MIX_DTYPE),
                   jax.ShapeDtypeStruct(s0.shape, F32)],
        scratch_shapes=[pltpu.VMEM((HK, GLA_WIDTH), F32), pltpu.VMEM((TT, GLA_WIDTH), F32)],
        compiler_params=_params("arbitrary", "arbitrary"),
        name="gla_mix",
    )(p["a_qk"], p["a_v"], p["a_gate"], p["a_lr"], w_lr, b_lr.reshape(1, HK),
      g_gla.reshape(1, GLA_WIDTH), s0)


def _swa_kernel(q_ref, k_ref, v_ref, g_ref, o_ref, qd_scr, kd_scr, vd_scr, op_scr, lse_scr,
                stage_scr, *, T):
    QB = SWA_QBLOCK
    CP = 256
    (_, d0), (_, d1), (_, d2) = DILATED_PATTERNS
    assert d0 == 1 and d2 % d1 == 0
    dd = d2 // d1

    def regroup(src_ref, dst_scr, scale):
        cast = lambda x: (x * scale).astype(BF16)
        dst = lambda j: pl.ds(pl.multiple_of(j * CP, CP), CP)

        def copy1(j, carry):
            per = T // d1 // CP
            x = src_ref[0, 0, pl.ds(j // per + d1 * CP * (j % per), CP, stride=d1), :]
            stage_scr[dst(j), :] = x
            dst_scr[0, dst(j), :] = cast(x)
            return carry

        def copy2(j, carry):
            per = T // d2 // CP
            r = j // per
            start = (r % d1) * (T // d1) + r // d1 + dd * CP * (j % per)
            dst_scr[1, dst(j), :] = cast(stage_scr[pl.ds(start, CP, stride=dd), :])
            return carry

        for body in (copy1, copy2):
            lax.fori_loop(0, T // CP, body, 0)

    regroup(q_ref, qd_scr, SWA_HD ** -0.5)
    regroup(k_ref, kd_scr, 1.0)
    regroup(v_ref, vd_scr, 1.0)

    ri = lax.broadcasted_iota(jnp.int32, (2 * QB, 2 * QB), 0) % QB
    ci = lax.broadcasted_iota(jnp.int32, (2 * QB, 2 * QB), 1)
    band = (ci >= ri) & (ci <= ri + QB)
    bias_band = jnp.where(band, 0.0, NEG_BIG)
    bias_first = jnp.where(band & (ci >= QB), 0.0, NEG_BIG)
    lo_q = lax.broadcasted_iota(jnp.int32, (QB, LANES), 1) < SWA_HD
    ones = jnp.ones((2 * QB, LANES), BF16)

    def step(i, carry):
        cur = pl.ds(pl.multiple_of(i * QB, QB), QB)
        prev = pl.ds(pl.multiple_of(jnp.maximum(i - 1, 0) * QB, QB), QB)
        for p, (_, d) in enumerate(DILATED_PATTERNS):
            nblk = T // d // QB
            r = i // nblk
            n = i % nblk
            if p == 0:
                q = (q_ref[0, 0, cur, :] * (SWA_HD ** -0.5)).astype(BF16)
                pair = lambda ref: jnp.concatenate([ref[0, 0, prev, :], ref[0, 0, cur, :]],
                                                   axis=0).astype(BF16)
                k, v = pair(k_ref), pair(v_ref)
            else:
                q = qd_scr[p - 1, cur, :]
                k = jnp.concatenate([kd_scr[p - 1, prev, :], kd_scr[p - 1, cur, :]], axis=0)
                v = jnp.concatenate([vd_scr[p - 1, prev, :], vd_scr[p - 1, cur, :]], axis=0)
            qz = jnp.concatenate([jnp.where(lo_q, q, 0), jnp.where(lo_q, 0, q)], axis=0)
            s = lax.dot_general(qz, k, (((1,), (1,)), ((), ())), preferred_element_type=F32)
            s = s + jnp.where(n > 0, bias_band, bias_first)
            m = jnp.max(s, axis=-1, keepdims=True)
            e = jnp.exp(s - m).astype(BF16)
            ox = jnp.dot(e, jnp.concatenate([v, ones], axis=1), preferred_element_type=F32)
            num = jnp.where(lo_q, ox[:QB, :LANES], ox[QB:, :LANES])
            den = jnp.where(lo_q, ox[:QB, LANES:], ox[QB:, LANES:])
            mx = jnp.where(lo_q, jnp.broadcast_to(m[:QB], (QB, LANES)),
                           jnp.broadcast_to(m[QB:], (QB, LANES)))
            start = r + d * QB * n
            dst = pl.ds(start, QB) if d == 1 else pl.ds(start, QB, stride=d)
            op_scr[p, dst, :] = num / den
            lse_scr[p, dst, :] = mx + jnp.log(den)
        return carry

    lax.fori_loop(0, T // QB, step, 0, unroll=4)

    CH = 512

    def combine(i, carry):
        rs = pl.ds(pl.multiple_of(i * CH, CH), CH)
        l0, l1, l2 = lse_scr[0, rs, :], lse_scr[1, rs, :], lse_scr[2, rs, :]
        mx = jnp.maximum(jnp.maximum(l0, l1), l2)
        w0, w1, w2 = jnp.exp(l0 - mx), jnp.exp(l1 - mx), jnp.exp(l2 - mx)
        mixed = (w0 * op_scr[0, rs, :] + w1 * op_scr[1, rs, :] + w2 * op_scr[2, rs, :]) / (w0 + w1 + w2)
        o_ref[0, 0, rs, :] = (mixed * _silu(g_ref[0, 0, rs, :])).astype(o_ref.dtype)
        return carry

    lax.fori_loop(0, T // CH, combine, 0)


def _swa_prompt(p):
    B, NP, T, _ = p["b_q"].shape
    spec = pl.BlockSpec((1, 1, T, LANES), lambda b, hp: (b, hp, 0, 0))
    return pl.pallas_call(
        functools.partial(_swa_kernel, T=T),
        grid=(B, NP),
        in_specs=[spec] * 4,
        out_specs=spec,
        out_shape=jax.ShapeDtypeStruct((B, NP, T, LANES), MIX_DTYPE),
        scratch_shapes=[pltpu.VMEM((len(DILATED_PATTERNS) - 1, T, LANES), BF16)] * 3
        + [pltpu.VMEM((len(DILATED_PATTERNS), T, LANES), F32)] * 2
        + [pltpu.VMEM((T, LANES), F32)],
        compiler_params=_params("arbitrary", "arbitrary"),
        name="swa_prompt",
    )(p["b_q"], p["b_k"], p["b_v"], p["b_gate"])


def _swa_dec_kernel(ck_ref, cv_ref, q_ref, k_ref, v_ref, g_ref, mult_ref, multf_ref, *rest, L, Tn):
    o_ref, ko_ref, vo_ref = rest[-3:]
    HG = ck_ref.shape[2]
    lane = lax.broadcasted_iota(jnp.int32, (HG, SWA_HD, LANES), 2)

    def shifted(c_ref, new_rows):
        r = pltpu.roll(c_ref[0, 0], L - Tn, 2)
        moved = pltpu.roll(new_rows, SAMPLE_PAD - Tn, 0)
        pad = jnp.concatenate([jnp.zeros((LANES - SAMPLE_PAD, HG * SWA_HD), F32), moved], axis=0)
        tile = pad.T.reshape(HG, SWA_HD, LANES)
        last = jnp.where(lane >= LANES - Tn, tile, r[:, :, L - LANES:])
        return jnp.concatenate([r[:, :, :L - LANES], last], axis=2)

    kout = shifted(ck_ref, k_ref[0])
    vout = shifted(cv_ref, v_ref[0])
    ko_ref[0, 0] = kout
    vo_ref[0, 0] = vout

    q = q_ref[0] * (SWA_HD ** -0.5)
    q3 = jnp.stack([q[:, h * SWA_HD:(h + 1) * SWA_HD] for h in range(HG)], axis=0).astype(BF16)
    qk = lambda kk: jnp.einsum('hid,hdr->hir', q3, kk.astype(BF16), preferred_element_type=F32)
    pv = lambda pp, vv: jnp.einsum('hir,hdr->hid', pp.astype(BF16), vv.astype(BF16),
                                   preferred_element_type=F32)
    mult = mult_ref[...][None]
    multf = multf_ref[...][None]
    s = jnp.where(mult > 0, qk(kout), NEG_BIG)
    sf = jnp.where(multf > 0, qk(ck_ref[0, 0, :, :, 0:LANES]), NEG_BIG)
    m = jnp.maximum(jnp.max(s, axis=-1, keepdims=True), jnp.max(sf, axis=-1, keepdims=True))
    pe = jnp.exp(s - m) * mult
    pf = jnp.exp(sf - m) * multf
    den = jnp.sum(pe, axis=-1, keepdims=True) + jnp.sum(pf, axis=-1, keepdims=True)
    den = jnp.where(den > 0, den, 1.0)
    o3 = (pv(pe, vout) + pv(pf, cv_ref[0, 0, :, :, 0:LANES])) / den
    o = jnp.concatenate([o3[h] for h in range(HG)], axis=1)
    o_ref[0] = (o * _silu(g_ref[0])).astype(o_ref.dtype)


def _mult_tables(L, Tn):
    assert L == SWA_WMAX
    qpos = SWA_WMAX + np.arange(SAMPLE_PAD)[:, None]

    def count(kpos):
        off = qpos - kpos[None, :]
        cnt = np.zeros(off.shape, np.float32)
        for w, d in DILATED_PATTERNS:
            cnt += ((off >= 0) & (off <= w) & (off % d == 0)).astype(np.float32)
        cnt[Tn:] = 0.0
        return cnt

    mult = count(np.arange(L) + Tn)
    multf = count(np.arange(LANES))
    multf[:, Tn:] = 0.0
    return jnp.asarray(mult), jnp.asarray(multf)


def _swa_sample(p, cache_k, cache_v, l, Tn, bufs):
    B = p["b_q"].shape[0]
    L = cache_k.shape[-1]
    HG = 8
    mult, multf = _mult_tables(L, Tn)
    cspec = pl.BlockSpec((1, 1, HG, SWA_HD, L), lambda b, g: (l, b, g, 0, 0))
    nspec = pl.BlockSpec((1, SAMPLE_PAD, HG * SWA_HD), lambda b, g: (b, 0, g))
    const = lambda a: pl.BlockSpec(a.shape, lambda b, g: (0, 0))
    stacked = jax.ShapeDtypeStruct(cache_k.shape, F32)
    in_specs = [cspec, cspec, nspec, nspec, nspec, nspec, const(mult), const(multf)]
    args = [cache_k, cache_v, p["b_q"], p["b_k"], p["b_v"], p["b_gate"], mult, multf]
    aliases = {}
    if bufs is not None:
        in_specs += [pl.BlockSpec(memory_space=pl.ANY)] * 2
        aliases = {len(args): 1, len(args) + 1: 2}
        args += list(bufs)
    return pl.pallas_call(
        functools.partial(_swa_dec_kernel, L=L, Tn=Tn),
        grid=(B, SWA_HEADS // HG),
        in_specs=in_specs,
        out_specs=[nspec, cspec, cspec],
        out_shape=[jax.ShapeDtypeStruct((B, SAMPLE_PAD, SWA_WIDTH), MIX_DTYPE), stacked, stacked],
        input_output_aliases=aliases,
        compiler_params=_params("arbitrary", "arbitrary"),
        name="swa_sample",
    )(*args)


def _s5prep_kernel(lr_ref, li_ref, ldt_ref, bre_ref, bim_ref, bcat_ref, tab_ref):
    lr = lr_ref[0]
    li = li_ref[0]
    dt = jnp.exp(ldt_ref[0])
    mag = jnp.exp(lr * dt)
    br = mag * jnp.cos(li * dt)
    bi = mag * jnp.sin(li * dt)
    den = lr * lr + li * li
    nr = br - 1.0
    cr = (nr * lr + bi * li) / den
    ci = (bi * lr - nr * li) / den
    bre = bre_ref[0]
    bim = bim_ref[0]
    bcat_ref[0, :, 0:SSM_FLAT] = (cr * bre - ci * bim).astype(BF16)
    bcat_ref[0, :, SSM_FLAT:2 * SSM_FLAT] = (cr * bim + ci * bre).astype(BF16)
    tab_ref[0, 0] = jnp.broadcast_to(br, (8, SSM_FLAT))
    tab_ref[0, 1] = jnp.broadcast_to(bi, (8, SSM_FLAT))


def _s5_prep(lam_re, lam_im, log_dt, b_re, b_im):
    eye = jnp.eye(SSM_GROUPS, dtype=F32)

    def block_diag_b(b):
        bt = jnp.transpose(b, (0, 1, 3, 2))
        return (bt[:, :, :, None, :] * eye[None, :, None, :, None]).reshape(DEPTH, SSM_WIDTH, SSM_FLAT)

    flat = lambda a: a.reshape(DEPTH, 1, SSM_FLAT)
    ldt = jnp.broadcast_to(log_dt[:, :, None], (DEPTH, SSM_GROUPS, SSM_STATE))
    vec = pl.BlockSpec((1, 1, SSM_FLAT), lambda l: (l, 0, 0))
    mat = pl.BlockSpec((1, SSM_WIDTH, SSM_FLAT), lambda l: (l, 0, 0))
    return pl.pallas_call(
        _s5prep_kernel,
        grid=(DEPTH,),
        in_specs=[vec, vec, vec, mat, mat],
        out_specs=[pl.BlockSpec((1, SSM_WIDTH, 2 * SSM_FLAT), lambda l: (l, 0, 0)),
                   pl.BlockSpec((1, 2, 8, SSM_FLAT), lambda l: (l, 0, 0, 0))],
        out_shape=[jax.ShapeDtypeStruct((DEPTH, SSM_WIDTH, 2 * SSM_FLAT), BF16),
                   jax.ShapeDtypeStruct((DEPTH, 2, 8, SSM_FLAT), F32)],
        compiler_params=_params("arbitrary"),
        name="s5_prep",
    )(flat(lam_re), flat(lam_im), flat(ldt), block_diag_b(b_re), block_diag_b(b_im))


def _s5_kernel(u_ref, gate_ref, x0r_ref, x0i_ref, bcat_ref, cre_ref, cim_ref, tab_ref, d_ref,
               wglu_ref, bglu_ref, o_ref, sr_ref, si_ref, xr_scr, xi_scr, cr_scr, ci_scr,
               *, TT, B, last_tile, last_t):
    j = pl.program_id(0)
    NBG, NL = xr_scr.shape[0], xr_scr.shape[1]
    split = lambda a: jnp.stack([a[:, LANES * i:LANES * (i + 1)] for i in range(NL)], axis=0)
    merge = lambda a: jnp.concatenate([a[i] for i in range(NL)], axis=1)

    @pl.when(j == 0)
    def _():
        xr_scr[...] = jnp.zeros(xr_scr.shape, F32)
        xi_scr[...] = jnp.zeros(xi_scr.shape, F32)
        for g in range(NBG):
            cr_scr[g] = split(x0r_ref[g])
            ci_scr[g] = split(x0i_ref[g])

    u = u_ref[...].reshape(B * TT, SSM_WIDTH)
    bu = _dot(u, bcat_ref[...])
    for b in range(B):
        g, s = divmod(b, 8)
        rows = pl.ds(s, TT, stride=8)
        for i in range(NL):
            xr_scr[g, i, rows, :] = bu[b * TT:(b + 1) * TT, LANES * i:LANES * (i + 1)]
            xi_scr[g, i, rows, :] = bu[b * TT:(b + 1) * TT, SSM_FLAT + LANES * i:SSM_FLAT + LANES * (i + 1)]

    lam_r = split(tab_ref[0])
    lam_i = split(tab_ref[1])

    def step(t, carry):
        rows = pl.ds(pl.multiple_of(t * 8, 8), 8)
        out = []
        for g in range(NBG):
            cr, ci = carry[2 * g], carry[2 * g + 1]
            nr = lam_r * cr - lam_i * ci + xr_scr[g, :, rows, :]
            ni = lam_r * ci + lam_i * cr + xi_scr[g, :, rows, :]
            xr_scr[g, :, rows, :] = nr
            xi_scr[g, :, rows, :] = ni
            out += [nr, ni]
        return tuple(out)

    init = tuple(ref[g] for g in range(NBG) for ref in (cr_scr, ci_scr))
    fin = lax.fori_loop(0, TT, step, init, unroll=min(TT, 8))
    for g in range(NBG):
        cr_scr[g] = fin[2 * g]
        ci_scr[g] = fin[2 * g + 1]

    @pl.when(j == last_tile)
    def _():
        rows = slice(last_t * 8, last_t * 8 + 8)
        for g in range(NBG):
            sr_ref[g] = merge(xr_scr[g, :, rows, :])
            si_ref[g] = merge(xi_scr[g, :, rows, :])

    def gather(scr):
        per_b = []
        for b in range(B):
            g, s = divmod(b, 8)
            rows = pl.ds(s, TT, stride=8)
            per_b.append(jnp.concatenate([scr[g, i, rows, :] for i in range(NL)], axis=1))
        return jnp.concatenate(per_b, axis=0)

    y = _dot(gather(xr_scr), cre_ref[...]) - _dot(gather(xi_scr), cim_ref[...]) + d_ref[...] * u
    z = jax.nn.gelu(y)
    out = z * jax.nn.sigmoid(_dot(z, wglu_ref[...]) + bglu_ref[...])
    out = out * _silu(gate_ref[...].reshape(B * TT, SSM_WIDTH))
    o_ref[...] = out.reshape(B, TT, SSM_WIDTH).astype(o_ref.dtype)


def _s5(p, x0r, x0i, bcat, cre, cim, tab, d_skip, w_glu, b_glu, *, TT, t_valid):
    B, T, _ = p["c_u"].shape
    NBG = x0r.shape[0]
    NL = SSM_FLAT // LANES
    last = t_valid - 1
    row = pl.BlockSpec((B, TT, SSM_WIDTH), lambda j: (0, j, 0))
    st = pl.BlockSpec((NBG, 8, SSM_FLAT), lambda j: (0, 0, 0))
    full = lambda a: pl.BlockSpec(a.shape, lambda j: (0,) * a.ndim)
    d2 = d_skip.reshape(1, SSM_WIDTH)
    bg2 = b_glu.reshape(1, SSM_WIDTH)
    state = jax.ShapeDtypeStruct((NBG, 8, SSM_FLAT), F32)
    return pl.pallas_call(
        functools.partial(_s5_kernel, TT=TT, B=B, last_tile=last // TT, last_t=last % TT),
        grid=(T // TT,),
        in_specs=[row, row, st, st, full(bcat), full(cre), full(cim), full(tab), full(d2),
                  full(w_glu), full(bg2)],
        out_specs=[row, st, st],
        out_shape=[jax.ShapeDtypeStruct((B, T, SSM_WIDTH), MIX_DTYPE), state, state],
        scratch_shapes=[pltpu.VMEM((NBG, NL, TT * 8, LANES), F32)] * 2
        + [pltpu.VMEM((NBG, NL, 8, LANES), F32)] * 2,
        compiler_params=_params("arbitrary"),
        name="s5_mix",
    )(p["c_u"], p["c_gate"], x0r, x0i, bcat, cre, cim, tab, d2, w_glu, bg2)


def _out_kernel(oa_ref, ob_ref, oc_ref, x_ref, gate_ref, g_ref, w_ref, o_ref):
    a1 = GLA_WIDTH
    a2 = GLA_WIDTH + SWA_WIDTH
    if len(ob_ref.shape) == 4:
        ob = jnp.concatenate([ob_ref[0, hp] for hp in range(ob_ref.shape[1])], axis=1)
    else:
        ob = ob_ref[0]
    y = (jnp.dot(oa_ref[0].astype(BF16), w_ref[0, 0:a1, :], preferred_element_type=F32)
         + jnp.dot(ob.astype(BF16), w_ref[0, a1:a2, :], preferred_element_type=F32)
         + jnp.dot(oc_ref[0].astype(BF16), w_ref[0, a2:, :], preferred_element_type=F32))
    ms = jnp.mean(y * y, axis=-1, keepdims=True)
    yn = y * lax.rsqrt(ms + NORM_EPS) * g_ref[...]
    o_ref[0] = x_ref[0] + gate_ref[0] * yn


def _out_proj(oa, ob, oc, x, gate, g_post, w_out, *, tm, layer):
    nb, T, _ = x.shape
    R = gate.shape[1]
    row = lambda w: pl.BlockSpec((1, tm, w), lambda b, j: (b, j, 0))
    if R == 1:
        gate_spec = pl.BlockSpec((1, 1, D_MODEL), lambda b, j: (b, 0, 0))
    else:
        gate_spec = row(D_MODEL)
    if ob.ndim == 4:
        ob_spec = pl.BlockSpec((1, ob.shape[1], tm, LANES), lambda b, j: (b, 0, j, 0))
    else:
        ob_spec = row(SWA_WIDTH)
    return pl.pallas_call(
        _out_kernel,
        grid=(nb, T // tm),
        in_specs=[row(GLA_WIDTH), ob_spec, row(SSM_WIDTH), row(D_MODEL), gate_spec,
                  pl.BlockSpec((1, D_MODEL), lambda b, j: (0, 0)),
                  pl.BlockSpec((1, D_MODEL, D_MODEL), lambda b, j: (layer, 0, 0))],
        out_specs=row(D_MODEL),
        out_shape=jax.ShapeDtypeStruct((nb, T, D_MODEL), F32),
        compiler_params=_params("arbitrary", "arbitrary"),
        name="out_proj",
    )(oa, ob, oc, x, gate, g_post.reshape(1, D_MODEL), w_out)


def _rope_tables(pos):
    half = ROPE_DIMS // 2
    inv = ROPE_THETA ** (-jnp.arange(half, dtype=F32) * (2.0 / ROPE_DIMS))
    ang = pos.astype(F32)[:, None] * inv[None, :]
    cos, sin = jnp.cos(ang), jnp.sin(ang)
    n = pos.shape[0]
    rest = SWA_HD - ROPE_DIMS
    head = lambda a, b, fill: jnp.concatenate([a, b, jnp.full((n, rest), fill, F32)], axis=1)
    zero = jnp.zeros((n, half), F32)
    reps = LANES // SWA_HD
    return (jnp.tile(head(cos, cos, 1.0), (1, reps)),
            jnp.tile(head(-sin, zero, 0.0), (1, reps)),
            jnp.tile(head(zero, sin, 0.0), (1, reps)))


def _rearrange_w_in(w_in):
    GK = GLA_HEADS * GLA_DK
    o = np.cumsum([0, GK, GK, GLA_WIDTH, GLA_RANK, GLA_WIDTH, SWA_WIDTH, SWA_WIDTH, SWA_WIDTH,
                   SWA_WIDTH, SSM_WIDTH, SSM_WIDTH])
    piece = lambda i: w_in[:, :, o[i]:o[i + 1]]
    lr = jnp.pad(piece(3), ((0, 0), (0, 0), (0, LANES - GLA_RANK)))
    cat = jnp.concatenate([piece(0), piece(1), piece(2), piece(4), piece(5), piece(6), piece(7),
                           piece(8), piece(9), piece(10), lr], axis=-1)
    return cat.astype(BF16)


def kernel(x_prompt, x_sample, c_prompt, c_sample, state_gla, cache_swa_k, cache_swa_v, state_ssm_re, state_ssm_im, w_ada, b_ada, g_pre, g_post, w_in, w_gla_lr, b_gla_lr, g_gla, ssm_lambda_re, ssm_lambda_im, ssm_log_dt, ssm_b_re, ssm_b_im, ssm_c_re, ssm_c_im, ssm_d, w_glu, b_glu, w_out):
    Bp, Tp, _ = x_prompt.shape
    Bs, Ts, _ = x_sample.shape
    PAD = SAMPLE_PAD
    Lc = cache_swa_k.shape[2]

    w_cat = _rearrange_w_in(w_in)
    w_out_b = w_out.astype(BF16)
    w_glu_b = w_glu.astype(BF16)
    w_lr_pad = jnp.pad(w_gla_lr, ((0, 0), (0, LANES - GLA_RANK), (0, 0)))
    eye = jnp.eye(SSM_GROUPS, dtype=F32)

    def block_diag_c(c):
        ct = jnp.transpose(c, (0, 1, 3, 2))
        return (ct[:, :, :, None, :] * eye[None, :, None, :, None]).reshape(DEPTH, SSM_FLAT, SSM_WIDTH)

    cre = block_diag_c(ssm_c_re).astype(BF16)
    cim = block_diag_c(ssm_c_im).astype(BF16)
    bcat, tab = _s5_prep(ssm_lambda_re, ssm_lambda_im, ssm_log_dt, ssm_b_re, ssm_b_im)

    n_c = Bp + Bs
    c_rows = -(-n_c // 8) * 8
    c_all = jnp.pad(jnp.concatenate([c_prompt, c_sample], axis=0), ((0, c_rows - n_c), (0, 0)))
    mod = _ada_mod(c_all, w_ada, b_ada)

    tabs_p = _rope_tables(jnp.arange(Tp))
    tabs_s = _rope_tables(PAST_LEN + jnp.arange(Bs * PAD) % PAD)

    xp = x_prompt
    xs = jnp.pad(x_sample, ((0, 0), (0, PAD - Ts), (0, 0))).reshape(1, Bs * PAD, D_MODEL)
    seq_last = lambda a: jnp.transpose(a, (0, 1, 3, 4, 2))
    seq_first = lambda a: jnp.transpose(a, (0, 1, 4, 2, 3))
    cache_k = seq_last(cache_swa_k)
    cache_v = seq_last(cache_swa_v)
    zero_gla = jnp.zeros((Bp, GLA_HEADS, GLA_DK, GLA_DV), F32)
    groups = lambda n: -(-n // 8)
    ssm_in = lambda a: jnp.pad(a.reshape(a.shape[0], SSM_FLAT),
                               ((0, groups(a.shape[0]) * 8 - a.shape[0]), (0, 0))
                               ).reshape(groups(a.shape[0]), 8, SSM_FLAT)
    zero_ssm = jnp.zeros((groups(Bp), 8, SSM_FLAT), F32)

    acc = [[] for _ in range(6)]
    tails = None
    shifted = None
    for l in range(DEPTH):
        def chunk(rows, k):
            return mod[l, rows, k * D_MODEL:(k + 1) * D_MODEL]

        pr = slice(0, Bp)
        shift, scale, gate = (chunk(pr, k).reshape(Bp, 1, D_MODEL) for k in range(3))
        p = _proj(xp, scale, shift, g_pre[l], w_cat, tabs_p, tm=512, layer=l, tail=(l, tails))
        tails = (p["k_tail"], p["v_tail"])
        oa, gla_p = _gla(p, w_lr_pad[l], b_gla_lr[l], g_gla[l], zero_gla, TT=1024, C=GLA_CHUNK, t_valid=Tp)
        ob = _swa_prompt(p)
        oc, re_p, im_p = _s5(p, zero_ssm, zero_ssm, bcat[l], cre[l], cim[l], tab[l], ssm_d[l],
                             w_glu_b[l], b_glu[l], TT=128, t_valid=Tp)
        xp = _out_proj(oa, ob, oc, xp, gate, g_post[l], w_out_b, tm=1024, layer=l)

        sr = slice(Bp, Bp + Bs)
        shift, scale, gate = (jnp.repeat(chunk(sr, k), PAD, axis=0).reshape(1, Bs * PAD, D_MODEL)
                              for k in range(3))
        qf = _proj(xs, scale, shift, g_pre[l], w_cat, tabs_s, tm=Bs * PAD, layer=l)
        q = {n: a.reshape(Bs, PAD, a.shape[-1]) for n, a in qf.items()}
        oa, gla_s = _gla(qf, w_lr_pad[l], b_gla_lr[l], g_gla[l], state_gla[l], TT=Bs * PAD, C=PAD,
                         t_valid=Ts, own_state=True)
        ob, k_s, v_s = _swa_sample(q, cache_k, cache_v, l, Ts, shifted)
        shifted = (k_s, v_s)
        oc, re_s, im_s = _s5(q, ssm_in(state_ssm_re[l]), ssm_in(state_ssm_im[l]), bcat[l], cre[l],
                             cim[l], tab[l], ssm_d[l], w_glu_b[l], b_glu[l], TT=PAD, t_valid=Ts)
        flat = lambda a: a.reshape(1, Bs * PAD, a.shape[-1])
        xs = _out_proj(flat(oa), flat(ob), flat(oc), xs, gate, g_post[l], w_out_b, tm=Bs * PAD, layer=l)

        for i, a in enumerate((gla_p, gla_s, re_p, im_p, re_s, im_s)):
            acc[i].append(a)

    st = [jnp.stack(a) for a in acc]
    ssm = lambda a, n: a.reshape(DEPTH, -1, SSM_GROUPS, SSM_STATE)[:, :n]
    y_sample = xs.reshape(Bs, PAD, D_MODEL)[:, :Ts]
    return (xp, y_sample, st[0], st[1], seq_first(tails[0]), seq_first(tails[1]),
            seq_first(shifted[0]), seq_first(shifted[1]),
            ssm(st[2], Bp), ssm(st[3], Bp), ssm(st[4], Bs), ssm(st[5], Bs))
```

```python
import functools

import numpy as np
import jax
import jax.numpy as jnp
from jax import lax
from jax.experimental import pallas as pl
from jax.experimental.pallas import tpu as pltpu

F32 = jnp.float32
BF16 = jnp.bfloat16

D_MODEL = 1024
DEPTH = 4
PAST_LEN = 8192
GLA_HEADS = 4
GLA_DK = 32
GLA_DV = 64
GLA_WIDTH = GLA_HEADS * GLA_DV
GLA_RANK = 16
GLA_TAU = 16.0
GLA_CHUNK = 64
SWA_WIDTH = 512
SWA_HEADS = 8
SWA_HD = 64
DILATED_PATTERNS = ((128, 1), (512, 4), (2048, 16))
SWA_WMAX = 2048
SWA_QBLOCK = 128
ROPE_THETA = 500000.0
ROPE_DIMS = 16
SSM_WIDTH = 256
SSM_GROUP = 16
SSM_GROUPS = 16
SSM_STATE = 64
SSM_FLAT = SSM_GROUPS * SSM_STATE
NORM_EPS = 1e-6
NEG_BIG = -1e30

SAMPLE_PAD = 8
LANES = 128
VMEM_LIMIT = 48 * 1024 * 1024
MIX_DTYPE = BF16

W_COLS = (("a_qk", 256), ("a_v", 256), ("a_gate", 256), ("b_q", 512), ("b_k", 512),
          ("b_v", 512), ("b_gate", 512), ("c_u", 256), ("c_gate", 256), ("a_lr", 128))
W_OFF = {}
_o = 0
for _n, _w in W_COLS:
    W_OFF[_n] = (_o, _o + _w)
    _o += _w
W_TOTAL = _o
SWA_PIECES = ("b_q", "b_k", "b_v", "b_gate")


def _params(*sem):
    return pltpu.CompilerParams(dimension_semantics=sem, vmem_limit_bytes=VMEM_LIMIT)


def _dot(a, b):
    return jnp.dot(a.astype(BF16), b.astype(BF16), preferred_element_type=F32)


def _dot_nt(a, b):
    return lax.dot_general(a.astype(BF16), b.astype(BF16), (((1,), (1,)), ((), ())),
                           preferred_element_type=F32)


def _dot_tn(a, b):
    return lax.dot_general(a.astype(BF16), b.astype(BF16), (((0,), (0,)), ((), ())),
                           preferred_element_type=F32)


def _dot_f32(a, b):
    a_hi, b_hi = a.astype(BF16), b.astype(BF16)
    a_lo = (a - a_hi.astype(F32)).astype(BF16)
    b_lo = (b - b_hi.astype(F32)).astype(BF16)
    dot = lambda x, y: jnp.dot(x, y, preferred_element_type=F32)
    return dot(a_hi, b_hi) + (dot(a_hi, b_lo) + dot(a_lo, b_hi))


def _silu(x):
    return x * jax.nn.sigmoid(x)


def _mod_kernel(c_ref, w_ref, b_ref, o_ref):
    o_ref[0] = _dot_f32(_silu(c_ref[...]), w_ref[0]) + b_ref[0]


def _ada_mod(c_all, w_ada, b_ada):
    rows = c_all.shape[0]
    nt = 3 * D_MODEL // 1024
    return pl.pallas_call(
        _mod_kernel,
        grid=(DEPTH, nt),
        in_specs=[pl.BlockSpec((rows, D_MODEL), lambda l, n: (0, 0)),
                  pl.BlockSpec((1, D_MODEL, 1024), lambda l, n: (l, 0, n)),
                  pl.BlockSpec((1, 1, 1024), lambda l, n: (l, 0, n))],
        out_specs=pl.BlockSpec((1, rows, 1024), lambda l, n: (l, 0, n)),
        out_shape=jax.ShapeDtypeStruct((DEPTH, rows, 3 * D_MODEL), F32),
        compiler_params=_params("arbitrary", "arbitrary"),
        name="ada_mod",
    )(c_all, w_ada, b_ada.reshape(DEPTH, 1, 3 * D_MODEL))


def _rope(x, cos, sa, sb):
    outs = []
    for c in range(SWA_WIDTH // LANES):
        xc = x[:, LANES * c:LANES * (c + 1)]
        outs.append(xc * cos + pltpu.roll(xc, LANES - ROPE_DIMS // 2, 1) * sa
                    + pltpu.roll(xc, ROPE_DIMS // 2, 1) * sb)
    return jnp.concatenate(outs, axis=1)


def _proj_kernel(x_ref, sc_ref, sh_ref, g_ref, w_ref, cos_ref, sa_ref, sb_ref, *rest, emit_tail):
    n_out = len(W_COLS) + (2 if emit_tail else 0)
    outs = rest[len(rest) - n_out:]
    x = x_ref[0]
    ms = jnp.mean(x * x, axis=-1, keepdims=True)
    h = x * lax.rsqrt(ms + NORM_EPS) * g_ref[...]
    h = h * (1.0 + sc_ref[0]) + sh_ref[0]
    hb = h.astype(BF16)

    def mm(name):
        c0, c1 = W_OFF[name]
        return jnp.dot(hb, w_ref[0, :, c0:c1], preferred_element_type=F32)

    names = [n for n, _ in W_COLS]
    cos, sa, sb = cos_ref[...], sa_ref[...], sb_ref[...]
    for idx, name in enumerate(names):
        val = mm(name)
        if name in ("b_q", "b_k"):
            val = _rope(val, cos, sa, sb)
        if emit_tail and name in SWA_PIECES:
            for hp in range(SWA_WIDTH // LANES):
                outs[idx][0, hp] = val[:, hp * LANES:(hp + 1) * LANES]
        else:
            outs[idx][0] = val
        if emit_tail and name == "b_k":
            outs[len(names)][0, 0] = val.T.reshape(SWA_HEADS, SWA_HD, val.shape[0])
        if emit_tail and name == "b_v":
            outs[len(names) + 1][0, 0] = val.T.reshape(SWA_HEADS, SWA_HD, val.shape[0])


def _proj(x, scale, shift, g_pre, w_cat, rope_tabs, *, tm, layer, tail=None):
    nb, T, _ = x.shape
    nt = T // tm
    R = scale.shape[1]
    emit_tail = tail is not None
    if R == 1:
        mod_spec = pl.BlockSpec((1, 1, D_MODEL), lambda b, j: (b, 0, 0))
    else:
        mod_spec = pl.BlockSpec((1, tm, D_MODEL), lambda b, j: (b, j, 0))
    tab_spec = pl.BlockSpec((tm, LANES), lambda b, j: (j, 0))
    out_specs, out_shape = [], []
    for name, w in W_COLS:
        if emit_tail and name in SWA_PIECES:
            out_specs.append(pl.BlockSpec((1, w // LANES, tm, LANES), lambda b, j: (b, 0, j, 0)))
            out_shape.append(jax.ShapeDtypeStruct((nb, w // LANES, T, LANES), F32))
        else:
            out_specs.append(pl.BlockSpec((1, tm, w), lambda b, j: (b, j, 0)))
            out_shape.append(jax.ShapeDtypeStruct((nb, T, w), F32))
    in_specs = [pl.BlockSpec((1, tm, D_MODEL), lambda b, j: (b, j, 0)),
                mod_spec, mod_spec,
                pl.BlockSpec((1, D_MODEL), lambda b, j: (0, 0)),
                pl.BlockSpec((1, D_MODEL, W_TOTAL), lambda b, j: (layer, 0, 0)),
                tab_spec, tab_spec, tab_spec]
    args = [x, scale, shift, g_pre.reshape(1, D_MODEL), w_cat, *rope_tabs]
    aliases = {}
    if emit_tail:
        l, bufs = tail
        keep = min(SWA_WMAX, T)
        first = (T - keep) // tm
        tail_spec = pl.BlockSpec((1, 1, SWA_HEADS, SWA_HD, tm),
                                 lambda b, j: (l, b, 0, 0, jnp.maximum(j - first, 0)))
        out_specs += [tail_spec, tail_spec]
        out_shape += [jax.ShapeDtypeStruct((DEPTH, nb, SWA_HEADS, SWA_HD, keep), F32)] * 2
        if bufs is not None:
            in_specs += [pl.BlockSpec(memory_space=pl.ANY)] * 2
            aliases = {len(args): len(W_COLS), len(args) + 1: len(W_COLS) + 1}
            args += list(bufs)
    res = pl.pallas_call(
        functools.partial(_proj_kernel, emit_tail=emit_tail),
        grid=(nb, nt),
        in_specs=in_specs,
        out_specs=out_specs,
        out_shape=out_shape,
        input_output_aliases=aliases,
        compiler_params=_params("arbitrary", "arbitrary"),
        name="in_proj",
    )(*args)
    names = [n for n, _ in W_COLS]
    d = dict(zip(names, res[:len(names)]))
    if emit_tail:
        d["k_tail"], d["v_tail"] = res[len(names)], res[len(names) + 1]
    return d


def _cumsum_rows(x, C):
    n = x.shape[0]
    blk = min(n, max(C, LANES))
    ri = lax.broadcasted_iota(jnp.int32, (blk, blk), 0)
    ci = lax.broadcasted_iota(jnp.int32, (blk, blk), 1)
    tri = jnp.where((ri >= ci) & (ri // C == ci // C), 1.0, 0.0).astype(BF16)
    hi = x.astype(BF16)
    r1 = x - hi.astype(F32)
    mid = r1.astype(BF16)
    lo = (r1 - mid.astype(F32)).astype(BF16)
    out = []
    for b in range(n // blk):
        rs = slice(b * blk, (b + 1) * blk)
        dot = lambda t: jnp.dot(tri, t[rs], preferred_element_type=F32)
        out.append(dot(hi) + dot(mid) + dot(lo))
    return jnp.concatenate(out, axis=0)


def _gla_kernel(qk_ref, v_ref, gate_ref, lr_ref, wlr_ref, blr_ref, gg_ref, s0_ref,
                o_ref, sf_ref, s_scr, o_scr, *, TT, C, t_valid, T, own_state):
    j = pl.program_id(1)
    H, DK, DV = GLA_HEADS, GLA_DK, GLA_DV
    HK, HV = H * DK, H * DV
    i0 = lambda shape: lax.broadcasted_iota(jnp.int32, shape, 0)
    i1 = lambda shape: lax.broadcasted_iota(jnp.int32, shape, 1)
    own_cols = i0((H * C, HK)) // C == i1((H * C, HK)) // DK
    causal = i0((H * C, C)) % C >= i1((H * C, C))
    diag = i0((HK, HV)) // DK == i1((HK, HV)) // DV
    lane_head = i1((C, HV)) // DV

    def block_diag(s):
        return jnp.where(diag, jnp.concatenate([s.reshape(HK, DV)] * H, axis=1), 0.0)

    if not own_state:
        @pl.when(j == 0)
        def _():
            s_scr[...] = block_diag(s0_ref[0])

    z = _dot_f32(lr_ref[0], wlr_ref[...]) + blr_ref[...]
    glog = (jnp.minimum(z, 0.0) - jnp.log1p(jnp.exp(-jnp.abs(z)))) * (1.0 / GLA_TAU)
    qk = qk_ref[0]
    q = qk[:, :HK]
    k = qk[:, HK:]
    if t_valid < (C if own_state else T):
        row = j * TT + lax.broadcasted_iota(jnp.int32, (TT, HK), 0)
        row = row % C if own_state else row
        glog = jnp.where(row < t_valid, glog, 0.0)
        k = jnp.where(row < t_valid, k, 0.0)
    v = v_ref[0]
    nc = TT // C
    bc = _cumsum_rows(glog, C)
    bl = jnp.concatenate([jnp.broadcast_to(bc[(c + 1) * C - 1:(c + 1) * C], (C, HK))
                          for c in range(nc)], axis=0)
    qd = q * (DK ** -0.5) * jnp.exp(bc)
    kd = k * jnp.exp(-bc)
    kdl = k * jnp.exp(bl - bc)
    dec = jnp.exp(bl)
    def take_heads(s_bd, dst):
        for h in range(H):
            dst[h] = s_bd[h * DK:(h + 1) * DK, h * DV:(h + 1) * DV]

    s_prev = None if own_state else s_scr[...]
    for c in range(nc):
        rows = slice(c * C, (c + 1) * C)
        qc, kc, klc, vc = qd[rows], kd[rows], kdl[rows], v[rows]
        if own_state:
            s_prev = block_diag(s0_ref[c])
        qz = jnp.where(own_cols, jnp.concatenate([qc] * H, axis=0), 0.0)
        att = jnp.where(causal, _dot_nt(qz, kc), 0.0)
        res = _dot(att, vc)
        intra = res[0:C]
        for h in range(1, H):
            intra = jnp.where(lane_head == h, res[h * C:(h + 1) * C], intra)
        o_scr[rows, :] = intra + _dot(qc, s_prev)
        dec_col = jnp.broadcast_to(dec[c * C:c * C + 1], (8, HK)).T[:, 0:1]
        s_prev = dec_col * s_prev + jnp.where(diag, _dot_tn(klc, vc), 0.0)
        if own_state:
            take_heads(s_prev, sf_ref.at[c])
    if not own_state:
        s_scr[...] = s_prev
        take_heads(s_prev, sf_ref.at[0])

    o = o_scr[...]
    sq = o * o
    hi = sq.astype(BF16)
    r1 = sq - hi.astype(F32)
    mid = r1.astype(BF16)
    lo = (r1 - mid.astype(F32)).astype(BF16)
    same_head = jnp.where(i0((HV, HV)) // DV == i1((HV, HV)) // DV, 1.0, 0.0).astype(BF16)
    hsum = lambda t: jnp.dot(t, same_head, preferred_element_type=F32)
    ms = (hsum(hi) + hsum(mid) + hsum(lo)) * (1.0 / DV)
    o_ref[0] = (o * lax.rsqrt(ms + NORM_EPS) * gg_ref[...] * _silu(gate_ref[0])).astype(o_ref.dtype)


def _gla(p, w_lr, b_lr, g_gla, s0, *, TT, C, t_valid, own_state=False):
    B, T, _ = p["a_qk"].shape
    HK = GLA_HEADS * GLA_DK
    row = lambda w: pl.BlockSpec((1, TT, w), lambda b, j: (b, j, 0))
    full = lambda r, c: pl.BlockSpec((r, c), lambda b, j: (0, 0))
    if own_state:
        st_spec = pl.BlockSpec((TT // C, GLA_HEADS, GLA_DK, GLA_DV), lambda b, j: (j, 0, 0, 0))
    else:
        st_spec = pl.BlockSpec((1, GLA_HEADS, GLA_DK, GLA_DV), lambda b, j: (b, 0, 0, 0))
    return pl.pallas_call(
        functools.partial(_gla_kernel, TT=TT, C=C, t_valid=t_valid, T=T, own_state=own_state),
        grid=(B, T // TT),
        in_specs=[row(2 * HK), row(GLA_WIDTH), row(GLA_WIDTH), row(LANES),
                  full(LANES, HK), full(1, HK), full(1, GLA_WIDTH), st_spec],
        out_specs=[row(GLA_WIDTH), st_spec],
        out_shape=[jax.ShapeDtypeStruct((B, T, GLA_WIDTH), MIX_DTYPE),
                   jax.ShapeDtypeStruct(s0.shape, F32)],
        scratch_shapes=[pltpu.VMEM((HK, GLA_WIDTH), F32), pltpu.VMEM((TT, GLA_WIDTH), F32)],
        compiler_params=_params("arbitrary", "arbitrary"),
        name="gla_mix",
    )(p["a_qk"], p["a_v"], p["a_gate"], p["a_lr"], w_lr, b_lr.reshape(1, HK),
      g_gla.reshape(1, GLA_WIDTH), s0)


def _swa_kernel(q_ref, k_ref, v_ref, g_ref, o_ref, qd_scr, kd_scr, vd_scr, op_scr, lse_scr,
                stage_scr, *, T):
    QB = SWA_QBLOCK
    CP = 256
    (_, d0), (_, d1), (_, d2) = DILATED_PATTERNS
    assert d0 == 1 and d2 % d1 == 0
    dd = d2 // d1

    def regroup(src_ref, dst_scr, scale):
        cast = lambda x: (x * scale).astype(BF16)
        dst = lambda j: pl.ds(pl.multiple_of(j * CP, CP), CP)

        def copy1(j, carry):
            per = T // d1 // CP
            x = src_ref[0, 0, pl.ds(j // per + d1 * CP * (j % per), CP, stride=d1), :]
            stage_scr[dst(j), :] = x
            dst_scr[0, dst(j), :] = cast(x)
            return carry

        def copy2(j, carry):
            per = T // d2 // CP
            r = j // per
            start = (r % d1) * (T // d1) + r // d1 + dd * CP * (j % per)
            dst_scr[1, dst(j), :] = cast(stage_scr[pl.ds(start, CP, stride=dd), :])
            return carry

        for body in (copy1, copy2):
            lax.fori_loop(0, T // CP, body, 0)

    regroup(q_ref, qd_scr, SWA_HD ** -0.5)
    regroup(k_ref, kd_scr, 1.0)
    regroup(v_ref, vd_scr, 1.0)

    ri = lax.broadcasted_iota(jnp.int32, (2 * QB, 2 * QB), 0) % QB
    ci = lax.broadcasted_iota(jnp.int32, (2 * QB, 2 * QB), 1)
    band = (ci >= ri) & (ci <= ri + QB)
    bias_band = jnp.where(band, 0.0, NEG_BIG)
    bias_first = jnp.where(band & (ci >= QB), 0.0, NEG_BIG)
    lo_q = lax.broadcasted_iota(jnp.int32, (QB, LANES), 1) < SWA_HD
    ones = jnp.ones((2 * QB, LANES), BF16)

    def step(i, carry):
        cur = pl.ds(pl.multiple_of(i * QB, QB), QB)
        prev = pl.ds(pl.multiple_of(jnp.maximum(i - 1, 0) * QB, QB), QB)
        for p, (_, d) in enumerate(DILATED_PATTERNS):
            nblk = T // d // QB
            r = i // nblk
            n = i % nblk
            if p == 0:
                q = (q_ref[0, 0, cur, :] * (SWA_HD ** -0.5)).astype(BF16)
                pair = lambda ref: jnp.concatenate([ref[0, 0, prev, :], ref[0, 0, cur, :]],
                                                   axis=0).astype(BF16)
                k, v = pair(k_ref), pair(v_ref)
            else:
                q = qd_scr[p - 1, cur, :]
                k = jnp.concatenate([kd_scr[p - 1, prev, :], kd_scr[p - 1, cur, :]], axis=0)
                v = jnp.concatenate([vd_scr[p - 1, prev, :], vd_scr[p - 1, cur, :]], axis=0)
            qz = jnp.concatenate([jnp.where(lo_q, q, 0), jnp.where(lo_q, 0, q)], axis=0)
            s = lax.dot_general(qz, k, (((1,), (1,)), ((), ())), preferred_element_type=F32)
            s = s + jnp.where(n > 0, bias_band, bias_first)
            m = jnp.max(s, axis=-1, keepdims=True)
            e = jnp.exp(s - m).astype(BF16)
            ox = jnp.dot(e, jnp.concatenate([v, ones], axis=1), preferred_element_type=F32)
            num = jnp.where(lo_q, ox[:QB, :LANES], ox[QB:, :LANES])
            den = jnp.where(lo_q, ox[:QB, LANES:], ox[QB:, LANES:])
            mx = jnp.where(lo_q, jnp.broadcast_to(m[:QB], (QB, LANES)),
                           jnp.broadcast_to(m[QB:], (QB, LANES)))
            start = r + d * QB * n
            dst = pl.ds(start, QB) if d == 1 else pl.ds(start, QB, stride=d)
            op_scr[p, dst, :] = num / den
            lse_scr[p, dst, :] = mx + jnp.log(den)
        return carry

    lax.fori_loop(0, T // QB, step, 0, unroll=8)

    CH = 512

    def combine(i, carry):
        rs = pl.ds(pl.multiple_of(i * CH, CH), CH)
        l0, l1, l2 = lse_scr[0, rs, :], lse_scr[1, rs, :], lse_scr[2, rs, :]
        mx = jnp.maximum(jnp.maximum(l0, l1), l2)
        w0, w1, w2 = jnp.exp(l0 - mx), jnp.exp(l1 - mx), jnp.exp(l2 - mx)
        mixed = (w0 * op_scr[0, rs, :] + w1 * op_scr[1, rs, :] + w2 * op_scr[2, rs, :]) / (w0 + w1 + w2)
        o_ref[0, 0, rs, :] = (mixed * _silu(g_ref[0, 0, rs, :])).astype(o_ref.dtype)
        return carry

    lax.fori_loop(0, T // CH, combine, 0)


def _swa_prompt(p):
    B, NP, T, _ = p["b_q"].shape
    spec = pl.BlockSpec((1, 1, T, LANES), lambda b, hp: (b, hp, 0, 0))
    return pl.pallas_call(
        functools.partial(_swa_kernel, T=T),
        grid=(B, NP),
        in_specs=[spec] * 4,
        out_specs=spec,
        out_shape=jax.ShapeDtypeStruct((B, NP, T, LANES), MIX_DTYPE),
        scratch_shapes=[pltpu.VMEM((len(DILATED_PATTERNS) - 1, T, LANES), BF16)] * 3
        + [pltpu.VMEM((len(DILATED_PATTERNS), T, LANES), F32)] * 2
        + [pltpu.VMEM((T, LANES), F32)],
        compiler_params=_params("arbitrary", "arbitrary"),
        name="swa_prompt",
    )(p["b_q"], p["b_k"], p["b_v"], p["b_gate"])


def _swa_dec_kernel(ck_ref, cv_ref, q_ref, k_ref, v_ref, g_ref, mult_ref, multf_ref, *rest, L, Tn):
    o_ref, ko_ref, vo_ref = rest[-3:]
    HG = ck_ref.shape[2]
    lane = lax.broadcasted_iota(jnp.int32, (HG, SWA_HD, LANES), 2)

    def shifted(c_ref, new_rows):
        r = pltpu.roll(c_ref[0, 0], L - Tn, 2)
        moved = pltpu.roll(new_rows, SAMPLE_PAD - Tn, 0)
        pad = jnp.concatenate([jnp.zeros((LANES - SAMPLE_PAD, HG * SWA_HD), F32), moved], axis=0)
        tile = pad.T.reshape(HG, SWA_HD, LANES)
        last = jnp.where(lane >= LANES - Tn, tile, r[:, :, L - LANES:])
        return jnp.concatenate([r[:, :, :L - LANES], last], axis=2)

    kout = shifted(ck_ref, k_ref[0])
    vout = shifted(cv_ref, v_ref[0])
    ko_ref[0, 0] = kout
    vo_ref[0, 0] = vout

    q = q_ref[0] * (SWA_HD ** -0.5)
    q3 = jnp.stack([q[:, h * SWA_HD:(h + 1) * SWA_HD] for h in range(HG)], axis=0).astype(BF16)
    qk = lambda kk: jnp.einsum('hid,hdr->hir', q3, kk.astype(BF16), preferred_element_type=F32)
    pv = lambda pp, vv: jnp.einsum('hir,hdr->hid', pp.astype(BF16), vv.astype(BF16),
                                   preferred_element_type=F32)
    mult = mult_ref[...][None]
    multf = multf_ref[...][None]
    s = jnp.where(mult > 0, qk(kout), NEG_BIG)
    sf = jnp.where(multf > 0, qk(ck_ref[0, 0, :, :, 0:LANES]), NEG_BIG)
    m = jnp.maximum(jnp.max(s, axis=-1, keepdims=True), jnp.max(sf, axis=-1, keepdims=True))
    pe = jnp.exp(s - m) * mult
    pf = jnp.exp(sf - m) * multf
    den = jnp.sum(pe, axis=-1, keepdims=True) + jnp.sum(pf, axis=-1, keepdims=True)
    den = jnp.where(den > 0, den, 1.0)
    o3 = (pv(pe, vout) + pv(pf, cv_ref[0, 0, :, :, 0:LANES])) / den
    o = jnp.concatenate([o3[h] for h in range(HG)], axis=1)
    o_ref[0] = (o * _silu(g_ref[0])).astype(o_ref.dtype)


def _mult_tables(L, Tn):
    assert L == SWA_WMAX
    qpos = SWA_WMAX + np.arange(SAMPLE_PAD)[:, None]

    def count(kpos):
        off = qpos - kpos[None, :]
        cnt = np.zeros(off.shape, np.float32)
        for w, d in DILATED_PATTERNS:
            cnt += ((off >= 0) & (off <= w) & (off % d == 0)).astype(np.float32)
        cnt[Tn:] = 0.0
        return cnt

    mult = count(np.arange(L) + Tn)
    multf = count(np.arange(LANES))
    multf[:, Tn:] = 0.0
    return jnp.asarray(mult), jnp.asarray(multf)


def _swa_sample(p, cache_k, cache_v, l, Tn, bufs):
    B = p["b_q"].shape[0]
    L = cache_k.shape[-1]
    HG = 8
    mult, multf = _mult_tables(L, Tn)
    cspec = pl.BlockSpec((1, 1, HG, SWA_HD, L), lambda b, g: (l, b, g, 0, 0))
    nspec = pl.BlockSpec((1, SAMPLE_PAD, HG * SWA_HD), lambda b, g: (b, 0, g))
    const = lambda a: pl.BlockSpec(a.shape, lambda b, g: (0, 0))
    stacked = jax.ShapeDtypeStruct(cache_k.shape, F32)
    in_specs = [cspec, cspec, nspec, nspec, nspec, nspec, const(mult), const(multf)]
    args = [cache_k, cache_v, p["b_q"], p["b_k"], p["b_v"], p["b_gate"], mult, multf]
    aliases = {}
    if bufs is not None:
        in_specs += [pl.BlockSpec(memory_space=pl.ANY)] * 2
        aliases = {len(args): 1, len(args) + 1: 2}
        args += list(bufs)
    return pl.pallas_call(
        functools.partial(_swa_dec_kernel, L=L, Tn=Tn),
        grid=(B, SWA_HEADS // HG),
        in_specs=in_specs,
        out_specs=[nspec, cspec, cspec],
        out_shape=[jax.ShapeDtypeStruct((B, SAMPLE_PAD, SWA_WIDTH), MIX_DTYPE), stacked, stacked],
        input_output_aliases=aliases,
        compiler_params=_params("arbitrary", "arbitrary"),
        name="swa_sample",
    )(*args)


def _s5prep_kernel(lr_ref, li_ref, ldt_ref, bre_ref, bim_ref, bcat_ref, tab_ref):
    lr = lr_ref[0]
    li = li_ref[0]
    dt = jnp.exp(ldt_ref[0])
    mag = jnp.exp(lr * dt)
    br = mag * jnp.cos(li * dt)
    bi = mag * jnp.sin(li * dt)
    den = lr * lr + li * li
    nr = br - 1.0
    cr = (nr * lr + bi * li) / den
    ci = (bi * lr - nr * li) / den
    bre = bre_ref[0]
    bim = bim_ref[0]
    bcat_ref[0, :, 0:SSM_FLAT] = (cr * bre - ci * bim).astype(BF16)
    bcat_ref[0, :, SSM_FLAT:2 * SSM_FLAT] = (cr * bim + ci * bre).astype(BF16)
    tab_ref[0, 0] = jnp.broadcast_to(br, (8, SSM_FLAT))
    tab_ref[0, 1] = jnp.broadcast_to(bi, (8, SSM_FLAT))


def _s5_prep(lam_re, lam_im, log_dt, b_re, b_im):
    eye = jnp.eye(SSM_GROUPS, dtype=F32)

    def block_diag_b(b):
        bt = jnp.transpose(b, (0, 1, 3, 2))
        return (bt[:, :, :, None, :] * eye[None, :, None, :, None]).reshape(DEPTH, SSM_WIDTH, SSM_FLAT)

    flat = lambda a: a.reshape(DEPTH, 1, SSM_FLAT)
    ldt = jnp.broadcast_to(log_dt[:, :, None], (DEPTH, SSM_GROUPS, SSM_STATE))
    vec = pl.BlockSpec((1, 1, SSM_FLAT), lambda l: (l, 0, 0))
    mat = pl.BlockSpec((1, SSM_WIDTH, SSM_FLAT), lambda l: (l, 0, 0))
    return pl.pallas_call(
        _s5prep_kernel,
        grid=(DEPTH,),
        in_specs=[vec, vec, vec, mat, mat],
        out_specs=[pl.BlockSpec((1, SSM_WIDTH, 2 * SSM_FLAT), lambda l: (l, 0, 0)),
                   pl.BlockSpec((1, 2, 8, SSM_FLAT), lambda l: (l, 0, 0, 0))],
        out_shape=[jax.ShapeDtypeStruct((DEPTH, SSM_WIDTH, 2 * SSM_FLAT), BF16),
                   jax.ShapeDtypeStruct((DEPTH, 2, 8, SSM_FLAT), F32)],
        compiler_params=_params("arbitrary"),
        name="s5_prep",
    )(flat(lam_re), flat(lam_im), flat(ldt), block_diag_b(b_re), block_diag_b(b_im))


def _s5_kernel(u_ref, gate_ref, x0r_ref, x0i_ref, bcat_ref, cre_ref, cim_ref, tab_ref, d_ref,
               wglu_ref, bglu_ref, o_ref, sr_ref, si_ref, xr_scr, xi_scr, cr_scr, ci_scr,
               *, TT, B, last_tile, last_t):
    j = pl.program_id(0)
    NBG, NL = xr_scr.shape[0], xr_scr.shape[1]
    split = lambda a: jnp.stack([a[:, LANES * i:LANES * (i + 1)] for i in range(NL)], axis=0)
    merge = lambda a: jnp.concatenate([a[i] for i in range(NL)], axis=1)

    @pl.when(j == 0)
    def _():
        xr_scr[...] = jnp.zeros(xr_scr.shape, F32)
        xi_scr[...] = jnp.zeros(xi_scr.shape, F32)
        for g in range(NBG):
            cr_scr[g] = split(x0r_ref[g])
            ci_scr[g] = split(x0i_ref[g])

    u = u_ref[...].reshape(B * TT, SSM_WIDTH)
    bu = _dot(u, bcat_ref[...])
    for b in range(B):
        g, s = divmod(b, 8)
        rows = pl.ds(s, TT, stride=8)
        for i in range(NL):
            xr_scr[g, i, rows, :] = bu[b * TT:(b + 1) * TT, LANES * i:LANES * (i + 1)]
            xi_scr[g, i, rows, :] = bu[b * TT:(b + 1) * TT, SSM_FLAT + LANES * i:SSM_FLAT + LANES * (i + 1)]

    lam_r = split(tab_ref[0])
    lam_i = split(tab_ref[1])

    def step(t, carry):
        rows = pl.ds(pl.multiple_of(t * 8, 8), 8)
        out = []
        for g in range(NBG):
            cr, ci = carry[2 * g], carry[2 * g + 1]
            nr = lam_r * cr - lam_i * ci + xr_scr[g, :, rows, :]
            ni = lam_r * ci + lam_i * cr + xi_scr[g, :, rows, :]
            xr_scr[g, :, rows, :] = nr
            xi_scr[g, :, rows, :] = ni
            out += [nr, ni]
        return tuple(out)

    init = tuple(ref[g] for g in range(NBG) for ref in (cr_scr, ci_scr))
    fin = lax.fori_loop(0, TT, step, init, unroll=min(TT, 8))
    for g in range(NBG):
        cr_scr[g] = fin[2 * g]
        ci_scr[g] = fin[2 * g + 1]

    @pl.when(j == last_tile)
    def _():
        rows = slice(last_t * 8, last_t * 8 + 8)
        for g in range(NBG):
            sr_ref[g] = merge(xr_scr[g, :, rows, :])
            si_ref[g] = merge(xi_scr[g, :, rows, :])

    def gather(scr):
        per_b = []
        for b in range(B):
            g, s = divmod(b, 8)
            rows = pl.ds(s, TT, stride=8)
            per_b.append(jnp.concatenate([scr[g, i, rows, :] for i in range(NL)], axis=1))
        return jnp.concatenate(per_b, axis=0)

    y = _dot(gather(xr_scr), cre_ref[...]) - _dot(gather(xi_scr), cim_ref[...]) + d_ref[...] * u
    z = jax.nn.gelu(y)
    out = z * jax.nn.sigmoid(_dot(z, wglu_ref[...]) + bglu_ref[...])
    out = out * _silu(gate_ref[...].reshape(B * TT, SSM_WIDTH))
    o_ref[...] = out.reshape(B, TT, SSM_WIDTH).astype(o_ref.dtype)


def _s5(p, x0r, x0i, bcat, cre, cim, tab, d_skip, w_glu, b_glu, *, TT, t_valid):
    B, T, _ = p["c_u"].shape
    NBG = x0r.shape[0]
    NL = SSM_FLAT // LANES
    last = t_valid - 1
    row = pl.BlockSpec((B, TT, SSM_WIDTH), lambda j: (0, j, 0))
    st = pl.BlockSpec((NBG, 8, SSM_FLAT), lambda j: (0, 0, 0))
    full = lambda a: pl.BlockSpec(a.shape, lambda j: (0,) * a.ndim)
    d2 = d_skip.reshape(1, SSM_WIDTH)
    bg2 = b_glu.reshape(1, SSM_WIDTH)
    state = jax.ShapeDtypeStruct((NBG, 8, SSM_FLAT), F32)
    return pl.pallas_call(
        functools.partial(_s5_kernel, TT=TT, B=B, last_tile=last // TT, last_t=last % TT),
        grid=(T // TT,),
        in_specs=[row, row, st, st, full(bcat), full(cre), full(cim), full(tab), full(d2),
                  full(w_glu), full(bg2)],
        out_specs=[row, st, st],
        out_shape=[jax.ShapeDtypeStruct((B, T, SSM_WIDTH), MIX_DTYPE), state, state],
        scratch_shapes=[pltpu.VMEM((NBG, NL, TT * 8, LANES), F32)] * 2
        + [pltpu.VMEM((NBG, NL, 8, LANES), F32)] * 2,
        compiler_params=_params("arbitrary"),
        name="s5_mix",
    )(p["c_u"], p["c_gate"], x0r, x0i, bcat, cre, cim, tab, d2, w_glu, bg2)


def _out_kernel(oa_ref, ob_ref, oc_ref, x_ref, gate_ref, g_ref, w_ref, o_ref):
    a1 = GLA_WIDTH
    a2 = GLA_WIDTH + SWA_WIDTH
    if len(ob_ref.shape) == 4:
        ob = jnp.concatenate([ob_ref[0, hp] for hp in range(ob_ref.shape[1])], axis=1)
    else:
        ob = ob_ref[0]
    y = (jnp.dot(oa_ref[0].astype(BF16), w_ref[0, 0:a1, :], preferred_element_type=F32)
         + jnp.dot(ob.astype(BF16), w_ref[0, a1:a2, :], preferred_element_type=F32)
         + jnp.dot(oc_ref[0].astype(BF16), w_ref[0, a2:, :], preferred_element_type=F32))
    ms = jnp.mean(y * y, axis=-1, keepdims=True)
    yn = y * lax.rsqrt(ms + NORM_EPS) * g_ref[...]
    o_ref[0] = x_ref[0] + gate_ref[0] * yn


def _out_proj(oa, ob, oc, x, gate, g_post, w_out, *, tm, layer):
    nb, T, _ = x.shape
    R = gate.shape[1]
    row = lambda w: pl.BlockSpec((1, tm, w), lambda b, j: (b, j, 0))
    if R == 1:
        gate_spec = pl.BlockSpec((1, 1, D_MODEL), lambda b, j: (b, 0, 0))
    else:
        gate_spec = row(D_MODEL)
    if ob.ndim == 4:
        ob_spec = pl.BlockSpec((1, ob.shape[1], tm, LANES), lambda b, j: (b, 0, j, 0))
    else:
        ob_spec = row(SWA_WIDTH)
    return pl.pallas_call(
        _out_kernel,
        grid=(nb, T // tm),
        in_specs=[row(GLA_WIDTH), ob_spec, row(SSM_WIDTH), row(D_MODEL), gate_spec,
                  pl.BlockSpec((1, D_MODEL), lambda b, j: (0, 0)),
                  pl.BlockSpec((1, D_MODEL, D_MODEL), lambda b, j: (layer, 0, 0))],
        out_specs=row(D_MODEL),
        out_shape=jax.ShapeDtypeStruct((nb, T, D_MODEL), F32),
        compiler_params=_params("arbitrary", "arbitrary"),
        name="out_proj",
    )(oa, ob, oc, x, gate, g_post.reshape(1, D_MODEL), w_out)


def _rope_tables(pos):
    half = ROPE_DIMS // 2
    inv = ROPE_THETA ** (-jnp.arange(half, dtype=F32) * (2.0 / ROPE_DIMS))
    ang = pos.astype(F32)[:, None] * inv[None, :]
    cos, sin = jnp.cos(ang), jnp.sin(ang)
    n = pos.shape[0]
    rest = SWA_HD - ROPE_DIMS
    head = lambda a, b, fill: jnp.concatenate([a, b, jnp.full((n, rest), fill, F32)], axis=1)
    zero = jnp.zeros((n, half), F32)
    reps = LANES // SWA_HD
    return (jnp.tile(head(cos, cos, 1.0), (1, reps)),
            jnp.tile(head(-sin, zero, 0.0), (1, reps)),
            jnp.tile(head(zero, sin, 0.0), (1, reps)))


def _rearrange_w_in(w_in):
    GK = GLA_HEADS * GLA_DK
    o = np.cumsum([0, GK, GK, GLA_WIDTH, GLA_RANK, GLA_WIDTH, SWA_WIDTH, SWA_WIDTH, SWA_WIDTH,
                   SWA_WIDTH, SSM_WIDTH, SSM_WIDTH])
    piece = lambda i: w_in[:, :, o[i]:o[i + 1]]
    lr = jnp.pad(piece(3), ((0, 0), (0, 0), (0, LANES - GLA_RANK)))
    cat = jnp.concatenate([piece(0), piece(1), piece(2), piece(4), piece(5), piece(6), piece(7),
                           piece(8), piece(9), piece(10), lr], axis=-1)
    return cat.astype(BF16)


def kernel(x_prompt, x_sample, c_prompt, c_sample, state_gla, cache_swa_k, cache_swa_v, state_ssm_re, state_ssm_im, w_ada, b_ada, g_pre, g_post, w_in, w_gla_lr, b_gla_lr, g_gla, ssm_lambda_re, ssm_lambda_im, ssm_log_dt, ssm_b_re, ssm_b_im, ssm_c_re, ssm_c_im, ssm_d, w_glu, b_glu, w_out):
    Bp, Tp, _ = x_prompt.shape
    Bs, Ts, _ = x_sample.shape
    PAD = SAMPLE_PAD
    Lc = cache_swa_k.shape[2]

    w_cat = _rearrange_w_in(w_in)
    w_out_b = w_out.astype(BF16)
    w_glu_b = w_glu.astype(BF16)
    w_lr_pad = jnp.pad(w_gla_lr, ((0, 0), (0, LANES - GLA_RANK), (0, 0)))
    eye = jnp.eye(SSM_GROUPS, dtype=F32)

    def block_diag_c(c):
        ct = jnp.transpose(c, (0, 1, 3, 2))
        return (ct[:, :, :, None, :] * eye[None, :, None, :, None]).reshape(DEPTH, SSM_FLAT, SSM_WIDTH)

    cre = block_diag_c(ssm_c_re).astype(BF16)
    cim = block_diag_c(ssm_c_im).astype(BF16)
    bcat, tab = _s5_prep(ssm_lambda_re, ssm_lambda_im, ssm_log_dt, ssm_b_re, ssm_b_im)

    n_c = Bp + Bs
    c_rows = -(-n_c // 8) * 8
    c_all = jnp.pad(jnp.concatenate([c_prompt, c_sample], axis=0), ((0, c_rows - n_c), (0, 0)))
    mod = _ada_mod(c_all, w_ada, b_ada)

    tabs_p = _rope_tables(jnp.arange(Tp))
    tabs_s = _rope_tables(PAST_LEN + jnp.arange(Bs * PAD) % PAD)

    xp = x_prompt
    xs = jnp.pad(x_sample, ((0, 0), (0, PAD - Ts), (0, 0))).reshape(1, Bs * PAD, D_MODEL)
    seq_last = lambda a: jnp.transpose(a, (0, 1, 3, 4, 2))
    seq_first = lambda a: jnp.transpose(a, (0, 1, 4, 2, 3))
    cache_k = seq_last(cache_swa_k)
    cache_v = seq_last(cache_swa_v)
    zero_gla = jnp.zeros((Bp, GLA_HEADS, GLA_DK, GLA_DV), F32)
    groups = lambda n: -(-n // 8)
    ssm_in = lambda a: jnp.pad(a.reshape(a.shape[0], SSM_FLAT),
                               ((0, groups(a.shape[0]) * 8 - a.shape[0]), (0, 0))
                               ).reshape(groups(a.shape[0]), 8, SSM_FLAT)
    zero_ssm = jnp.zeros((groups(Bp), 8, SSM_FLAT), F32)

    acc = [[] for _ in range(6)]
    tails = None
    shifted = None
    for l in range(DEPTH):
        def chunk(rows, k):
            return mod[l, rows, k * D_MODEL:(k + 1) * D_MODEL]

        pr = slice(0, Bp)
        shift, scale, gate = (chunk(pr, k).reshape(Bp, 1, D_MODEL) for k in range(3))
        p = _proj(xp, scale, shift, g_pre[l], w_cat, tabs_p, tm=512, layer=l, tail=(l, tails))
        tails = (p["k_tail"], p["v_tail"])
        oa, gla_p = _gla(p, w_lr_pad[l], b_gla_lr[l], g_gla[l], zero_gla, TT=1024, C=GLA_CHUNK, t_valid=Tp)
        ob = _swa_prompt(p)
        oc, re_p, im_p = _s5(p, zero_ssm, zero_ssm, bcat[l], cre[l], cim[l], tab[l], ssm_d[l],
                             w_glu_b[l], b_glu[l], TT=128, t_valid=Tp)
        xp = _out_proj(oa, ob, oc, xp, gate, g_post[l], w_out_b, tm=1024, layer=l)

        sr = slice(Bp, Bp + Bs)
        shift, scale, gate = (jnp.repeat(chunk(sr, k), PAD, axis=0).reshape(1, Bs * PAD, D_MODEL)
                              for k in range(3))
        qf = _proj(xs, scale, shift, g_pre[l], w_cat, tabs_s, tm=Bs * PAD, layer=l)
        q = {n: a.reshape(Bs, PAD, a.shape[-1]) for n, a in qf.items()}
        oa, gla_s = _gla(qf, w_lr_pad[l], b_gla_lr[l], g_gla[l], state_gla[l], TT=Bs * PAD, C=PAD,
                         t_valid=Ts, own_state=True)
        ob, k_s, v_s = _swa_sample(q, cache_k, cache_v, l, Ts, shifted)
        shifted = (k_s, v_s)
        oc, re_s, im_s = _s5(q, ssm_in(state_ssm_re[l]), ssm_in(state_ssm_im[l]), bcat[l], cre[l],
                             cim[l], tab[l], ssm_d[l], w_glu_b[l], b_glu[l], TT=PAD, t_valid=Ts)
        flat = lambda a: a.reshape(1, Bs * PAD, a.shape[-1])
        xs = _out_proj(flat(oa), flat(ob), flat(oc), xs, gate, g_post[l], w_out_b, tm=Bs * PAD, layer=l)

        for i, a in enumerate((gla_p, gla_s, re_p, im_p, re_s, im_s)):
            acc[i].append(a)

    st = [jnp.stack(a) for a in acc]
    ssm = lambda a, n: a.reshape(DEPTH, -1, SSM_GROUPS, SSM_STATE)[:, :n]
    y_sample = xs.reshape(Bs, PAD, D_MODEL)[:, :Ts]
    return (xp, y_sample, st[0], st[1], seq_first(tails[0]), seq_first(tails[1]),
            seq_first(shifted[0]), seq_first(shifted[1]),
            ssm(st[2], Bp), ssm(st[3], Bp), ssm(st[4], Bs), ssm(st[5], Bs))
```
